```python
import jax, jax.numpy as jnp
from jax import lax
import numpy as np

D_MODEL = 1024
BATCH = 2
SEQ = 16384
DEPTH = 2
DEC_BATCH = 32
DEC_SEQ = 64
PAST_LEN = 2048

CHUNK = 64
D_CONV = 1024
CONV_W = 3
N_HEADS = 16
HEAD_DIM = 64
D_RWKV = N_HEADS * HEAD_DIM
DECAY_LORA = 64
AAA_LORA = 64
GATE_LORA = 128
D_SHIFT = 3 * D_RWKV + DECAY_LORA + AAA_LORA + GATE_LORA
D_PROJ = 2 * D_MODEL + 3 * D_CONV + D_SHIFT
D_FF = 2816
N_EXPERTS = 8
TOP_K = 2
D_FF_EXPERT = 3584
N_DENSE = (DEPTH + 1) // 2
N_MOE = DEPTH // 2
RMS_EPS = 1e-5
GN_EPS = 64e-5
L2_EPS = 1e-12

kernel_name = "hybrid_shortconv_rwkv7_stream_step"


def rmsnorm(x, g):
    xf = x.astype(jnp.float32)
    y = xf * lax.rsqrt(jnp.mean(xf * xf, axis=-1, keepdims=True) + RMS_EPS)
    return (y * g.astype(jnp.float32)).astype(x.dtype)


def short_conv(pre, conv_state, conv_w):
    T = pre.shape[1]
    padded = jnp.concatenate([conv_state.astype(pre.dtype), pre], axis=1)
    out = sum(padded[:, j:j + T] * conv_w[j] for j in range(CONV_W))
    return out, padded[:, -(CONV_W - 1):]


def wkv_scan(r, decay, k, v, kk, a, state):
    def step(S, inp):
        r_t, w_t, k_t, v_t, kk_t, a_t = inp
        sa = jnp.einsum('bhij,bhj->bhi', S, kk_t)
        S = (S * w_t[:, :, None, :]
             - sa[..., None] * (kk_t * a_t)[:, :, None, :]
             + v_t[..., None] * k_t[:, :, None, :])
        return S, jnp.einsum('bhij,bhj->bhi', S, r_t)
    xs = tuple(jnp.swapaxes(t, 0, 1) for t in (r, decay, k, v, kk, a))
    S, y = lax.scan(step, state, xs)
    return jnp.swapaxes(y, 0, 1), S


def rwkv7_branch(ps, shift_state, wkv_state, mu, w0, w2, a0, a2, g2, k_k, k_a, r_k, lnx_w, lnx_b):
    B, T, _ = ps.shape
    f32 = jnp.float32
    prev = jnp.concatenate([shift_state[:, None].astype(ps.dtype), ps[:, :-1]], axis=1)
    xm = ps + (prev - ps) * mu
    cuts = [D_RWKV, 2 * D_RWKV, 3 * D_RWKV, 3 * D_RWKV + DECAY_LORA, 3 * D_RWKV + DECAY_LORA + AAA_LORA]
    r, k, v, pw, pa, pg = jnp.split(xm, cuts, axis=-1)
    logw = -jax.nn.softplus(-(w0 + jnp.tanh(pw) @ w2).astype(f32)) - 0.5
    decay = jnp.exp(-jnp.exp(logw))
    a = jax.nn.sigmoid((a0 + pa @ a2).astype(f32))
    g = jax.nn.sigmoid(pg) @ g2
    heads = lambda t: t.astype(f32).reshape(B, T, N_HEADS, HEAD_DIM)
    kf = k.astype(f32)
    kk = heads(kf * k_k)
    kk = kk / jnp.maximum(jnp.sqrt(jnp.sum(kk * kk, axis=-1, keepdims=True)), L2_EPS)
    kf = kf * (1.0 + (a - 1.0) * k_a)
    rh, kh, vh = heads(r), heads(kf), heads(v)
    y, S = wkv_scan(rh, heads(decay), kh, vh, kk, heads(a), wkv_state.astype(f32))
    mean = jnp.mean(y, axis=-1, keepdims=True)
    var = jnp.mean(jnp.square(y - mean), axis=-1, keepdims=True)
    yn = ((y - mean) * lax.rsqrt(var + GN_EPS)).reshape(B, T, D_RWKV) * lnx_w + lnx_b
    bonus = jnp.sum(rh * kh * r_k.astype(f32).reshape(N_HEADS, HEAD_DIM), axis=-1, keepdims=True) * vh
    out = (yn + bonus.reshape(B, T, D_RWKV)).astype(ps.dtype) * g
    return out, ps[:, -1], S.astype(wkv_state.dtype)


def mixer_layer(x, conv_state, shift_state, wkv_state, norm_g, w_in, conv_w, mu, w0, w2,
                a0, a2, g2, k_k, k_a, r_k, lnx_w, lnx_b, w_out):
    xn = rmsnorm(x, norm_g)
    p = xn @ w_in
    cuts = [D_MODEL, 2 * D_MODEL, 2 * D_MODEL + D_CONV, 2 * D_MODEL + 2 * D_CONV, 2 * D_MODEL + 3 * D_CONV]
    z_a, z_b, c_b, c_c, c_h, ps = jnp.split(p, cuts, axis=-1)
    conv_out, new_conv = short_conv(c_c * c_h, conv_state, conv_w)
    y_a = c_b * conv_out
    y_b, new_shift, new_wkv = rwkv7_branch(ps, shift_state, wkv_state, mu, w0, w2, a0, a2, g2,
                                           k_k, k_a, r_k, lnx_w, lnx_b)
    merged = jax.nn.sigmoid(z_a) * y_a + jax.nn.sigmoid(z_b) * y_b
    return x + merged @ w_out, new_conv, new_shift, new_wkv


def swiglu(x, w_gate, w_up, w_down):
    return (jax.nn.silu(x @ w_gate) * (x @ w_up)) @ w_down


def moe_swiglu(x, w_router, w_gate, w_up, w_down):
    logits = (x @ w_router).astype(jnp.float32)
    top_v, top_i = lax.top_k(logits, TOP_K)
    top_w = jax.nn.softmax(top_v, axis=-1)
    gates = jnp.sum(jax.nn.one_hot(top_i, N_EXPERTS, dtype=jnp.float32) * top_w[..., None], axis=-2)
    gates = gates.astype(x.dtype)
    out = jnp.zeros_like(x)
    for e in range(N_EXPERTS):
        out = out + gates[..., e:e + 1] * swiglu(x, w_gate[e], w_up[e], w_down[e])
    return out


def trunk(x, conv_st, shift_st, wkv_st, mix_params, ffn_norm, dense_params, moe_params, final_norm):
    convs, shifts, wkvs = [], [], []
    for l in range(DEPTH):
        x, c, s, w = mixer_layer(x, conv_st[l], shift_st[l], wkv_st[l], *[p[l] for p in mix_params])
        h = rmsnorm(x, ffn_norm[l])
        if l % 2 == 0:
            x = x + swiglu(h, *[p[l // 2] for p in dense_params])
        else:
            x = x + moe_swiglu(h, *[p[l // 2] for p in moe_params])
        convs.append(c)
        shifts.append(s)
        wkvs.append(w)
    return rmsnorm(x, final_norm), jnp.stack(convs), jnp.stack(shifts), jnp.stack(wkvs)


def setup_inputs(seed: int = 0) -> dict:
    key = jax.random.key(seed)
    ks = iter(jax.random.split(key, 40))
    nrm = lambda shape, s: s * jax.random.normal(next(ks), shape, jnp.float32)
    L = DEPTH
    return {
        'x_prompt': nrm((BATCH, SEQ, D_MODEL), 1.0),
        'x_sample': nrm((DEC_BATCH, DEC_SEQ, D_MODEL), 1.0),
        'state_conv': nrm((L, DEC_BATCH, CONV_W - 1, D_CONV), 0.5),
        'state_shift': nrm((L, DEC_BATCH, D_SHIFT), 0.5),
        'state_wkv': nrm((L, DEC_BATCH, N_HEADS, HEAD_DIM, HEAD_DIM), 0.3),
        'mix_norm': 1.0 + nrm((L, D_MODEL), 0.05),
        'w_in': nrm((L, D_MODEL, D_PROJ), D_MODEL ** -0.5),
        'conv_w': nrm((L, CONV_W, D_CONV), CONV_W ** -0.5),
        'shift_mu': jax.random.uniform(next(ks), (L, D_SHIFT), jnp.float32),
        'decay_w0': -0.5 + nrm((L, D_RWKV), 0.5),
        'decay_w2': nrm((L, DECAY_LORA, D_RWKV), 0.5 * DECAY_LORA ** -0.5),
        'aaa_a0': nrm((L, D_RWKV), 0.1),
        'aaa_a2': nrm((L, AAA_LORA, D_RWKV), 0.5 * AAA_LORA ** -0.5),
        'gate_g2': nrm((L, GATE_LORA, D_RWKV), GATE_LORA ** -0.5),
        'key_k': 0.85 + nrm((L, D_RWKV), 0.05),
        'key_a': 1.0 + nrm((L, D_RWKV), 0.05),
        'bonus_r_k': nrm((L, D_RWKV), 0.1),
        'lnx_w': 1.0 + nrm((L, D_RWKV), 0.05),
        'lnx_b': nrm((L, D_RWKV), 0.02),
        'w_out': nrm((L, D_MODEL, D_MODEL), D_MODEL ** -0.5),
        'ffn_norm': 1.0 + nrm((L, D_MODEL), 0.05),
        'ffn_w_gate': nrm((N_DENSE, D_MODEL, D_FF), D_MODEL ** -0.5),
        'ffn_w_up': nrm((N_DENSE, D_MODEL, D_FF), D_MODEL ** -0.5),
        'ffn_w_down': nrm((N_DENSE, D_FF, D_MODEL), D_FF ** -0.5),
        'router_w': nrm((N_MOE, D_MODEL, N_EXPERTS), D_MODEL ** -0.5),
        'moe_w_gate': nrm((N_MOE, N_EXPERTS, D_MODEL, D_FF_EXPERT), D_MODEL ** -0.5),
        'moe_w_up': nrm((N_MOE, N_EXPERTS, D_MODEL, D_FF_EXPERT), D_MODEL ** -0.5),
        'moe_w_down': nrm((N_MOE, N_EXPERTS, D_FF_EXPERT, D_MODEL), D_FF_EXPERT ** -0.5),
        'final_norm': 1.0 + nrm((D_MODEL,), 0.05),
    }


def reference(x_prompt, x_sample, state_conv, state_shift, state_wkv, mix_norm, w_in, conv_w,
              shift_mu, decay_w0, decay_w2, aaa_a0, aaa_a2, gate_g2, key_k, key_a, bonus_r_k,
              lnx_w, lnx_b, w_out, ffn_norm, ffn_w_gate, ffn_w_up, ffn_w_down, router_w,
              moe_w_gate, moe_w_up, moe_w_down, final_norm):
    mix_params = (mix_norm, w_in, conv_w, shift_mu, decay_w0, decay_w2, aaa_a0, aaa_a2, gate_g2,
                  key_k, key_a, bonus_r_k, lnx_w, lnx_b, w_out)
    dense_params = (ffn_w_gate, ffn_w_up, ffn_w_down)
    moe_params = (router_w, moe_w_gate, moe_w_up, moe_w_down)
    b = x_prompt.shape[0]
    zero_conv = jnp.zeros((DEPTH, b, CONV_W - 1, D_CONV), state_conv.dtype)
    zero_shift = jnp.zeros((DEPTH, b, D_SHIFT), state_shift.dtype)
    zero_wkv = jnp.zeros((DEPTH, b, N_HEADS, HEAD_DIM, HEAD_DIM), state_wkv.dtype)
    y_prompt, conv_p, shift_p, wkv_p = trunk(x_prompt, zero_conv, zero_shift, zero_wkv, mix_params,
                                             ffn_norm, dense_params, moe_params, final_norm)
    y_sample, conv_s, shift_s, wkv_s = trunk(x_sample, state_conv, state_shift, state_wkv, mix_params,
                                             ffn_norm, dense_params, moe_params, final_norm)
    return (y_prompt, y_sample, conv_p, shift_p, wkv_p, conv_s, shift_s, wkv_s)
```

```python
import functools

import jax
import jax.numpy as jnp
from jax import lax
from jax.experimental import pallas as pl
from jax.experimental.pallas import tpu as pltpu

F32 = jnp.float32
BF16 = jnp.bfloat16

D_MODEL = 1024
N_HEADS = 16
HEAD_DIM = 64
D_RWKV = N_HEADS * HEAD_DIM
D_CONV = 1024
CONV_W = 3
DECAY_LORA = 64
AAA_LORA = 64
GATE_LORA = 128
D_SHIFT = 3 * D_RWKV + DECAY_LORA + AAA_LORA + GATE_LORA
D_PROJ = 2 * D_MODEL + 3 * D_CONV + D_SHIFT
N_EXPERTS = 8
RMS_EPS = 1e-5
GN_EPS = 64e-5
L2_EPS = 1e-12

LANES = 128
SUBLANES = 8
CHUNK = 64
HEADS_PER_GROUP = LANES // HEAD_DIM
N_GROUPS = N_HEADS // HEADS_PER_GROUP
VMEM_LIMIT = 56 * 1024 * 1024

_ZA, _ZB, _CB, _CC, _CHH, _PS = (0, D_MODEL, 2 * D_MODEL, 2 * D_MODEL + D_CONV,
                                 2 * D_MODEL + 2 * D_CONV, 2 * D_MODEL + 3 * D_CONV)
_R, _K, _V, _WA, _G = 0, D_RWKV, 2 * D_RWKV, 3 * D_RWKV, 3 * D_RWKV + DECAY_LORA + AAA_LORA


def _params(semantics):
    return pltpu.CompilerParams(dimension_semantics=semantics, vmem_limit_bytes=VMEM_LIMIT)


def _rms(x, g):
    ms = jnp.mean(x * x, axis=-1, keepdims=True)
    return x * lax.rsqrt(ms + RMS_EPS) * g


def _split(x):
    hi = x.astype(BF16)
    lo = (x - hi.astype(F32)).astype(BF16)
    return hi, lo


_NN = (((1,), (0,)), ((), ()))
_NT = (((1,), (1,)), ((), ()))


def _dot(a, b, dims=_NN):
    return lax.dot_general(a, b, dims, preferred_element_type=F32)


def _mm1(a, b, dims=_NN):
    return _dot(a.astype(BF16), b.astype(BF16), dims)


def _mm3(a, b, dims=_NN):
    a1, a2 = _split(a)
    b1, b2 = _split(b)
    return _dot(a1, b1, dims) + (_dot(a1, b2, dims) + _dot(a2, b1, dims))


def _mm_exact_rhs(a, b):
    a1, a2 = _split(a)
    bb = b.astype(BF16)
    return _dot(a1, bb) + _dot(a2, bb)


def _norm_proj_kernel(x_ref, g_ref, w_ref, o_ref, xn_ref):
    @pl.when(pl.program_id(1) == 0)
    def _():
        xn_ref[...] = _rms(x_ref[...], g_ref[...]).astype(BF16)

    o_ref[...] = _dot(xn_ref[...], w_ref[...])


def _norm_proj(x2d, g, w_bf16, tm, tn):
    n, d = x2d.shape
    dp = w_bf16.shape[1]
    return pl.pallas_call(
        _norm_proj_kernel,
        out_shape=jax.ShapeDtypeStruct((n, dp), F32),
        grid=(n // tm, dp // tn),
        in_specs=[pl.BlockSpec((tm, d), lambda i, j: (i, 0)),
                  pl.BlockSpec((1, d), lambda i, j: (0, 0)),
                  pl.BlockSpec((d, tn), lambda i, j: (0, j))],
        out_specs=pl.BlockSpec((tm, tn), lambda i, j: (i, j)),
        scratch_shapes=[pltpu.VMEM((tm, d), BF16)],
        compiler_params=_params(("arbitrary", "arbitrary")),
        name="norm_proj",
    )(x2d, g, w_bf16)


def _block_diag(x, head_of_lane):
    return jnp.concatenate([jnp.where(head_of_lane == h, x, 0.0) for h in range(HEADS_PER_GROUP)],
                           axis=0)


def _wkv_chunk_group(mm, r, k, v, kkn, a, ld, cum, s_prev, masks):
    head_of_lane, causal2, bd_mask = masks
    bd = functools.partial(_block_diag, head_of_lane=head_of_lane)
    cend = cum[CHUNK - 1:CHUNK, :]
    w_prev = jnp.exp(cum - ld)
    w_t = jnp.exp(cum)
    w_inv = jnp.exp(-cum)
    w_rest = jnp.exp(cend - cum)
    b = kkn * a
    a_t = -kkn * w_prev
    r_t = r * w_t
    b_t = b * w_inv
    k_t = k * w_inv
    b_h = b * w_rest
    k_h = k * w_rest

    lhs2 = jnp.concatenate([a_t, r_t], axis=0)
    g_b = jnp.where(causal2, mm(lhs2, bd(b_t), _NT), 0.0)
    g_k = jnp.where(causal2, mm(lhs2, bd(k_t), _NT), 0.0)
    g_s = mm(lhs2, s_prev, _NT)
    l_ab, l_rb = g_b[:CHUNK], g_b[CHUNK:]
    l_ak, l_rk = g_k[:CHUNK], g_k[CHUNK:]
    bd_v = bd(v)

    u = g_s[:CHUNK] + mm(l_ak, bd_v)
    pw = l_ab
    n_steps = CHUNK.bit_length() - 1
    for i in range(n_steps):
        u = u + mm(pw, bd(u))
        if i + 1 < n_steps:
            pw = mm(pw, bd(pw))

    y = g_s[CHUNK:] + mm(l_rb, bd(u)) + mm(l_rk, bd_v)
    uv_t = jnp.concatenate([u, v], axis=0).T
    upd = mm(uv_t, jnp.concatenate([b_h, k_h], axis=0))
    s_new = s_prev * jnp.exp(cend) + jnp.where(bd_mask, upd, 0.0)
    return y, s_new


def _mixer_kernel(p_ref, conv0_ref, shift0_ref, wkv0_ref, convw_ref, mu_ref, vec_ref, wwa_ref,
                  g2_ref, merged_ref, nconv_ref, nshift_ref, nwkv_ref, cbuf, sbuf, state,
                  *, n_chunks, mm):
    t = pl.program_id(1)

    @pl.when(t == 0)
    def _():
        cbuf[0:SUBLANES, :] = jnp.zeros((SUBLANES, D_CONV), F32)
        cbuf[SUBLANES - (CONV_W - 1):SUBLANES, :] = conv0_ref[0]
        sbuf[0:SUBLANES, :] = jnp.zeros((SUBLANES, D_SHIFT), F32)
        sbuf[SUBLANES - 1:SUBLANES, :] = shift0_ref[0]
        state[...] = wkv0_ref[0]

    c0 = SUBLANES
    ch = p_ref[:, _CC:_CC + D_CONV] * p_ref[:, _CHH:_CHH + D_CONV]
    cbuf[c0:c0 + CHUNK, :] = ch
    conv = (cbuf[c0 - 2:c0 - 2 + CHUNK, :] * convw_ref[0:1, :]
            + cbuf[c0 - 1:c0 - 1 + CHUNK, :] * convw_ref[1:2, :]
            + ch * convw_ref[2:3, :])
    y_a = jax.nn.sigmoid(p_ref[:, _ZA:_ZA + D_MODEL]) * (p_ref[:, _CB:_CB + D_CONV] * conv)
    new_conv = cbuf[c0 + CHUNK - (CONV_W - 1):c0 + CHUNK, :]
    cbuf[0:SUBLANES, :] = cbuf[CHUNK:CHUNK + SUBLANES, :]

    ps = p_ref[:, _PS:_PS + D_SHIFT]
    sbuf[c0:c0 + CHUNK, :] = ps
    prev = sbuf[c0 - 1:c0 - 1 + CHUNK, :]
    xm = ps + (prev - ps) * mu_ref[...]
    new_shift = sbuf[c0 + CHUNK - 1:c0 + CHUNK, :]
    sbuf[0:SUBLANES, :] = sbuf[CHUNK:CHUNK + SUBLANES, :]

    w0, a0, k_k, k_a, r_k, lnx_w, lnx_b = (vec_ref[i:i + 1, :] for i in range(7))

    lane = lax.broadcasted_iota(jnp.int32, (CHUNK, LANES), 1)
    wa_in = xm[:, _WA:_WA + LANES]
    wa_in = jnp.where(lane < DECAY_LORA, jnp.tanh(wa_in), wa_in)
    wa = _mm1(wa_in, wwa_ref[...])
    gate = _mm1(jax.nn.sigmoid(xm[:, _G:_G + GATE_LORA]), g2_ref[...])
    ld = -jnp.exp(-0.5) * jax.nn.sigmoid(w0 + wa[:, :D_RWKV])
    a_all = jax.nn.sigmoid(a0 + wa[:, D_RWKV:])
    row = lax.broadcasted_iota(jnp.int32, (CHUNK, CHUNK), 0)
    col = lax.broadcasted_iota(jnp.int32, (CHUNK, CHUNK), 1)
    tri = jnp.where(col <= row, 1.0, 0.0).astype(BF16)
    ld1 = ld.astype(BF16)
    ld2 = (ld - ld1.astype(F32)).astype(BF16)
    ld3 = (ld - ld1.astype(F32) - ld2.astype(F32)).astype(BF16)
    cum_all = _dot(tri, ld1) + (_dot(tri, ld2) + _dot(tri, ld3))

    head_of_lane = lane >> (HEAD_DIM.bit_length() - 1)
    row2 = lax.broadcasted_iota(jnp.int32, (2 * CHUNK, LANES), 0)
    lane2 = lax.broadcasted_iota(jnp.int32, (2 * CHUNK, LANES), 1)
    causal2 = (lane2 & (HEAD_DIM - 1)) < (row2 & (CHUNK - 1)) + (row2 >> (CHUNK.bit_length() - 1))
    rowl = lax.broadcasted_iota(jnp.int32, (LANES, LANES), 0)
    lanel = lax.broadcasted_iota(jnp.int32, (LANES, LANES), 1)
    bd_mask = (rowl >> (HEAD_DIM.bit_length() - 1)) == (lanel >> (HEAD_DIM.bit_length() - 1))
    seg_ones = jnp.where(bd_mask, 1.0, 0.0)
    masks = (head_of_lane, causal2, bd_mask)

    for g in range(N_GROUPS):
        sl = slice(g * LANES, (g + 1) * LANES)
        r = xm[:, _R + g * LANES:_R + (g + 1) * LANES]
        k = xm[:, _K + g * LANES:_K + (g + 1) * LANES]
        v = xm[:, _V + g * LANES:_V + (g + 1) * LANES]
        a = a_all[:, sl]
        kk = k * k_k[:, sl]
        norm = jnp.sqrt(_mm_exact_rhs(kk * kk, seg_ones))
        kkn = kk / jnp.maximum(norm, L2_EPS)
        kf = k * (1.0 + (a - 1.0) * k_a[:, sl])
        y, s_new = _wkv_chunk_group(mm, r, kf, v, kkn, a, ld[:, sl], cum_all[:, sl], state[g], masks)
        state[g] = s_new
        mean = _mm_exact_rhs(y, seg_ones) * (1.0 / HEAD_DIM)
        dev = y - mean
        var = _mm_exact_rhs(dev * dev, seg_ones) * (1.0 / HEAD_DIM)
        yn = dev * lax.rsqrt(var + GN_EPS) * lnx_w[:, sl] + lnx_b[:, sl]
        bonus = _mm_exact_rhs(r * kf * r_k[:, sl], seg_ones) * v
        y_b = (yn + bonus) * gate[:, sl]
        z_b = p_ref[:, _ZB + g * LANES:_ZB + (g + 1) * LANES]
        merged_ref[:, sl] = (y_a[:, sl] + jax.nn.sigmoid(z_b) * y_b).astype(merged_ref.dtype)

    @pl.when(t == n_chunks - 1)
    def _():
        nconv_ref[0] = new_conv
        nshift_ref[0] = new_shift
        nwkv_ref[0] = state[...]


def _mixer(p2d, conv0, shift0, wkv0_bd, convw, mu, vecs, wwa, g2, batch, seq, mm):
    n_chunks = seq // CHUNK
    kern = functools.partial(_mixer_kernel, n_chunks=n_chunks, mm=mm)
    const = lambda b, t: (0, 0)
    per_b3 = lambda b, t: (b, 0, 0)
    per_b4 = lambda b, t: (b, 0, 0, 0)
    return pl.pallas_call(
        kern,
        out_shape=(jax.ShapeDtypeStruct((batch * seq, D_MODEL), BF16),
                   jax.ShapeDtypeStruct((batch, CONV_W - 1, D_CONV), F32),
                   jax.ShapeDtypeStruct((batch, 1, D_SHIFT), F32),
                   jax.ShapeDtypeStruct((batch, N_GROUPS, LANES, LANES), F32)),
        grid=(batch, n_chunks),
        in_specs=[pl.BlockSpec((CHUNK, D_PROJ), lambda b, t: (b * n_chunks + t, 0)),
                  pl.BlockSpec((1, CONV_W - 1, D_CONV), per_b3),
                  pl.BlockSpec((1, 1, D_SHIFT), per_b3),
                  pl.BlockSpec((1, N_GROUPS, LANES, LANES), per_b4),
                  pl.BlockSpec((CONV_W, D_CONV), const),
                  pl.BlockSpec((1, D_SHIFT), const),
                  pl.BlockSpec((SUBLANES, D_RWKV), const),
                  pl.BlockSpec((LANES, 2 * D_RWKV), const),
                  pl.BlockSpec((GATE_LORA, D_RWKV), const)],
        out_specs=(pl.BlockSpec((CHUNK, D_MODEL), lambda b, t: (b * n_chunks + t, 0)),
                   pl.BlockSpec((1, CONV_W - 1, D_CONV), per_b3),
                   pl.BlockSpec((1, 1, D_SHIFT), per_b3),
                   pl.BlockSpec((1, N_GROUPS, LANES, LANES), per_b4)),
        scratch_shapes=[pltpu.VMEM((CHUNK + SUBLANES, D_CONV), F32),
                        pltpu.VMEM((CHUNK + SUBLANES, D_SHIFT), F32),
                        pltpu.VMEM((N_GROUPS, LANES, LANES), F32)],
        compiler_params=_params(("arbitrary", "arbitrary")),
        name="mixer",
    )(p2d, conv0, shift0, wkv0_bd, convw, mu, vecs, wwa, g2)


def _silu_mul(gate, up):
    return gate * jax.nn.sigmoid(gate) * up


def _ffn_dense_kernel(x_ref, m_ref, wo_ref, g_ref, wg_ref, wu_ref, wd_ref, o_ref, h_ref):
    @pl.when(pl.program_id(1) == 0)
    def _():
        x1 = x_ref[...] + _dot(m_ref[...], wo_ref[...])
        o_ref[...] = x1
        h_ref[...] = _rms(x1, g_ref[...]).astype(BF16)

    h = h_ref[...]
    act = _silu_mul(_dot(h, wg_ref[...]), _dot(h, wu_ref[...])).astype(BF16)
    o_ref[...] += _dot(act, wd_ref[...])


def _ffn_dense(x2d, merged, wo, g, wg, wu, wd, tm, tf):
    n, d = x2d.shape
    f = wg.shape[1]
    return pl.pallas_call(
        _ffn_dense_kernel,
        out_shape=jax.ShapeDtypeStruct((n, d), F32),
        grid=(n // tm, f // tf),
        in_specs=[pl.BlockSpec((tm, d), lambda i, j: (i, 0)),
                  pl.BlockSpec((tm, d), lambda i, j: (i, 0)),
                  pl.BlockSpec((d, d), lambda i, j: (0, 0)),
                  pl.BlockSpec((1, d), lambda i, j: (0, 0)),
                  pl.BlockSpec((d, tf), lambda i, j: (0, j)),
                  pl.BlockSpec((d, tf), lambda i, j: (0, j)),
                  pl.BlockSpec((tf, d), lambda i, j: (j, 0))],
        out_specs=pl.BlockSpec((tm, d), lambda i, j: (i, 0)),
        scratch_shapes=[pltpu.VMEM((tm, d), BF16)],
        compiler_params=_params(("arbitrary", "arbitrary")),
        name="ffn_dense",
    )(x2d, merged, wo, g, wg, wu, wd)


def _top2_gates(logits):
    lane = lax.broadcasted_iota(jnp.int32, logits.shape, 1).astype(F32)
    neg = jnp.float32(-jnp.inf)
    lg = jnp.where(lane < N_EXPERTS, logits, neg)
    m1 = jnp.max(lg, axis=-1, keepdims=True)
    i1 = jnp.min(jnp.where(lg == m1, lane, float(LANES)), axis=-1, keepdims=True)
    lg2 = jnp.where(lane == i1, neg, lg)
    m2 = jnp.max(lg2, axis=-1, keepdims=True)
    i2 = jnp.min(jnp.where(lg2 == m2, lane, float(LANES)), axis=-1, keepdims=True)
    e2 = jnp.exp(m2 - m1)
    den = 1.0 + e2
    return jnp.where(lane == i1, 1.0 / den, 0.0) + jnp.where(lane == i2, e2 / den, 0.0)


def _ffn_moe_kernel(x_ref, m_ref, wo_ref, g_ref, rw_ref, wg_ref, wu_ref, wd_ref, fin_ref,
                    o_ref, h_ref, gates_ref, *, n_f):
    e = pl.program_id(1)
    j = pl.program_id(2)

    @pl.when((e == 0) & (j == 0))
    def _():
        x1 = x_ref[...] + _dot(m_ref[...], wo_ref[...])
        o_ref[...] = x1
        h = _rms(x1, g_ref[...])
        h_ref[...] = h.astype(BF16)
        gates_ref[...] = _top2_gates(_mm3(h, rw_ref[...]))

    h = h_ref[...]
    act = _silu_mul(_dot(h, wg_ref[0]), _dot(h, wu_ref[0])).astype(BF16)
    lane = lax.broadcasted_iota(jnp.int32, gates_ref.shape, 1)
    gate_e = jnp.sum(jnp.where(lane == e, gates_ref[...], 0.0), axis=-1, keepdims=True)
    o_ref[...] += gate_e * _dot(act, wd_ref[0])

    @pl.when((e == N_EXPERTS - 1) & (j == n_f - 1))
    def _():
        o_ref[...] = _rms(o_ref[...], fin_ref[...])


def _ffn_moe(x2d, merged, wo, g, rw, wg, wu, wd, fin, tm, tf):
    n, d = x2d.shape
    f = wg.shape[2]
    n_f = f // tf
    return pl.pallas_call(
        functools.partial(_ffn_moe_kernel, n_f=n_f),
        out_shape=jax.ShapeDtypeStruct((n, d), F32),
        grid=(n // tm, N_EXPERTS, n_f),
        in_specs=[pl.BlockSpec((tm, d), lambda i, e, j: (i, 0)),
                  pl.BlockSpec((tm, d), lambda i, e, j: (i, 0)),
                  pl.BlockSpec((d, d), lambda i, e, j: (0, 0)),
                  pl.BlockSpec((1, d), lambda i, e, j: (0, 0)),
                  pl.BlockSpec((d, LANES), lambda i, e, j: (0, 0)),
                  pl.BlockSpec((1, d, tf), lambda i, e, j: (e, 0, j)),
                  pl.BlockSpec((1, d, tf), lambda i, e, j: (e, 0, j)),
                  pl.BlockSpec((1, tf, d), lambda i, e, j: (e, j, 0)),
                  pl.BlockSpec((1, d), lambda i, e, j: (0, 0))],
        out_specs=pl.BlockSpec((tm, d), lambda i, e, j: (i, 0)),
        scratch_shapes=[pltpu.VMEM((tm, d), BF16), pltpu.VMEM((tm, LANES), F32)],
        compiler_params=_params(("arbitrary", "arbitrary", "arbitrary")),
        name="ffn_moe",
    )(x2d, merged, wo, g, rw, wg, wu, wd, fin)


def _state_to_block_diag(s):
    b = s.shape[0]
    s = s.reshape(b, N_GROUPS, HEADS_PER_GROUP, HEAD_DIM, HEAD_DIM)
    eye = jnp.eye(HEADS_PER_GROUP, dtype=s.dtype)
    bd = jnp.einsum('bghij,hk->bghikj', s, eye)
    return bd.reshape(b, N_GROUPS, LANES, LANES)


def _state_from_block_diag(bd):
    b = bd.shape[0]
    bd = bd.reshape(b, N_GROUPS, HEADS_PER_GROUP, HEAD_DIM, HEADS_PER_GROUP, HEAD_DIM)
    idx = jnp.arange(HEADS_PER_GROUP)
    s = bd[:, :, idx, :, idx, :]
    return jnp.moveaxis(s, 0, 2).reshape(b, N_HEADS, HEAD_DIM, HEAD_DIM)


def _prep_layer_weights(l, w_in, conv_w, shift_mu, decay_w0, decay_w2, aaa_a0, aaa_a2, gate_g2,
                        key_k, key_a, bonus_r_k, lnx_w, lnx_b, w_out):
    zero = jnp.zeros((DECAY_LORA, D_RWKV), F32)
    wwa = jnp.concatenate([jnp.concatenate([decay_w2[l], zero], axis=1),
                           jnp.concatenate([zero, aaa_a2[l]], axis=1)], axis=0).astype(BF16)
    vecs = jnp.stack([decay_w0[l], aaa_a0[l], key_k[l], key_a[l], bonus_r_k[l], lnx_w[l], lnx_b[l],
                      jnp.zeros((D_RWKV,), F32)])
    return dict(w_in=w_in[l].astype(BF16), convw=conv_w[l], mu=shift_mu[l][None], vecs=vecs,
                wwa=wwa, g2=gate_g2[l].astype(BF16), w_out=w_out[l].astype(BF16))


def _row_tile(n, want):
    return want if n % want == 0 else n


def _trunk(x, conv_st, shift_st, wkv_st, layers, ffn_norm, dense, moe, final_norm, mm):
    batch, seq, d = x.shape
    n = batch * seq
    x2d = x.reshape(n, d)
    tm = _row_tile(n, 512)
    convs, shifts, wkvs = [], [], []
    for l, lw in enumerate(layers):
        p = _norm_proj(x2d, lw['norm'], lw['w_in'], tm, D_PROJ // 6)
        merged, c, s, w = _mixer(p, conv_st[l], shift_st[l][:, None], _state_to_block_diag(wkv_st[l]),
                                 lw['convw'], lw['mu'], lw['vecs'], lw['wwa'], lw['g2'],
                                 batch, seq, mm)
        convs.append(c)
        shifts.append(s[:, 0])
        wkvs.append(_state_from_block_diag(w))
        if l % 2 == 0:
            wg, wu, wd = dense
            x2d = _ffn_dense(x2d, merged, lw['w_out'], ffn_norm[l][None], wg, wu, wd, tm,
                             wg.shape[1] // 2)
        else:
            rw, wg, wu, wd = moe
            x2d = _ffn_moe(x2d, merged, lw['w_out'], ffn_norm[l][None], rw, wg, wu, wd,
                           final_norm[None], tm, wg.shape[2] // 2)
    return x2d.reshape(batch, seq, d), jnp.stack(convs), jnp.stack(shifts), jnp.stack(wkvs)


def kernel(x_prompt, x_sample, state_conv, state_shift, state_wkv, mix_norm, w_in, conv_w, shift_mu,
           decay_w0, decay_w2, aaa_a0, aaa_a2, gate_g2, key_k, key_a, bonus_r_k, lnx_w, lnx_b, w_out,
           ffn_norm, ffn_w_gate, ffn_w_up, ffn_w_down, router_w, moe_w_gate, moe_w_up, moe_w_down,
           final_norm):
    depth = w_in.shape[0]
    assert depth == 2 and ffn_w_gate.shape[0] == 1 and moe_w_gate.shape[0] == 1
    layers = []
    for l in range(depth):
        lw = _prep_layer_weights(l, w_in, conv_w, shift_mu, decay_w0, decay_w2, aaa_a0, aaa_a2,
                                 gate_g2, key_k, key_a, bonus_r_k, lnx_w, lnx_b, w_out)
        lw['norm'] = mix_norm[l][None]
        layers.append(lw)
    dense = (ffn_w_gate[0].astype(BF16), ffn_w_up[0].astype(BF16), ffn_w_down[0].astype(BF16))
    rw = jnp.pad(router_w[0], ((0, 0), (0, LANES - N_EXPERTS)))
    moe = (rw, moe_w_gate[0].astype(BF16), moe_w_up[0].astype(BF16), moe_w_down[0].astype(BF16))

    b = x_prompt.shape[0]
    zero_conv = jnp.zeros((depth, b) + state_conv.shape[2:], state_conv.dtype)
    zero_shift = jnp.zeros((depth, b) + state_shift.shape[2:], state_shift.dtype)
    zero_wkv = jnp.zeros((depth, b) + state_wkv.shape[2:], state_wkv.dtype)
    run = functools.partial(_trunk, layers=layers, ffn_norm=ffn_norm, dense=dense, moe=moe,
                            final_norm=final_norm, mm=_mm3)
    y_p, conv_p, shift_p, wkv_p = run(x_prompt, zero_conv, zero_shift, zero_wkv)
    y_s, conv_s, shift_s, wkv_s = run(x_sample, state_conv, state_shift, state_wkv)
    return (y_p, y_s, conv_p, shift_p, wkv_p, conv_s, shift_s, wkv_s)
```

```python
import functools

import jax
import jax.numpy as jnp
from jax import lax
from jax.experimental import pallas as pl
from jax.experimental.pallas import tpu as pltpu

F32 = jnp.float32
BF16 = jnp.bfloat16

D_MODEL = 1024
N_HEADS = 16
HEAD_DIM = 64
D_RWKV = N_HEADS * HEAD_DIM
D_CONV = 1024
CONV_W = 3
DECAY_LORA = 64
AAA_LORA = 64
GATE_LORA = 128
D_SHIFT = 3 * D_RWKV + DECAY_LORA + AAA_LORA + GATE_LORA
D_PROJ = 2 * D_MODEL + 3 * D_CONV + D_SHIFT
N_EXPERTS = 8
RMS_EPS = 1e-5
GN_EPS = 64e-5
L2_EPS = 1e-12

LANES = 128
SUBLANES = 8
CHUNK = 64
HEADS_PER_GROUP = LANES // HEAD_DIM
N_GROUPS = N_HEADS // HEADS_PER_GROUP
VMEM_LIMIT = 56 * 1024 * 1024

_ZA, _ZB, _CB, _CC, _CHH, _PS = (0, D_MODEL, 2 * D_MODEL, 2 * D_MODEL + D_CONV,
                                 2 * D_MODEL + 2 * D_CONV, 2 * D_MODEL + 3 * D_CONV)
_R, _K, _V, _WA, _G = 0, D_RWKV, 2 * D_RWKV, 3 * D_RWKV, 3 * D_RWKV + DECAY_LORA + AAA_LORA


def _params(semantics):
    return pltpu.CompilerParams(dimension_semantics=semantics, vmem_limit_bytes=VMEM_LIMIT)


def _rms(x, g):
    ms = jnp.mean(x * x, axis=-1, keepdims=True)
    return x * lax.rsqrt(ms + RMS_EPS) * g


def _split(x):
    hi = x.astype(BF16)
    lo = (x - hi.astype(F32)).astype(BF16)
    return hi, lo


_NN = (((1,), (0,)), ((), ()))
_NT = (((1,), (1,)), ((), ()))


def _dot(a, b, dims=_NN):
    return lax.dot_general(a, b, dims, preferred_element_type=F32)


def _mm1(a, b, dims=_NN):
    return _dot(a.astype(BF16), b.astype(BF16), dims)


def _mm3(a, b, dims=_NN):
    a1, a2 = _split(a)
    b1, b2 = _split(b)
    return _dot(a1, b1, dims) + (_dot(a1, b2, dims) + _dot(a2, b1, dims))


def _mm_exact_rhs(a, b):
    a1, a2 = _split(a)
    bb = b.astype(BF16)
    return _dot(a1, bb) + _dot(a2, bb)


def _norm_proj_kernel(x_ref, g_ref, w_ref, o_ref, xn_ref):
    @pl.when(pl.program_id(1) == 0)
    def _():
        xn_ref[...] = _rms(x_ref[...], g_ref[...]).astype(BF16)

    o_ref[...] = _dot(xn_ref[...], w_ref[...])


def _norm_proj(x2d, g, w_bf16, tm, tn):
    n, d = x2d.shape
    dp = w_bf16.shape[1]
    return pl.pallas_call(
        _norm_proj_kernel,
        out_shape=jax.ShapeDtypeStruct((n, dp), F32),
        grid=(n // tm, dp // tn),
        in_specs=[pl.BlockSpec((tm, d), lambda i, j: (i, 0)),
                  pl.BlockSpec((1, d), lambda i, j: (0, 0)),
                  pl.BlockSpec((d, tn), lambda i, j: (0, j))],
        out_specs=pl.BlockSpec((tm, tn), lambda i, j: (i, j)),
        scratch_shapes=[pltpu.VMEM((tm, d), BF16)],
        compiler_params=_params(("arbitrary", "arbitrary")),
        name="norm_proj",
    )(x2d, g, w_bf16)


def _block_diag(x, head_masks):
    x = x.astype(BF16)
    return jnp.concatenate([x * m for m in head_masks], axis=0)


def _wkv_chunk(r, k, v, kkn, a, ld, cum, s_prev, head_masks, causal2, bd_mask):
    groups = range(len(r))
    bd = functools.partial(_block_diag, head_masks=head_masks)
    n_steps = CHUNK.bit_length() - 1
    g_b, g_k, g_s, bd_v, b_h, k_h, cend = [], [], [], [], [], [], []
    for g in groups:
        ce = cum[g][CHUNK - 1:CHUNK, :]
        w_prev = jnp.exp(cum[g] - ld[g])
        w_t = jnp.exp(cum[g])
        w_inv = jnp.exp(-cum[g])
        w_rest = jnp.exp(ce - cum[g])
        b = kkn[g] * a[g]
        lhs2 = jnp.concatenate([-kkn[g] * w_prev, r[g] * w_t], axis=0)
        rhs = jnp.concatenate([bd(b * w_inv), bd(k[g] * w_inv), s_prev[g].astype(BF16)], axis=0)
        gram = _mm1(lhs2, rhs, _NT)
        g_b.append(jnp.where(causal2, gram[:, :LANES], 0.0))
        g_k.append(jnp.where(causal2, gram[:, LANES:2 * LANES], 0.0))
        g_s.append(gram[:, 2 * LANES:])
        bd_v.append(bd(v[g]))
        b_h.append(b * w_rest)
        k_h.append(k[g] * w_rest)
        cend.append(ce)

    u = [g_s[g][:CHUNK] + _mm1(g_k[g][:CHUNK], bd_v[g]) for g in groups]
    pw = [g_b[g][:CHUNK] for g in groups]
    for i in range(n_steps):
        for g in groups:
            if i + 1 < n_steps:
                res = _mm1(pw[g], jnp.concatenate([bd(u[g]), bd(pw[g])], axis=1))
                u[g] = u[g] + res[:, :LANES]
                pw[g] = res[:, LANES:]
            else:
                u[g] = u[g] + _mm1(pw[g], bd(u[g]))

    y, s_new = [], []
    for g in groups:
        l_r = jnp.concatenate([g_b[g][CHUNK:], g_k[g][CHUNK:]], axis=1)
        y.append(g_s[g][CHUNK:] + _mm1(l_r, jnp.concatenate([bd(u[g]), bd_v[g]], axis=0)))
    for g in groups:
        uv_t = jnp.concatenate([u[g], v[g]], axis=0).T
        upd = _mm1(uv_t, jnp.concatenate([b_h[g], k_h[g]], axis=0))
        s_new.append(s_prev[g] * jnp.exp(cend[g]) + jnp.where(bd_mask, upd, 0.0))
    return y, s_new


def _mixer_kernel(p_ref, conv0_ref, shift0_ref, wkv0_ref, convw_ref, mu_ref, vec_ref, wwa_ref,
                  g2_ref, merged_ref, nconv_ref, nshift_ref, nwkv_ref, cbuf, sbuf, state,
                  *, n_chunks):
    t = pl.program_id(1)

    @pl.when(t == 0)
    def _():
        cbuf[0:SUBLANES, :] = jnp.zeros((SUBLANES, D_CONV), F32)
        cbuf[SUBLANES - (CONV_W - 1):SUBLANES, :] = conv0_ref[0]
        sbuf[0:SUBLANES, :] = jnp.zeros((SUBLANES, D_SHIFT), F32)
        sbuf[SUBLANES - 1:SUBLANES, :] = shift0_ref[0]
        state[...] = wkv0_ref[0]

    c0 = SUBLANES
    ch = p_ref[:, _CC:_CC + D_CONV] * p_ref[:, _CHH:_CHH + D_CONV]
    cbuf[c0:c0 + CHUNK, :] = ch
    conv = (cbuf[c0 - 2:c0 - 2 + CHUNK, :] * convw_ref[0:1, :]
            + cbuf[c0 - 1:c0 - 1 + CHUNK, :] * convw_ref[1:2, :]
            + ch * convw_ref[2:3, :])
    y_a = jax.nn.sigmoid(p_ref[:, _ZA:_ZA + D_MODEL]) * (p_ref[:, _CB:_CB + D_CONV] * conv)
    new_conv = cbuf[c0 + CHUNK - (CONV_W - 1):c0 + CHUNK, :]
    cbuf[0:SUBLANES, :] = cbuf[CHUNK:CHUNK + SUBLANES, :]

    ps = p_ref[:, _PS:_PS + D_SHIFT]
    sbuf[c0:c0 + CHUNK, :] = ps
    prev = sbuf[c0 - 1:c0 - 1 + CHUNK, :]
    xm = ps + (prev - ps) * mu_ref[...]
    new_shift = sbuf[c0 + CHUNK - 1:c0 + CHUNK, :]
    sbuf[0:SUBLANES, :] = sbuf[CHUNK:CHUNK + SUBLANES, :]

    w0, a0, k_k, k_a, r_k, lnx_w, lnx_b = (vec_ref[i:i + 1, :] for i in range(7))

    lane = lax.broadcasted_iota(jnp.int32, (CHUNK, LANES), 1)
    wa_in = xm[:, _WA:_WA + LANES]
    wa_in = jnp.where(lane < DECAY_LORA, jnp.tanh(wa_in), wa_in)
    wa = _mm1(wa_in, wwa_ref[...])
    gate = _mm1(jax.nn.sigmoid(xm[:, _G:_G + GATE_LORA]), g2_ref[...])
    ld_all = -jnp.exp(-0.5) * jax.nn.sigmoid(w0 + wa[:, :D_RWKV])
    a_all = jax.nn.sigmoid(a0 + wa[:, D_RWKV:])
    row = lax.broadcasted_iota(jnp.int32, (CHUNK, CHUNK), 0)
    col = lax.broadcasted_iota(jnp.int32, (CHUNK, CHUNK), 1)
    tri = jnp.where(col <= row, 1.0, 0.0).astype(BF16)
    ld1 = ld_all.astype(BF16)
    ld2 = (ld_all - ld1.astype(F32)).astype(BF16)
    ld3 = (ld_all - ld1.astype(F32) - ld2.astype(F32)).astype(BF16)
    cum_all = _dot(tri, ld1) + (_dot(tri, ld2) + _dot(tri, ld3))

    head_shift = HEAD_DIM.bit_length() - 1
    head_masks = [jnp.where((lane >> head_shift) == h, 1.0, 0.0).astype(BF16)
                  for h in range(HEADS_PER_GROUP)]
    row2 = lax.broadcasted_iota(jnp.int32, (2 * CHUNK, LANES), 0)
    lane2 = lax.broadcasted_iota(jnp.int32, (2 * CHUNK, LANES), 1)
    causal2 = (lane2 & (HEAD_DIM - 1)) < (row2 & (CHUNK - 1)) + (row2 >> (CHUNK.bit_length() - 1))
    rowl = lax.broadcasted_iota(jnp.int32, (LANES, LANES), 0)
    lanel = lax.broadcasted_iota(jnp.int32, (LANES, LANES), 1)
    bd_mask = (rowl >> head_shift) == (lanel >> head_shift)
    seg_ones = jnp.where(bd_mask, 1.0, 0.0)

    groups = range(N_GROUPS)
    sls = [slice(g * LANES, (g + 1) * LANES) for g in groups]
    r = [xm[:, _R + g * LANES:_R + (g + 1) * LANES] for g in groups]
    k = [xm[:, _K + g * LANES:_K + (g + 1) * LANES] for g in groups]
    v = [xm[:, _V + g * LANES:_V + (g + 1) * LANES] for g in groups]
    a = [a_all[:, sl] for sl in sls]
    kk = [k[g] * k_k[:, sls[g]] for g in groups]
    norm = [jnp.sqrt(_mm_exact_rhs(kk[g] * kk[g], seg_ones)) for g in groups]
    kkn = [kk[g] / jnp.maximum(norm[g], L2_EPS) for g in groups]
    kf = [k[g] * (1.0 + (a[g] - 1.0) * k_a[:, sls[g]]) for g in groups]
    bonus = [_mm_exact_rhs(r[g] * kf[g] * r_k[:, sls[g]], seg_ones) * v[g] for g in groups]
    y, s_new = _wkv_chunk(r, kf, v, kkn, a, [ld_all[:, sl] for sl in sls],
                          [cum_all[:, sl] for sl in sls], [state[g] for g in groups],
                          head_masks, causal2, bd_mask)
    for g in groups:
        state[g] = s_new[g]
    mean = [_mm_exact_rhs(y[g], seg_ones) * (1.0 / HEAD_DIM) for g in groups]
    dev = [y[g] - mean[g] for g in groups]
    var = [_mm_exact_rhs(dev[g] * dev[g], seg_ones) * (1.0 / HEAD_DIM) for g in groups]
    for g in groups:
        sl = sls[g]
        yn = dev[g] * lax.rsqrt(var[g] + GN_EPS) * lnx_w[:, sl] + lnx_b[:, sl]
        y_b = (yn + bonus[g]) * gate[:, sl]
        z_b = p_ref[:, _ZB + g * LANES:_ZB + (g + 1) * LANES]
        merged_ref[:, sl] = (y_a[:, sl] + jax.nn.sigmoid(z_b) * y_b).astype(merged_ref.dtype)

    @pl.when(t == n_chunks - 1)
    def _():
        nconv_ref[0] = new_conv
        nshift_ref[0] = new_shift
        nwkv_ref[0] = state[...]


def _mixer(p2d, conv0, shift0, wkv0_bd, convw, mu, vecs, wwa, g2, batch, seq):
    n_chunks = seq // CHUNK
    kern = functools.partial(_mixer_kernel, n_chunks=n_chunks)
    const = lambda b, t: (0, 0)
    per_b3 = lambda b, t: (b, 0, 0)
    per_b4 = lambda b, t: (b, 0, 0, 0)
    return pl.pallas_call(
        kern,
        out_shape=(jax.ShapeDtypeStruct((batch * seq, D_MODEL), BF16),
                   jax.ShapeDtypeStruct((batch, CONV_W - 1, D_CONV), F32),
                   jax.ShapeDtypeStruct((batch, 1, D_SHIFT), F32),
                   jax.ShapeDtypeStruct((batch, N_GROUPS, LANES, LANES), F32)),
        grid=(batch, n_chunks),
        in_specs=[pl.BlockSpec((CHUNK, D_PROJ), lambda b, t: (b * n_chunks + t, 0)),
                  pl.BlockSpec((1, CONV_W - 1, D_CONV), per_b3),
                  pl.BlockSpec((1, 1, D_SHIFT), per_b3),
                  pl.BlockSpec((1, N_GROUPS, LANES, LANES), per_b4),
                  pl.BlockSpec((CONV_W, D_CONV), const),
                  pl.BlockSpec((1, D_SHIFT), const),
                  pl.BlockSpec((SUBLANES, D_RWKV), const),
                  pl.BlockSpec((LANES, 2 * D_RWKV), const),
                  pl.BlockSpec((GATE_LORA, D_RWKV), const)],
        out_specs=(pl.BlockSpec((CHUNK, D_MODEL), lambda b, t: (b * n_chunks + t, 0)),
                   pl.BlockSpec((1, CONV_W - 1, D_CONV), per_b3),
                   pl.BlockSpec((1, 1, D_SHIFT), per_b3),
                   pl.BlockSpec((1, N_GROUPS, LANES, LANES), per_b4)),
        scratch_shapes=[pltpu.VMEM((CHUNK + SUBLANES, D_CONV), F32),
                        pltpu.VMEM((CHUNK + SUBLANES, D_SHIFT), F32),
                        pltpu.VMEM((N_GROUPS, LANES, LANES), F32)],
        compiler_params=_params(("arbitrary", "arbitrary")),
        name="mixer",
    )(p2d, conv0, shift0, wkv0_bd, convw, mu, vecs, wwa, g2)


def _silu_mul(gate, up):
    return gate * jax.nn.sigmoid(gate) * up


def _ffn_dense_kernel(x_ref, m_ref, wo_ref, g_ref, wg_ref, wu_ref, wd_ref, o_ref, h_ref):
    @pl.when(pl.program_id(1) == 0)
    def _():
        x1 = x_ref[...] + _dot(m_ref[...], wo_ref[...])
        o_ref[...] = x1
        h_ref[...] = _rms(x1, g_ref[...]).astype(BF16)

    h = h_ref[...]
    act = _silu_mul(_dot(h, wg_ref[...]), _dot(h, wu_ref[...])).astype(BF16)
    o_ref[...] += _dot(act, wd_ref[...])


def _ffn_dense(x2d, merged, wo, g, wg, wu, wd, tm, tf):
    n, d = x2d.shape
    f = wg.shape[1]
    return pl.pallas_call(
        _ffn_dense_kernel,
        out_shape=jax.ShapeDtypeStruct((n, d), F32),
        grid=(n // tm, f // tf),
        in_specs=[pl.BlockSpec((tm, d), lambda i, j: (i, 0)),
                  pl.BlockSpec((tm, d), lambda i, j: (i, 0)),
                  pl.BlockSpec((d, d), lambda i, j: (0, 0)),
                  pl.BlockSpec((1, d), lambda i, j: (0, 0)),
                  pl.BlockSpec((d, tf), lambda i, j: (0, j)),
                  pl.BlockSpec((d, tf), lambda i, j: (0, j)),
                  pl.BlockSpec((tf, d), lambda i, j: (j, 0))],
        out_specs=pl.BlockSpec((tm, d), lambda i, j: (i, 0)),
        scratch_shapes=[pltpu.VMEM((tm, d), BF16)],
        compiler_params=_params(("arbitrary", "arbitrary")),
        name="ffn_dense",
    )(x2d, merged, wo, g, wg, wu, wd)


def _top2_gates(logits):
    lane = lax.broadcasted_iota(jnp.int32, logits.shape, 1).astype(F32)
    neg = jnp.float32(-jnp.inf)
    lg = jnp.where(lane < N_EXPERTS, logits, neg)
    m1 = jnp.max(lg, axis=-1, keepdims=True)
    i1 = jnp.min(jnp.where(lg == m1, lane, float(LANES)), axis=-1, keepdims=True)
    lg2 = jnp.where(lane == i1, neg, lg)
    m2 = jnp.max(lg2, axis=-1, keepdims=True)
    i2 = jnp.min(jnp.where(lg2 == m2, lane, float(LANES)), axis=-1, keepdims=True)
    e2 = jnp.exp(m2 - m1)
    den = 1.0 + e2
    return jnp.where(lane == i1, 1.0 / den, 0.0) + jnp.where(lane == i2, e2 / den, 0.0)


def _ffn_moe_kernel(x_ref, m_ref, wo_ref, g_ref, rw_ref, wg_ref, wu_ref, wd_ref, fin_ref,
                    o_ref, h_ref, gates_ref, *, n_f):
    e = pl.program_id(1)
    j = pl.program_id(2)

    @pl.when((e == 0) & (j == 0))
    def _():
        x1 = x_ref[...] + _dot(m_ref[...], wo_ref[...])
        o_ref[...] = x1
        h = _rms(x1, g_ref[...])
        h_ref[...] = h.astype(BF16)
        gates_ref[...] = _top2_gates(_mm3(h, rw_ref[...]))

    h = h_ref[...]
    act = _silu_mul(_dot(h, wg_ref[0]), _dot(h, wu_ref[0])).astype(BF16)
    lane = lax.broadcasted_iota(jnp.int32, gates_ref.shape, 1)
    gate_e = jnp.sum(jnp.where(lane == e, gates_ref[...], 0.0), axis=-1, keepdims=True)
    o_ref[...] += gate_e * _dot(act, wd_ref[0])

    @pl.when((e == N_EXPERTS - 1) & (j == n_f - 1))
    def _():
        o_ref[...] = _rms(o_ref[...], fin_ref[...])


def _ffn_moe(x2d, merged, wo, g, rw, wg, wu, wd, fin, tm, tf):
    n, d = x2d.shape
    f = wg.shape[2]
    n_f = f // tf
    return pl.pallas_call(
        functools.partial(_ffn_moe_kernel, n_f=n_f),
        out_shape=jax.ShapeDtypeStruct((n, d), F32),
        grid=(n // tm, N_EXPERTS, n_f),
        in_specs=[pl.BlockSpec((tm, d), lambda i, e, j: (i, 0)),
                  pl.BlockSpec((tm, d), lambda i, e, j: (i, 0)),
                  pl.BlockSpec((d, d), lambda i, e, j: (0, 0)),
                  pl.BlockSpec((1, d), lambda i, e, j: (0, 0)),
                  pl.BlockSpec((d, LANES), lambda i, e, j: (0, 0)),
                  pl.BlockSpec((1, d, tf), lambda i, e, j: (e, 0, j)),
                  pl.BlockSpec((1, d, tf), lambda i, e, j: (e, 0, j)),
                  pl.BlockSpec((1, tf, d), lambda i, e, j: (e, j, 0)),
                  pl.BlockSpec((1, d), lambda i, e, j: (0, 0))],
        out_specs=pl.BlockSpec((tm, d), lambda i, e, j: (i, 0)),
        scratch_shapes=[pltpu.VMEM((tm, d), BF16), pltpu.VMEM((tm, LANES), F32)],
        compiler_params=_params(("arbitrary", "arbitrary", "arbitrary")),
        name="ffn_moe",
    )(x2d, merged, wo, g, rw, wg, wu, wd, fin)


def _state_to_block_diag(s):
    b = s.shape[0]
    s = s.reshape(b, N_GROUPS, HEADS_PER_GROUP, HEAD_DIM, HEAD_DIM)
    eye = jnp.eye(HEADS_PER_GROUP, dtype=s.dtype)
    bd = jnp.einsum('bghij,hk->bghikj', s, eye)
    return bd.reshape(b, N_GROUPS, LANES, LANES)


def _state_from_block_diag(bd):
    b = bd.shape[0]
    bd = bd.reshape(b, N_GROUPS, HEADS_PER_GROUP, HEAD_DIM, HEADS_PER_GROUP, HEAD_DIM)
    idx = jnp.arange(HEADS_PER_GROUP)
    s = bd[:, :, idx, :, idx, :]
    return jnp.moveaxis(s, 0, 2).reshape(b, N_HEADS, HEAD_DIM, HEAD_DIM)


def _prep_layer_weights(l, w_in, conv_w, shift_mu, decay_w0, decay_w2, aaa_a0, aaa_a2, gate_g2,
                        key_k, key_a, bonus_r_k, lnx_w, lnx_b, w_out):
    zero = jnp.zeros((DECAY_LORA, D_RWKV), F32)
    wwa = jnp.concatenate([jnp.concatenate([decay_w2[l], zero], axis=1),
                           jnp.concatenate([zero, aaa_a2[l]], axis=1)], axis=0).astype(BF16)
    vecs = jnp.stack([decay_w0[l], aaa_a0[l], key_k[l], key_a[l], bonus_r_k[l], lnx_w[l], lnx_b[l],
                      jnp.zeros((D_RWKV,), F32)])
    return dict(w_in=w_in[l].astype(BF16), convw=conv_w[l], mu=shift_mu[l][None], vecs=vecs,
                wwa=wwa, g2=gate_g2[l].astype(BF16), w_out=w_out[l].astype(BF16))


def _row_tile(n, want):
    return want if n % want == 0 else n


def _trunk(x, conv_st, shift_st, wkv_st, layers, ffn_norm, dense, moe, final_norm):
    batch, seq, d = x.shape
    n = batch * seq
    x2d = x.reshape(n, d)
    tm = _row_tile(n, 512)
    convs, shifts, wkvs = [], [], []
    for l, lw in enumerate(layers):
        p = _norm_proj(x2d, lw['norm'], lw['w_in'], tm, D_PROJ // 6)
        merged, c, s, w = _mixer(p, conv_st[l], shift_st[l][:, None], _state_to_block_diag(wkv_st[l]),
                                 lw['convw'], lw['mu'], lw['vecs'], lw['wwa'], lw['g2'],
                                 batch, seq)
        convs.append(c)
        shifts.append(s[:, 0])
        wkvs.append(_state_from_block_diag(w))
        if l % 2 == 0:
            wg, wu, wd = dense
            x2d = _ffn_dense(x2d, merged, lw['w_out'], ffn_norm[l][None], wg, wu, wd, tm,
                             wg.shape[1] // 2)
        else:
            rw, wg, wu, wd = moe
            x2d = _ffn_moe(x2d, merged, lw['w_out'], ffn_norm[l][None], rw, wg, wu, wd,
                           final_norm[None], tm, wg.shape[2] // 2)
    return x2d.reshape(batch, seq, d), jnp.stack(convs), jnp.stack(shifts), jnp.stack(wkvs)


def kernel(x_prompt, x_sample, state_conv, state_shift, state_wkv, mix_norm, w_in, conv_w, shift_mu,
           decay_w0, decay_w2, aaa_a0, aaa_a2, gate_g2, key_k, key_a, bonus_r_k, lnx_w, lnx_b, w_out,
           ffn_norm, ffn_w_gate, ffn_w_up, ffn_w_down, router_w, moe_w_gate, moe_w_up, moe_w_down,
           final_norm):
    depth = w_in.shape[0]
    assert depth == 2 and ffn_w_gate.shape[0] == 1 and moe_w_gate.shape[0] == 1
    layers = []
    for l in range(depth):
        lw = _prep_layer_weights(l, w_in, conv_w, shift_mu, decay_w0, decay_w2, aaa_a0, aaa_a2,
                                 gate_g2, key_k, key_a, bonus_r_k, lnx_w, lnx_b, w_out)
        lw['norm'] = mix_norm[l][None]
        layers.append(lw)
    dense = (ffn_w_gate[0].astype(BF16), ffn_w_up[0].astype(BF16), ffn_w_down[0].astype(BF16))
    rw = jnp.pad(router_w[0], ((0, 0), (0, LANES - N_EXPERTS)))
    moe = (rw, moe_w_gate[0].astype(BF16), moe_w_up[0].astype(BF16), moe_w_down[0].astype(BF16))

    b = x_prompt.shape[0]
    zero_conv = jnp.zeros((depth, b) + state_conv.shape[2:], state_conv.dtype)
    zero_shift = jnp.zeros((depth, b) + state_shift.shape[2:], state_shift.dtype)
    zero_wkv = jnp.zeros((depth, b) + state_wkv.shape[2:], state_wkv.dtype)
    run = functools.partial(_trunk, layers=layers, ffn_norm=ffn_norm, dense=dense, moe=moe,
                            final_norm=final_norm)
    y_p, conv_p, shift_p, wkv_p = run(x_prompt, zero_conv, zero_shift, zero_wkv)
    y_s, conv_s, shift_s, wkv_s = run(x_sample, state_conv, state_shift, state_wkv)
    return (y_p, y_s, conv_p, shift_p, wkv_p, conv_s, shift_s, wkv_s)
```

```python
import functools

import jax
import jax.numpy as jnp
from jax import lax
from jax.experimental import pallas as pl
from jax.experimental.pallas import tpu as pltpu

F32 = jnp.float32
BF16 = jnp.bfloat16

D_MODEL = 1024
N_HEADS = 16
HEAD_DIM = 64
D_RWKV = N_HEADS * HEAD_DIM
D_CONV = 1024
CONV_W = 3
DECAY_LORA = 64
AAA_LORA = 64
GATE_LORA = 128
D_SHIFT = 3 * D_RWKV + DECAY_LORA + AAA_LORA + GATE_LORA
D_PROJ = 2 * D_MODEL + 3 * D_CONV + D_SHIFT
N_EXPERTS = 8
RMS_EPS = 1e-5
GN_EPS = 64e-5
L2_EPS = 1e-12

LANES = 128
SUBLANES = 8
CHUNK = 64
HEADS_PER_GROUP = LANES // HEAD_DIM
N_GROUPS = N_HEADS // HEADS_PER_GROUP
VMEM_LIMIT = 56 * 1024 * 1024

_ZA, _ZB, _CB, _CC, _CHH, _PS = (0, D_MODEL, 2 * D_MODEL, 2 * D_MODEL + D_CONV,
                                 2 * D_MODEL + 2 * D_CONV, 2 * D_MODEL + 3 * D_CONV)
_R, _K, _V, _WA, _G = 0, D_RWKV, 2 * D_RWKV, 3 * D_RWKV, 3 * D_RWKV + DECAY_LORA + AAA_LORA


def _params(semantics):
    return pltpu.CompilerParams(dimension_semantics=semantics, vmem_limit_bytes=VMEM_LIMIT)


def _rms(x, g):
    ms = jnp.mean(x * x, axis=-1, keepdims=True)
    return x * lax.rsqrt(ms + RMS_EPS) * g


def _split(x):
    hi = x.astype(BF16)
    lo = (x - hi.astype(F32)).astype(BF16)
    return hi, lo


_NN = (((1,), (0,)), ((), ()))
_NT = (((1,), (1,)), ((), ()))


def _dot(a, b, dims=_NN):
    return lax.dot_general(a, b, dims, preferred_element_type=F32)


def _mm1(a, b, dims=_NN):
    return _dot(a.astype(BF16), b.astype(BF16), dims)


def _mm3(a, b, dims=_NN):
    a1, a2 = _split(a)
    b1, b2 = _split(b)
    return _dot(a1, b1, dims) + (_dot(a1, b2, dims) + _dot(a2, b1, dims))


def _mm_exact_rhs(a, b):
    a1, a2 = _split(a)
    bb = b.astype(BF16)
    return _dot(a1, bb) + _dot(a2, bb)


def _norm_proj_kernel(x_ref, g_ref, w_ref, o_ref, xn_ref):
    @pl.when(pl.program_id(1) == 0)
    def _():
        xn_ref[...] = _rms(x_ref[...], g_ref[...]).astype(BF16)

    o_ref[...] = _dot(xn_ref[...], w_ref[...])


def _norm_proj(x2d, g, w_bf16, tm, tn):
    n, d = x2d.shape
    dp = w_bf16.shape[1]
    return pl.pallas_call(
        _norm_proj_kernel,
        out_shape=jax.ShapeDtypeStruct((n, dp), F32),
        grid=(n // tm, dp // tn),
        in_specs=[pl.BlockSpec((tm, d), lambda i, j: (i, 0)),
                  pl.BlockSpec((1, d), lambda i, j: (0, 0)),
                  pl.BlockSpec((d, tn), lambda i, j: (0, j))],
        out_specs=pl.BlockSpec((tm, tn), lambda i, j: (i, j)),
        scratch_shapes=[pltpu.VMEM((tm, d), BF16)],
        compiler_params=_params(("arbitrary", "arbitrary")),
        name="norm_proj",
    )(x2d, g, w_bf16)


def _block_diag(x, head_masks):
    x = x.astype(BF16)
    return jnp.concatenate([x * m for m in head_masks], axis=0)


def _wkv_chunk(r, k, v, kkn, a, ld, cum, s_prev, head_masks, causal2, bd_mask):
    groups = range(len(r))
    bd = functools.partial(_block_diag, head_masks=head_masks)
    n_steps = CHUNK.bit_length() - 1
    g_b, g_k, g_s, bd_v, b_h, k_h, cend = [], [], [], [], [], [], []
    for g in groups:
        ce = cum[g][CHUNK - 1:CHUNK, :]
        w_prev = jnp.exp(cum[g] - ld[g])
        w_t = jnp.exp(cum[g])
        w_inv = jnp.exp(-cum[g])
        w_rest = jnp.exp(ce - cum[g])
        b = kkn[g] * a[g]
        lhs2 = jnp.concatenate([-kkn[g] * w_prev, r[g] * w_t], axis=0)
        rhs = jnp.concatenate([bd(b * w_inv), bd(k[g] * w_inv), s_prev[g].astype(BF16)], axis=0)
        gram = _mm1(lhs2, rhs, _NT)
        g_b.append(jnp.where(causal2, gram[:, :LANES], 0.0))
        g_k.append(jnp.where(causal2, gram[:, LANES:2 * LANES], 0.0))
        g_s.append(gram[:, 2 * LANES:])
        bd_v.append(bd(v[g]))
        b_h.append(b * w_rest)
        k_h.append(k[g] * w_rest)
        cend.append(ce)

    u = [g_s[g][:CHUNK] + _mm1(g_k[g][:CHUNK], bd_v[g]) for g in groups]
    pw = [g_b[g][:CHUNK] for g in groups]
    for i in range(n_steps):
        for g in groups:
            if i + 1 < n_steps:
                res = _mm1(pw[g], jnp.concatenate([bd(u[g]), bd(pw[g])], axis=1))
                u[g] = u[g] + res[:, :LANES]
                pw[g] = res[:, LANES:]
            else:
                u[g] = u[g] + _mm1(pw[g], bd(u[g]))

    y, s_new = [], []
    for g in groups:
        l_r = jnp.concatenate([g_b[g][CHUNK:], g_k[g][CHUNK:]], axis=1)
        y.append(g_s[g][CHUNK:] + _mm1(l_r, jnp.concatenate([bd(u[g]), bd_v[g]], axis=0)))
    for g in groups:
        uv_t = jnp.concatenate([u[g], v[g]], axis=0).T
        upd = _mm1(uv_t, jnp.concatenate([b_h[g], k_h[g]], axis=0))
        s_new.append(s_prev[g] * jnp.exp(cend[g]) + jnp.where(bd_mask, upd, 0.0))
    return y, s_new


def _mixer_kernel(p_ref, conv0_ref, shift0_ref, wkv0_ref, convw_ref, mu_ref, vec_ref, wwa_ref,
                  g2_ref, merged_ref, nconv_ref, nshift_ref, nwkv_ref, cbuf, sbuf, state,
                  *, n_chunks):
    t = pl.program_id(1)

    @pl.when(t == 0)
    def _():
        cbuf[0:SUBLANES, :] = jnp.zeros((SUBLANES, D_CONV), F32)
        cbuf[SUBLANES - (CONV_W - 1):SUBLANES, :] = conv0_ref[0]
        sbuf[0:SUBLANES, :] = jnp.zeros((SUBLANES, D_SHIFT), F32)
        sbuf[SUBLANES - 1:SUBLANES, :] = shift0_ref[0]
        state[...] = wkv0_ref[0]

    c0 = SUBLANES
    ch = p_ref[:, _CC:_CC + D_CONV] * p_ref[:, _CHH:_CHH + D_CONV]
    cbuf[c0:c0 + CHUNK, :] = ch
    conv = (cbuf[c0 - 2:c0 - 2 + CHUNK, :] * convw_ref[0:1, :]
            + cbuf[c0 - 1:c0 - 1 + CHUNK, :] * convw_ref[1:2, :]
            + ch * convw_ref[2:3, :])
    y_a = jax.nn.sigmoid(p_ref[:, _ZA:_ZA + D_MODEL]) * (p_ref[:, _CB:_CB + D_CONV] * conv)
    new_conv = cbuf[c0 + CHUNK - (CONV_W - 1):c0 + CHUNK, :]
    cbuf[0:SUBLANES, :] = cbuf[CHUNK:CHUNK + SUBLANES, :]

    ps = p_ref[:, _PS:_PS + D_SHIFT]
    sbuf[c0:c0 + CHUNK, :] = ps
    prev = sbuf[c0 - 1:c0 - 1 + CHUNK, :]
    xm = ps + (prev - ps) * mu_ref[...]
    new_shift = sbuf[c0 + CHUNK - 1:c0 + CHUNK, :]
    sbuf[0:SUBLANES, :] = sbuf[CHUNK:CHUNK + SUBLANES, :]

    w0, a0, k_k, k_a, r_k, lnx_w, lnx_b = (vec_ref[i:i + 1, :] for i in range(7))

    lane = lax.broadcasted_iota(jnp.int32, (CHUNK, LANES), 1)
    wa_in = xm[:, _WA:_WA + LANES]
    wa_in = jnp.where(lane < DECAY_LORA, jnp.tanh(wa_in), wa_in)
    wa = _mm1(wa_in, wwa_ref[...])
    gate = _mm1(jax.nn.sigmoid(xm[:, _G:_G + GATE_LORA]), g2_ref[...])
    ld_all = -jnp.exp(-0.5) * jax.nn.sigmoid(w0 + wa[:, :D_RWKV])
    a_all = jax.nn.sigmoid(a0 + wa[:, D_RWKV:])
    row = lax.broadcasted_iota(jnp.int32, (CHUNK, CHUNK), 0)
    col = lax.broadcasted_iota(jnp.int32, (CHUNK, CHUNK), 1)
    tri = jnp.where(col <= row, 1.0, 0.0).astype(BF16)
    ld1 = ld_all.astype(BF16)
    ld2 = (ld_all - ld1.astype(F32)).astype(BF16)
    ld3 = (ld_all - ld1.astype(F32) - ld2.astype(F32)).astype(BF16)
    cum_all = _dot(tri, ld1) + (_dot(tri, ld2) + _dot(tri, ld3))

    head_shift = HEAD_DIM.bit_length() - 1
    head_masks = [jnp.where((lane >> head_shift) == h, 1.0, 0.0).astype(BF16)
                  for h in range(HEADS_PER_GROUP)]
    row2 = lax.broadcasted_iota(jnp.int32, (2 * CHUNK, LANES), 0)
    lane2 = lax.broadcasted_iota(jnp.int32, (2 * CHUNK, LANES), 1)
    causal2 = (lane2 & (HEAD_DIM - 1)) < (row2 & (CHUNK - 1)) + (row2 >> (CHUNK.bit_length() - 1))
    rowl = lax.broadcasted_iota(jnp.int32, (LANES, LANES), 0)
    lanel = lax.broadcasted_iota(jnp.int32, (LANES, LANES), 1)
    bd_mask = (rowl >> head_shift) == (lanel >> head_shift)
    seg_ones = jnp.where(bd_mask, 1.0, 0.0)

    groups = range(N_GROUPS)
    sls = [slice(g * LANES, (g + 1) * LANES) for g in groups]
    r = [xm[:, _R + g * LANES:_R + (g + 1) * LANES] for g in groups]
    k = [xm[:, _K + g * LANES:_K + (g + 1) * LANES] for g in groups]
    v = [xm[:, _V + g * LANES:_V + (g + 1) * LANES] for g in groups]
    a = [a_all[:, sl] for sl in sls]
    kk = [k[g] * k_k[:, sls[g]] for g in groups]
    norm = [jnp.sqrt(_mm_exact_rhs(kk[g] * kk[g], seg_ones)) for g in groups]
    kkn = [kk[g] / jnp.maximum(norm[g], L2_EPS) for g in groups]
    kf = [k[g] * (1.0 + (a[g] - 1.0) * k_a[:, sls[g]]) for g in groups]
    bonus = [_mm_exact_rhs(r[g] * kf[g] * r_k[:, sls[g]], seg_ones) * v[g] for g in groups]
    y, s_new = _wkv_chunk(r, kf, v, kkn, a, [ld_all[:, sl] for sl in sls],
                          [cum_all[:, sl] for sl in sls], [state[g] for g in groups],
                          head_masks, causal2, bd_mask)
    for g in groups:
        state[g] = s_new[g]
    mean = [_mm_exact_rhs(y[g], seg_ones) * (1.0 / HEAD_DIM) for g in groups]
    dev = [y[g] - mean[g] for g in groups]
    var = [_mm_exact_rhs(dev[g] * dev[g], seg_ones) * (1.0 / HEAD_DIM) for g in groups]
    for g in groups:
        sl = sls[g]
        yn = dev[g] * lax.rsqrt(var[g] + GN_EPS) * lnx_w[:, sl] + lnx_b[:, sl]
        y_b = (yn + bonus[g]) * gate[:, sl]
        z_b = p_ref[:, _ZB + g * LANES:_ZB + (g + 1) * LANES]
        merged_ref[:, sl] = (y_a[:, sl] + jax.nn.sigmoid(z_b) * y_b).astype(merged_ref.dtype)

    @pl.when(t == n_chunks - 1)
    def _():
        nconv_ref[0] = new_conv
        nshift_ref[0] = new_shift
        nwkv_ref[0] = state[...]


def _mixer(p2d, conv0, shift0, wkv0_bd, convw, mu, vecs, wwa, g2, batch, seq):
    n_chunks = seq // CHUNK
    kern = functools.partial(_mixer_kernel, n_chunks=n_chunks)
    const = lambda b, t: (0, 0)
    per_b3 = lambda b, t: (b, 0, 0)
    per_b4 = lambda b, t: (b, 0, 0, 0)
    return pl.pallas_call(
        kern,
        out_shape=(jax.ShapeDtypeStruct((batch * seq, D_MODEL), BF16),
                   jax.ShapeDtypeStruct((batch, CONV_W - 1, D_CONV), F32),
                   jax.ShapeDtypeStruct((batch, 1, D_SHIFT), F32),
                   jax.ShapeDtypeStruct((batch, N_GROUPS, LANES, LANES), F32)),
        grid=(batch, n_chunks),
        in_specs=[pl.BlockSpec((CHUNK, D_PROJ), lambda b, t: (b * n_chunks + t, 0)),
                  pl.BlockSpec((1, CONV_W - 1, D_CONV), per_b3),
                  pl.BlockSpec((1, 1, D_SHIFT), per_b3),
                  pl.BlockSpec((1, N_GROUPS, LANES, LANES), per_b4),
                  pl.BlockSpec((CONV_W, D_CONV), const),
                  pl.BlockSpec((1, D_SHIFT), const),
                  pl.BlockSpec((SUBLANES, D_RWKV), const),
                  pl.BlockSpec((LANES, 2 * D_RWKV), const),
                  pl.BlockSpec((GATE_LORA, D_RWKV), const)],
        out_specs=(pl.BlockSpec((CHUNK, D_MODEL), lambda b, t: (b * n_chunks + t, 0)),
                   pl.BlockSpec((1, CONV_W - 1, D_CONV), per_b3),
                   pl.BlockSpec((1, 1, D_SHIFT), per_b3),
                   pl.BlockSpec((1, N_GROUPS, LANES, LANES), per_b4)),
        scratch_shapes=[pltpu.VMEM((CHUNK + SUBLANES, D_CONV), F32),
                        pltpu.VMEM((CHUNK + SUBLANES, D_SHIFT), F32),
                        pltpu.VMEM((N_GROUPS, LANES, LANES), F32)],
        compiler_params=_params(("arbitrary", "arbitrary")),
        name="mixer",
    )(p2d, conv0, shift0, wkv0_bd, convw, mu, vecs, wwa, g2)


def _silu_mul(gate, up):
    return gate * jax.nn.sigmoid(gate) * up


def _ffn_dense_kernel(x_ref, m_ref, wo_ref, g_ref, wg_ref, wu_ref, wd_ref, o_ref, h_ref):
    @pl.when(pl.program_id(1) == 0)
    def _():
        x1 = x_ref[...] + _dot(m_ref[...], wo_ref[...])
        o_ref[...] = x1
        h_ref[...] = _rms(x1, g_ref[...]).astype(BF16)

    h = h_ref[...]
    act = _silu_mul(_dot(h, wg_ref[...]), _dot(h, wu_ref[...])).astype(BF16)
    o_ref[...] += _dot(act, wd_ref[...])


def _ffn_dense(x2d, merged, wo, g, wg, wu, wd, tm, tf):
    n, d = x2d.shape
    f = wg.shape[1]
    return pl.pallas_call(
        _ffn_dense_kernel,
        out_shape=jax.ShapeDtypeStruct((n, d), F32),
        grid=(n // tm, f // tf),
        in_specs=[pl.BlockSpec((tm, d), lambda i, j: (i, 0)),
                  pl.BlockSpec((tm, d), lambda i, j: (i, 0)),
                  pl.BlockSpec((d, d), lambda i, j: (0, 0)),
                  pl.BlockSpec((1, d), lambda i, j: (0, 0)),
                  pl.BlockSpec((d, tf), lambda i, j: (0, j)),
                  pl.BlockSpec((d, tf), lambda i, j: (0, j)),
                  pl.BlockSpec((tf, d), lambda i, j: (j, 0))],
        out_specs=pl.BlockSpec((tm, d), lambda i, j: (i, 0)),
        scratch_shapes=[pltpu.VMEM((tm, d), BF16)],
        compiler_params=_params(("arbitrary", "arbitrary")),
        name="ffn_dense",
    )(x2d, merged, wo, g, wg, wu, wd)


def _top2(logits):
    lane_i = lax.broadcasted_iota(jnp.int32, logits.shape, 1)
    lane = lane_i.astype(F32)
    neg = jnp.float32(-jnp.inf)
    lg = jnp.where(lane_i < N_EXPERTS, logits, neg)
    m1 = jnp.max(lg, axis=-1, keepdims=True)
    i1 = jnp.min(jnp.where(lg == m1, lane, float(LANES)), axis=-1, keepdims=True)
    lg2 = jnp.where(lane == i1, neg, lg)
    m2 = jnp.max(lg2, axis=-1, keepdims=True)
    i2 = jnp.min(jnp.where(lg2 == m2, lane, float(LANES)), axis=-1, keepdims=True)
    e2 = jnp.exp(m2 - m1)
    den = 1.0 + e2
    ids = jnp.where(lane_i == 0, i1, jnp.where(lane_i == 1, i2, 0.0)).astype(jnp.int32)
    wts = jnp.where(lane_i == 0, 1.0 / den, jnp.where(lane_i == 1, e2 / den, 0.0))
    return ids, wts


def _router_kernel(x_ref, m_ref, wo_ref, g_ref, rw_ref, x1_ref, ids_ref, wts_ref):
    x1 = x_ref[...] + _dot(m_ref[...], wo_ref[...])
    x1_ref[...] = x1
    ids_ref[...], wts_ref[...] = _top2(_mm3(_rms(x1, g_ref[...]), rw_ref[...]))


def _router(x2d, merged, wo, g, rw, tm):
    n, d = x2d.shape
    row = lambda i: (i, 0)
    const = lambda i: (0, 0)
    return pl.pallas_call(
        _router_kernel,
        out_shape=(jax.ShapeDtypeStruct((n, d), F32),
                   jax.ShapeDtypeStruct((n, LANES), jnp.int32),
                   jax.ShapeDtypeStruct((n, LANES), F32)),
        grid=(n // tm,),
        in_specs=[pl.BlockSpec((tm, d), row), pl.BlockSpec((tm, d), row),
                  pl.BlockSpec((d, d), const), pl.BlockSpec((1, d), const),
                  pl.BlockSpec((d, LANES), const)],
        out_specs=(pl.BlockSpec((tm, d), row), pl.BlockSpec((tm, LANES), row),
                   pl.BlockSpec((tm, LANES), row)),
        compiler_params=_params(("arbitrary",)),
        name="router",
    )(x2d, merged, wo, g, rw)


def _experts_kernel(te_ref, src_ref, dst_ref, w_ref, x1_hbm, g_ref, wg_ref, wu_ref, wd_ref,
                    y_hbm, xbuf, obuf, gather_sem, scatter_sem, *, tm, f_split):
    t = pl.program_id(0)

    def gather_copy(r, tok):
        return pltpu.make_async_copy(x1_hbm.at[pl.ds(tok, 1)], xbuf.at[pl.ds(r, 1)], gather_sem)

    def scatter_copy(r, row):
        return pltpu.make_async_copy(obuf.at[pl.ds(r, 1)], y_hbm.at[pl.ds(row, 1)], scatter_sem)

    @pl.when(te_ref[t] < N_EXPERTS)
    def _():
        def gather_start(r, c):
            gather_copy(r, src_ref[0, 0, r]).start()
            return c

        def gather_wait(r, c):
            gather_copy(r, 0).wait()
            return c

        lax.fori_loop(0, tm, gather_start, 0)
        lax.fori_loop(0, tm, gather_wait, 0)

        h = _rms(xbuf[...], g_ref[...]).astype(BF16)
        f = wg_ref.shape[2]
        fs = f // f_split
        acc = None
        for i in range(f_split):
            act = _silu_mul(_dot(h, wg_ref[0, :, i * fs:(i + 1) * fs]),
                            _dot(h, wu_ref[0, :, i * fs:(i + 1) * fs])).astype(BF16)
            part = _dot(act, wd_ref[0, i * fs:(i + 1) * fs, :])
            acc = part if acc is None else acc + part
        obuf[...] = w_ref[...] * acc

        def scatter_start(r, c):
            row = dst_ref[0, 0, r]

            @pl.when(row >= 0)
            def _():
                scatter_copy(r, row).start()
            return c

        def scatter_wait(r, c):
            row = dst_ref[0, 0, r]

            @pl.when(row >= 0)
            def _():
                scatter_copy(r, row).wait()
            return c

        lax.fori_loop(0, tm, scatter_start, 0)
        lax.fori_loop(0, tm, scatter_wait, 0)


def _experts(x1, g, wg, wu, wd, tile_expert, src_tok, dst_row, w_sorted, tm):
    n, d = x1.shape
    n_tiles = tile_expert.shape[0]
    f = wg.shape[2]
    expert_w = lambda t, te: (jnp.minimum(te[t], N_EXPERTS - 1), 0, 0)
    single = pl.Buffered(1)
    grid_spec = pltpu.PrefetchScalarGridSpec(
        num_scalar_prefetch=1,
        grid=(n_tiles,),
        in_specs=[pl.BlockSpec((1, 1, tm), lambda t, te: (t, 0, 0), memory_space=pltpu.SMEM),
                  pl.BlockSpec((1, 1, tm), lambda t, te: (t, 0, 0), memory_space=pltpu.SMEM),
                  pl.BlockSpec((tm, 1), lambda t, te: (t, 0)),
                  pl.BlockSpec(memory_space=pl.ANY),
                  pl.BlockSpec((1, d), lambda t, te: (0, 0)),
                  pl.BlockSpec((1, d, f), expert_w, pipeline_mode=single),
                  pl.BlockSpec((1, d, f), expert_w, pipeline_mode=single),
                  pl.BlockSpec((1, f, d), expert_w, pipeline_mode=single)],
        out_specs=pl.BlockSpec(memory_space=pl.ANY),
        scratch_shapes=[pltpu.VMEM((tm, d), F32), pltpu.VMEM((tm, d), F32),
                        pltpu.SemaphoreType.DMA(()), pltpu.SemaphoreType.DMA(())])
    return pl.pallas_call(
        functools.partial(_experts_kernel, tm=tm, f_split=2),
        out_shape=jax.ShapeDtypeStruct((2 * n, d), F32),
        grid_spec=grid_spec,
        compiler_params=_params(("arbitrary",)),
        name="experts",
    )(tile_expert, src_tok.reshape(n_tiles, 1, tm), dst_row.reshape(n_tiles, 1, tm),
      w_sorted.reshape(-1, 1), x1, g, wg, wu, wd)


def _combine_kernel(x1_ref, y_ref, fin_ref, o_ref):
    d = x1_ref.shape[1]
    o_ref[...] = _rms(x1_ref[...] + (y_ref[:, :d] + y_ref[:, d:]), fin_ref[...])


def _combine(x1, y2, fin, tm):
    n, d = x1.shape
    return pl.pallas_call(
        _combine_kernel,
        out_shape=jax.ShapeDtypeStruct((n, d), F32),
        grid=(n // tm,),
        in_specs=[pl.BlockSpec((tm, d), lambda i: (i, 0)),
                  pl.BlockSpec((tm, 2 * d), lambda i: (i, 0)),
                  pl.BlockSpec((1, d), lambda i: (0, 0))],
        out_specs=pl.BlockSpec((tm, d), lambda i: (i, 0)),
        compiler_params=_params(("arbitrary",)),
        name="combine",
    )(x1, y2.reshape(n, 2 * d), fin)


def _routing_plan(ids, wts, tm):
    n = ids.shape[0]
    e_flat = ids[:, :2].reshape(-1)
    onehot = (e_flat[:, None] == jnp.arange(N_EXPERTS, dtype=jnp.int32)[None, :]).astype(jnp.int32)
    csum = jnp.cumsum(onehot, axis=0)
    counts = csum[-1]
    padded = (counts + tm - 1) // tm * tm
    ends = jnp.cumsum(padded)
    pos = jnp.sum(onehot * (csum - 1 + (ends - padded)[None, :]), axis=1)
    total = 2 * n + N_EXPERTS * tm
    rows = jnp.arange(2 * n, dtype=jnp.int32)
    src_tok = jnp.zeros((total,), jnp.int32).at[pos].set(rows // 2)
    dst_row = jnp.full((total,), -1, jnp.int32).at[pos].set(rows)
    w_sorted = jnp.zeros((total,), F32).at[pos].set(wts[:, :2].reshape(-1))
    tile_start = jnp.arange(total // tm, dtype=jnp.int32) * tm
    tile_expert = jnp.sum((tile_start[:, None] >= ends[None, :]).astype(jnp.int32), axis=1)
    return tile_expert, src_tok, dst_row, w_sorted


def _ffn_moe(x2d, merged, wo, g, rw, wg, wu, wd, fin, tm, tm_e):
    x1, ids, wts = _router(x2d, merged, wo, g, rw, tm)
    tile_expert, src_tok, dst_row, w_sorted = _routing_plan(ids, wts, tm_e)
    y2 = _experts(x1, g, wg, wu, wd, tile_expert, src_tok, dst_row, w_sorted, tm_e)
    return _combine(x1, y2, fin, tm)


def _state_to_block_diag(s):
    b = s.shape[0]
    s = s.reshape(b, N_GROUPS, HEADS_PER_GROUP, HEAD_DIM, HEAD_DIM)
    eye = jnp.eye(HEADS_PER_GROUP, dtype=s.dtype)
    bd = jnp.einsum('bghij,hk->bghikj', s, eye)
    return bd.reshape(b, N_GROUPS, LANES, LANES)


def _state_from_block_diag(bd):
    b = bd.shape[0]
    bd = bd.reshape(b, N_GROUPS, HEADS_PER_GROUP, HEAD_DIM, HEADS_PER_GROUP, HEAD_DIM)
    idx = jnp.arange(HEADS_PER_GROUP)
    s = bd[:, :, idx, :, idx, :]
    return jnp.moveaxis(s, 0, 2).reshape(b, N_HEADS, HEAD_DIM, HEAD_DIM)


def _prep_layer_weights(l, w_in, conv_w, shift_mu, decay_w0, decay_w2, aaa_a0, aaa_a2, gate_g2,
                        key_k, key_a, bonus_r_k, lnx_w, lnx_b, w_out):
    zero = jnp.zeros((DECAY_LORA, D_RWKV), F32)
    wwa = jnp.concatenate([jnp.concatenate([decay_w2[l], zero], axis=1),
                           jnp.concatenate([zero, aaa_a2[l]], axis=1)], axis=0).astype(BF16)
    vecs = jnp.stack([decay_w0[l], aaa_a0[l], key_k[l], key_a[l], bonus_r_k[l], lnx_w[l], lnx_b[l],
                      jnp.zeros((D_RWKV,), F32)])
    return dict(w_in=w_in[l].astype(BF16), convw=conv_w[l], mu=shift_mu[l][None], vecs=vecs,
                wwa=wwa, g2=gate_g2[l].astype(BF16), w_out=w_out[l].astype(BF16))


def _row_tile(n, want):
    return want if n % want == 0 else n


def _trunk(x, conv_st, shift_st, wkv_st, layers, ffn_norm, dense, moe, final_norm):
    batch, seq, d = x.shape
    n = batch * seq
    x2d = x.reshape(n, d)
    tm = _row_tile(n, 512)
    convs, shifts, wkvs = [], [], []
    for l, lw in enumerate(layers):
        p = _norm_proj(x2d, lw['norm'], lw['w_in'], tm, D_PROJ // 6)
        merged, c, s, w = _mixer(p, conv_st[l], shift_st[l][:, None], _state_to_block_diag(wkv_st[l]),
                                 lw['convw'], lw['mu'], lw['vecs'], lw['wwa'], lw['g2'],
                                 batch, seq)
        convs.append(c)
        shifts.append(s[:, 0])
        wkvs.append(_state_from_block_diag(w))
        if l % 2 == 0:
            wg, wu, wd = dense
            x2d = _ffn_dense(x2d, merged, lw['w_out'], ffn_norm[l][None], wg, wu, wd, tm,
                             wg.shape[1] // 2)
        else:
            rw, wg, wu, wd = moe
            x2d = _ffn_moe(x2d, merged, lw['w_out'], ffn_norm[l][None], rw, wg, wu, wd,
                           final_norm[None], tm, _row_tile(2 * n, 512))
    return x2d.reshape(batch, seq, d), jnp.stack(convs), jnp.stack(shifts), jnp.stack(wkvs)


def kernel(x_prompt, x_sample, state_conv, state_shift, state_wkv, mix_norm, w_in, conv_w, shift_mu,
           decay_w0, decay_w2, aaa_a0, aaa_a2, gate_g2, key_k, key_a, bonus_r_k, lnx_w, lnx_b, w_out,
           ffn_norm, ffn_w_gate, ffn_w_up, ffn_w_down, router_w, moe_w_gate, moe_w_up, moe_w_down,
           final_norm):
    depth = w_in.shape[0]
    assert depth == 2 and ffn_w_gate.shape[0] == 1 and moe_w_gate.shape[0] == 1
    layers = []
    for l in range(depth):
        lw = _prep_layer_weights(l, w_in, conv_w, shift_mu, decay_w0, decay_w2, aaa_a0, aaa_a2,
                                 gate_g2, key_k, key_a, bonus_r_k, lnx_w, lnx_b, w_out)
        lw['norm'] = mix_norm[l][None]
        layers.append(lw)
    dense = (ffn_w_gate[0].astype(BF16), ffn_w_up[0].astype(BF16), ffn_w_down[0].astype(BF16))
    rw = jnp.pad(router_w[0], ((0, 0), (0, LANES - N_EXPERTS)))
    moe = (rw, moe_w_gate[0].astype(BF16), moe_w_up[0].astype(BF16), moe_w_down[0].astype(BF16))

    b = x_prompt.shape[0]
    zero_conv = jnp.zeros((depth, b) + state_conv.shape[2:], state_conv.dtype)
    zero_shift = jnp.zeros((depth, b) + state_shift.shape[2:], state_shift.dtype)
    zero_wkv = jnp.zeros((depth, b) + state_wkv.shape[2:], state_wkv.dtype)
    run = functools.partial(_trunk, layers=layers, ffn_norm=ffn_norm, dense=dense, moe=moe,
                            final_norm=final_norm)
    y_s, conv_s, shift_s, wkv_s = run(x_sample, state_conv, state_shift, state_wkv)
    y_p, conv_p, shift_p, wkv_p = run(x_prompt, zero_conv, zero_shift, zero_wkv)
    return (y_p, y_s, conv_p, shift_p, wkv_p, conv_s, shift_s, wkv_s)
```

```python
import functools

import jax
import jax.numpy as jnp
from jax import lax
from jax.experimental import pallas as pl
from jax.experimental.pallas import tpu as pltpu

F32 = jnp.float32
BF16 = jnp.bfloat16

D_MODEL = 1024
N_HEADS = 16
HEAD_DIM = 64
D_RWKV = N_HEADS * HEAD_DIM
D_CONV = 1024
CONV_W = 3
DECAY_LORA = 64
AAA_LORA = 64
GATE_LORA = 128
D_SHIFT = 3 * D_RWKV + DECAY_LORA + AAA_LORA + GATE_LORA
D_PROJ = 2 * D_MODEL + 3 * D_CONV + D_SHIFT
N_EXPERTS = 8
RMS_EPS = 1e-5
GN_EPS = 64e-5
L2_EPS = 1e-12

LANES = 128
SUBLANES = 8
CHUNK = 64
HEADS_PER_GROUP = LANES // HEAD_DIM
N_GROUPS = N_HEADS // HEADS_PER_GROUP
VMEM_LIMIT = 56 * 1024 * 1024

_ZA, _ZB, _CB, _CC, _CHH, _PS = (0, D_MODEL, 2 * D_MODEL, 2 * D_MODEL + D_CONV,
                                 2 * D_MODEL + 2 * D_CONV, 2 * D_MODEL + 3 * D_CONV)
_R, _K, _V, _WA, _G = 0, D_RWKV, 2 * D_RWKV, 3 * D_RWKV, 3 * D_RWKV + DECAY_LORA + AAA_LORA


def _params(semantics):
    return pltpu.CompilerParams(dimension_semantics=semantics, vmem_limit_bytes=VMEM_LIMIT)


def _rms(x, g):
    ms = jnp.mean(x * x, axis=-1, keepdims=True)
    return x * lax.rsqrt(ms + RMS_EPS) * g


def _split(x):
    hi = x.astype(BF16)
    lo = (x - hi.astype(F32)).astype(BF16)
    return hi, lo


_NN = (((1,), (0,)), ((), ()))
_NT = (((1,), (1,)), ((), ()))


def _dot(a, b, dims=_NN):
    return lax.dot_general(a, b, dims, preferred_element_type=F32)


def _mm1(a, b, dims=_NN):
    return _dot(a.astype(BF16), b.astype(BF16), dims)


def _mm3(a, b, dims=_NN):
    a1, a2 = _split(a)
    b1, b2 = _split(b)
    return _dot(a1, b1, dims) + (_dot(a1, b2, dims) + _dot(a2, b1, dims))


def _mm_exact_rhs(a, b):
    a1, a2 = _split(a)
    bb = b.astype(BF16)
    return _dot(a1, bb) + _dot(a2, bb)


def _norm_proj_kernel(x_ref, g_ref, w_ref, o_ref, xn_ref):
    @pl.when(pl.program_id(1) == 0)
    def _():
        xn_ref[...] = _rms(x_ref[...], g_ref[...]).astype(BF16)

    o_ref[...] = _dot(xn_ref[...], w_ref[...])


def _norm_proj(x2d, g, w_bf16, tm, tn):
    n, d = x2d.shape
    dp = w_bf16.shape[1]
    return pl.pallas_call(
        _norm_proj_kernel,
        out_shape=jax.ShapeDtypeStruct((n, dp), F32),
        grid=(n // tm, dp // tn),
        in_specs=[pl.BlockSpec((tm, d), lambda i, j: (i, 0)),
                  pl.BlockSpec((1, d), lambda i, j: (0, 0)),
                  pl.BlockSpec((d, tn), lambda i, j: (0, j))],
        out_specs=pl.BlockSpec((tm, tn), lambda i, j: (i, j)),
        scratch_shapes=[pltpu.VMEM((tm, d), BF16)],
        compiler_params=_params(("arbitrary", "arbitrary")),
        name="norm_proj",
    )(x2d, g, w_bf16)


def _block_diag(x, head_masks):
    x = x.astype(BF16)
    return jnp.concatenate([x * m for m in head_masks], axis=0)


def _wkv_chunk(r, k, v, kkn, a, ld, cum, s_prev, head_masks, causal2, bd_mask):
    groups = range(len(r))
    bd = functools.partial(_block_diag, head_masks=head_masks)
    n_steps = CHUNK.bit_length() - 1
    g_b, g_k, g_s, bd_v, b_h, k_h, cend = [], [], [], [], [], [], []
    for g in groups:
        ce = cum[g][CHUNK - 1:CHUNK, :]
        w_prev = jnp.exp(cum[g] - ld[g])
        w_t = jnp.exp(cum[g])
        w_inv = jnp.exp(-cum[g])
        w_rest = jnp.exp(ce - cum[g])
        b = kkn[g] * a[g]
        lhs2 = jnp.concatenate([-kkn[g] * w_prev, r[g] * w_t], axis=0)
        rhs = jnp.concatenate([bd(b * w_inv), bd(k[g] * w_inv), s_prev[g].astype(BF16)], axis=0)
        gram = _mm1(lhs2, rhs, _NT)
        g_b.append(jnp.where(causal2, gram[:, :LANES], 0.0))
        g_k.append(jnp.where(causal2, gram[:, LANES:2 * LANES], 0.0))
        g_s.append(gram[:, 2 * LANES:])
        bd_v.append(bd(v[g]))
        b_h.append(b * w_rest)
        k_h.append(k[g] * w_rest)
        cend.append(ce)

    u = [g_s[g][:CHUNK] + _mm1(g_k[g][:CHUNK], bd_v[g]) for g in groups]
    pw = [g_b[g][:CHUNK] for g in groups]
    for i in range(n_steps):
        for g in groups:
            if i + 1 < n_steps:
                res = _mm1(pw[g], jnp.concatenate([bd(u[g]), bd(pw[g])], axis=1))
                u[g] = u[g] + res[:, :LANES]
                pw[g] = res[:, LANES:]
            else:
                u[g] = u[g] + _mm1(pw[g], bd(u[g]))

    y, s_new = [], []
    for g in groups:
        l_r = jnp.concatenate([g_b[g][CHUNK:], g_k[g][CHUNK:]], axis=1)
        y.append(g_s[g][CHUNK:] + _mm1(l_r, jnp.concatenate([bd(u[g]), bd_v[g]], axis=0)))
    for g in groups:
        uv_t = jnp.concatenate([u[g], v[g]], axis=0).T
        upd = _mm1(uv_t, jnp.concatenate([b_h[g], k_h[g]], axis=0))
        s_new.append(s_prev[g] * jnp.exp(cend[g]) + jnp.where(bd_mask, upd, 0.0))
    return y, s_new


def _mixer_kernel(p_ref, conv0_ref, shift0_ref, wkv0_ref, convw_ref, mu_ref, vec_ref, wwa_ref,
                  g2_ref, merged_ref, nconv_ref, nshift_ref, nwkv_ref, cbuf, sbuf, state,
                  *, n_chunks):
    t = pl.program_id(1)

    @pl.when(t == 0)
    def _():
        cbuf[0:SUBLANES, :] = jnp.zeros((SUBLANES, D_CONV), F32)
        cbuf[SUBLANES - (CONV_W - 1):SUBLANES, :] = conv0_ref[0]
        sbuf[0:SUBLANES, :] = jnp.zeros((SUBLANES, D_SHIFT), F32)
        sbuf[SUBLANES - 1:SUBLANES, :] = shift0_ref[0]
        zero = jnp.zeros((HEAD_DIM, HEAD_DIM), F32)
        for g in range(N_GROUPS):
            h0 = g * HEADS_PER_GROUP
            state[g] = jnp.concatenate(
                [jnp.concatenate([wkv0_ref[0, h0 + h] if hh == h else zero
                                  for hh in range(HEADS_PER_GROUP)], axis=1)
                 for h in range(HEADS_PER_GROUP)], axis=0)

    c0 = SUBLANES
    ch = p_ref[:, _CC:_CC + D_CONV] * p_ref[:, _CHH:_CHH + D_CONV]
    cbuf[c0:c0 + CHUNK, :] = ch
    conv = (cbuf[c0 - 2:c0 - 2 + CHUNK, :] * convw_ref[0:1, :]
            + cbuf[c0 - 1:c0 - 1 + CHUNK, :] * convw_ref[1:2, :]
            + ch * convw_ref[2:3, :])
    y_a = jax.nn.sigmoid(p_ref[:, _ZA:_ZA + D_MODEL]) * (p_ref[:, _CB:_CB + D_CONV] * conv)
    new_conv = cbuf[c0 + CHUNK - (CONV_W - 1):c0 + CHUNK, :]
    cbuf[0:SUBLANES, :] = cbuf[CHUNK:CHUNK + SUBLANES, :]

    ps = p_ref[:, _PS:_PS + D_SHIFT]
    sbuf[c0:c0 + CHUNK, :] = ps
    prev = sbuf[c0 - 1:c0 - 1 + CHUNK, :]
    xm = ps + (prev - ps) * mu_ref[...]
    new_shift = sbuf[c0 + CHUNK - 1:c0 + CHUNK, :]
    sbuf[0:SUBLANES, :] = sbuf[CHUNK:CHUNK + SUBLANES, :]

    w0, a0, k_k, k_a, r_k, lnx_w, lnx_b = (vec_ref[i:i + 1, :] for i in range(7))

    lane = lax.broadcasted_iota(jnp.int32, (CHUNK, LANES), 1)
    wa_in = xm[:, _WA:_WA + LANES]
    wa_in = jnp.where(lane < DECAY_LORA, jnp.tanh(wa_in), wa_in)
    wa = _mm1(wa_in, wwa_ref[...])
    gate = _mm1(jax.nn.sigmoid(xm[:, _G:_G + GATE_LORA]), g2_ref[...])
    ld_all = -jnp.exp(-0.5) * jax.nn.sigmoid(w0 + wa[:, :D_RWKV])
    a_all = jax.nn.sigmoid(a0 + wa[:, D_RWKV:])
    row = lax.broadcasted_iota(jnp.int32, (CHUNK, CHUNK), 0)
    col = lax.broadcasted_iota(jnp.int32, (CHUNK, CHUNK), 1)
    tri = jnp.where(col <= row, 1.0, 0.0).astype(BF16)
    ld1 = ld_all.astype(BF16)
    ld2 = (ld_all - ld1.astype(F32)).astype(BF16)
    ld3 = (ld_all - ld1.astype(F32) - ld2.astype(F32)).astype(BF16)
    cum_all = _dot(tri, ld1) + (_dot(tri, ld2) + _dot(tri, ld3))

    head_shift = HEAD_DIM.bit_length() - 1
    head_masks = [jnp.where((lane >> head_shift) == h, 1.0, 0.0).astype(BF16)
                  for h in range(HEADS_PER_GROUP)]
    row2 = lax.broadcasted_iota(jnp.int32, (2 * CHUNK, LANES), 0)
    lane2 = lax.broadcasted_iota(jnp.int32, (2 * CHUNK, LANES), 1)
    causal2 = (lane2 & (HEAD_DIM - 1)) < (row2 & (CHUNK - 1)) + (row2 >> (CHUNK.bit_length() - 1))
    rowl = lax.broadcasted_iota(jnp.int32, (LANES, LANES), 0)
    lanel = lax.broadcasted_iota(jnp.int32, (LANES, LANES), 1)
    bd_mask = (rowl >> head_shift) == (lanel >> head_shift)
    seg_ones = jnp.where(bd_mask, 1.0, 0.0)

    groups = range(N_GROUPS)
    sls = [slice(g * LANES, (g + 1) * LANES) for g in groups]
    r = [xm[:, _R + g * LANES:_R + (g + 1) * LANES] for g in groups]
    k = [xm[:, _K + g * LANES:_K + (g + 1) * LANES] for g in groups]
    v = [xm[:, _V + g * LANES:_V + (g + 1) * LANES] for g in groups]
    a = [a_all[:, sl] for sl in sls]
    kk = [k[g] * k_k[:, sls[g]] for g in groups]
    norm = [jnp.sqrt(_mm_exact_rhs(kk[g] * kk[g], seg_ones)) for g in groups]
    kkn = [kk[g] / jnp.maximum(norm[g], L2_EPS) for g in groups]
    kf = [k[g] * (1.0 + (a[g] - 1.0) * k_a[:, sls[g]]) for g in groups]
    bonus = [_mm_exact_rhs(r[g] * kf[g] * r_k[:, sls[g]], seg_ones) * v[g] for g in groups]
    y, s_new = _wkv_chunk(r, kf, v, kkn, a, [ld_all[:, sl] for sl in sls],
                          [cum_all[:, sl] for sl in sls], [state[g] for g in groups],
                          head_masks, causal2, bd_mask)
    for g in groups:
        state[g] = s_new[g]
    mean = [_mm_exact_rhs(y[g], seg_ones) * (1.0 / HEAD_DIM) for g in groups]
    dev = [y[g] - mean[g] for g in groups]
    var = [_mm_exact_rhs(dev[g] * dev[g], seg_ones) * (1.0 / HEAD_DIM) for g in groups]
    for g in groups:
        sl = sls[g]
        yn = dev[g] * lax.rsqrt(var[g] + GN_EPS) * lnx_w[:, sl] + lnx_b[:, sl]
        y_b = (yn + bonus[g]) * gate[:, sl]
        z_b = p_ref[:, _ZB + g * LANES:_ZB + (g + 1) * LANES]
        merged_ref[:, sl] = (y_a[:, sl] + jax.nn.sigmoid(z_b) * y_b).astype(merged_ref.dtype)

    @pl.when(t == n_chunks - 1)
    def _():
        nconv_ref[0] = new_conv
        nshift_ref[0] = new_shift
        for h in range(N_HEADS):
            o = (h % HEADS_PER_GROUP) * HEAD_DIM
            nwkv_ref[0, h] = state[h // HEADS_PER_GROUP, o:o + HEAD_DIM, o:o + HEAD_DIM]


def _mixer(p2d, conv0, shift0, wkv0, convw, mu, vecs, wwa, g2, batch, seq):
    n_chunks = seq // CHUNK
    kern = functools.partial(_mixer_kernel, n_chunks=n_chunks)
    const = lambda b, t: (0, 0)
    per_b3 = lambda b, t: (b, 0, 0)
    per_b4 = lambda b, t: (b, 0, 0, 0)
    return pl.pallas_call(
        kern,
        out_shape=(jax.ShapeDtypeStruct((batch * seq, D_MODEL), BF16),
                   jax.ShapeDtypeStruct((batch, CONV_W - 1, D_CONV), F32),
                   jax.ShapeDtypeStruct((batch, 1, D_SHIFT), F32),
                   jax.ShapeDtypeStruct((batch, N_HEADS, HEAD_DIM, HEAD_DIM), F32)),
        grid=(batch, n_chunks),
        in_specs=[pl.BlockSpec((CHUNK, D_PROJ), lambda b, t: (b * n_chunks + t, 0)),
                  pl.BlockSpec((1, CONV_W - 1, D_CONV), per_b3),
                  pl.BlockSpec((1, 1, D_SHIFT), per_b3),
                  pl.BlockSpec((1, N_HEADS, HEAD_DIM, HEAD_DIM), per_b4),
                  pl.BlockSpec((CONV_W, D_CONV), const),
                  pl.BlockSpec((1, D_SHIFT), const),
                  pl.BlockSpec((SUBLANES, D_RWKV), const),
                  pl.BlockSpec((LANES, 2 * D_RWKV), const),
                  pl.BlockSpec((GATE_LORA, D_RWKV), const)],
        out_specs=(pl.BlockSpec((CHUNK, D_MODEL), lambda b, t: (b * n_chunks + t, 0)),
                   pl.BlockSpec((1, CONV_W - 1, D_CONV), per_b3),
                   pl.BlockSpec((1, 1, D_SHIFT), per_b3),
                   pl.BlockSpec((1, N_HEADS, HEAD_DIM, HEAD_DIM), per_b4)),
        scratch_shapes=[pltpu.VMEM((CHUNK + SUBLANES, D_CONV), F32),
                        pltpu.VMEM((CHUNK + SUBLANES, D_SHIFT), F32),
                        pltpu.VMEM((N_GROUPS, LANES, LANES), F32)],
        compiler_params=_params(("arbitrary", "arbitrary")),
        name="mixer",
    )(p2d, conv0, shift0, wkv0, convw, mu, vecs, wwa, g2)


def _silu_mul(gate, up):
    return gate * jax.nn.sigmoid(gate) * up


def _ffn_dense_kernel(x_ref, m_ref, wo_ref, g_ref, wg_ref, wu_ref, wd_ref, o_ref, h_ref):
    @pl.when(pl.program_id(1) == 0)
    def _():
        x1 = x_ref[...] + _dot(m_ref[...], wo_ref[...])
        o_ref[...] = x1
        h_ref[...] = _rms(x1, g_ref[...]).astype(BF16)

    h = h_ref[...]
    act = _silu_mul(_dot(h, wg_ref[...]), _dot(h, wu_ref[...])).astype(BF16)
    o_ref[...] += _dot(act, wd_ref[...])


def _ffn_dense(x2d, merged, wo, g, wg, wu, wd, tm, tf):
    n, d = x2d.shape
    f = wg.shape[1]
    return pl.pallas_call(
        _ffn_dense_kernel,
        out_shape=jax.ShapeDtypeStruct((n, d), F32),
        grid=(n // tm, f // tf),
        in_specs=[pl.BlockSpec((tm, d), lambda i, j: (i, 0)),
                  pl.BlockSpec((tm, d), lambda i, j: (i, 0)),
                  pl.BlockSpec((d, d), lambda i, j: (0, 0)),
                  pl.BlockSpec((1, d), lambda i, j: (0, 0)),
                  pl.BlockSpec((d, tf), lambda i, j: (0, j)),
                  pl.BlockSpec((d, tf), lambda i, j: (0, j)),
                  pl.BlockSpec((tf, d), lambda i, j: (j, 0))],
        out_specs=pl.BlockSpec((tm, d), lambda i, j: (i, 0)),
        scratch_shapes=[pltpu.VMEM((tm, d), BF16)],
        compiler_params=_params(("arbitrary", "arbitrary")),
        name="ffn_dense",
    )(x2d, merged, wo, g, wg, wu, wd)


def _top2(logits):
    lane_i = lax.broadcasted_iota(jnp.int32, logits.shape, 1)
    lane = lane_i.astype(F32)
    neg = jnp.float32(-jnp.inf)
    lg = jnp.where(lane_i < N_EXPERTS, logits, neg)
    m1 = jnp.max(lg, axis=-1, keepdims=True)
    i1 = jnp.min(jnp.where(lg == m1, lane, float(LANES)), axis=-1, keepdims=True)
    lg2 = jnp.where(lane == i1, neg, lg)
    m2 = jnp.max(lg2, axis=-1, keepdims=True)
    i2 = jnp.min(jnp.where(lg2 == m2, lane, float(LANES)), axis=-1, keepdims=True)
    e2 = jnp.exp(m2 - m1)
    den = 1.0 + e2
    ids = jnp.where(lane_i == 0, i1, jnp.where(lane_i == 1, i2, 0.0)).astype(jnp.int32)
    wts = jnp.where(lane_i == 0, 1.0 / den, jnp.where(lane_i == 1, e2 / den, 0.0))
    return ids, wts


def _router_kernel(x_ref, m_ref, wo_ref, g_ref, rw_ref, x1_ref, ids_ref, wts_ref):
    x1 = x_ref[...] + _dot(m_ref[...], wo_ref[...])
    x1_ref[...] = x1
    ids_ref[...], wts_ref[...] = _top2(_mm3(_rms(x1, g_ref[...]), rw_ref[...]))


def _router(x2d, merged, wo, g, rw, tm):
    n, d = x2d.shape
    row = lambda i: (i, 0)
    const = lambda i: (0, 0)
    return pl.pallas_call(
        _router_kernel,
        out_shape=(jax.ShapeDtypeStruct((n, d), F32),
                   jax.ShapeDtypeStruct((n, LANES), jnp.int32),
                   jax.ShapeDtypeStruct((n, LANES), F32)),
        grid=(n // tm,),
        in_specs=[pl.BlockSpec((tm, d), row), pl.BlockSpec((tm, d), row),
                  pl.BlockSpec((d, d), const), pl.BlockSpec((1, d), const),
                  pl.BlockSpec((d, LANES), const)],
        out_specs=(pl.BlockSpec((tm, d), row), pl.BlockSpec((tm, LANES), row),
                   pl.BlockSpec((tm, LANES), row)),
        compiler_params=_params(("arbitrary",)),
        name="router",
    )(x2d, merged, wo, g, rw)


def _experts_kernel(te_ref, inv_ref, x1_hbm, g_ref, wg_ref, wu_ref, wd_ref,
                    y_hbm, xbuf, obuf, gather_sem, scatter_sem, *, tm, n_tok, f_split):
    t = pl.program_id(0)

    @pl.when(t == 0)
    def _():
        obuf[...] = jnp.zeros_like(obuf)
        spare = pltpu.make_async_copy(obuf, y_hbm.at[pl.ds(2 * n_tok, tm)], scatter_sem)
        spare.start()
        spare.wait()

    @pl.when(te_ref[t] < N_EXPERTS)
    def _():
        def gather_start(r, c):
            tok = jnp.maximum(inv_ref[0, 0, r], 0) >> 1
            pltpu.make_async_copy(x1_hbm.at[pl.ds(tok, 1)], xbuf.at[pl.ds(r, 1)], gather_sem).start()
            return c

        lax.fori_loop(0, tm, gather_start, 0, unroll=8)
        pltpu.make_async_copy(x1_hbm.at[pl.ds(0, tm)], xbuf, gather_sem).wait()

        h = _rms(xbuf[...], g_ref[...]).astype(BF16)
        f = wg_ref.shape[2]
        fs = f // f_split
        acc = None
        for i in range(f_split):
            act = _silu_mul(_dot(h, wg_ref[0, :, i * fs:(i + 1) * fs]),
                            _dot(h, wu_ref[0, :, i * fs:(i + 1) * fs])).astype(BF16)
            part = _dot(act, wd_ref[0, i * fs:(i + 1) * fs, :])
            acc = part if acc is None else acc + part
        obuf[...] = acc

        def scatter_start(r, c):
            a = inv_ref[0, 0, r]
            row = jnp.where(a >= 0, (a & 1) * n_tok + (a >> 1), 2 * n_tok + r)
            pltpu.make_async_copy(obuf.at[pl.ds(r, 1)], y_hbm.at[pl.ds(row, 1)], scatter_sem).start()
            return c

        lax.fori_loop(0, tm, scatter_start, 0, unroll=8)
        pltpu.make_async_copy(obuf, y_hbm.at[pl.ds(0, tm)], scatter_sem).wait()


def _experts(x1, g, wg, wu, wd, tile_expert, inv, tm):
    n, d = x1.shape
    n_tiles = tile_expert.shape[0]
    f = wg.shape[2]
    expert_w = lambda t, te: (jnp.minimum(te[t], N_EXPERTS - 1), 0, 0)
    single = pl.Buffered(1)
    grid_spec = pltpu.PrefetchScalarGridSpec(
        num_scalar_prefetch=1,
        grid=(n_tiles,),
        in_specs=[pl.BlockSpec((1, 1, tm), lambda t, te: (t, 0, 0), memory_space=pltpu.SMEM),
                  pl.BlockSpec(memory_space=pl.ANY),
                  pl.BlockSpec((1, d), lambda t, te: (0, 0)),
                  pl.BlockSpec((1, d, f), expert_w, pipeline_mode=single),
                  pl.BlockSpec((1, d, f), expert_w, pipeline_mode=single),
                  pl.BlockSpec((1, f, d), expert_w, pipeline_mode=single)],
        out_specs=pl.BlockSpec(memory_space=pl.ANY),
        scratch_shapes=[pltpu.VMEM((tm, d), F32), pltpu.VMEM((tm, d), F32),
                        pltpu.SemaphoreType.DMA(()), pltpu.SemaphoreType.DMA(())])
    return pl.pallas_call(
        functools.partial(_experts_kernel, tm=tm, n_tok=n, f_split=2),
        out_shape=jax.ShapeDtypeStruct((2 * n + tm, d), F32),
        grid_spec=grid_spec,
        compiler_params=_params(("arbitrary",)),
        name="experts",
    )(tile_expert, inv.reshape(n_tiles, 1, tm), x1, g, wg, wu, wd)


def _combine_kernel(x1_ref, y0_ref, y1_ref, wts_ref, fin_ref, o_ref):
    w = wts_ref[...]
    moe = w[:, 0:1] * y0_ref[...] + w[:, 1:2] * y1_ref[...]
    o_ref[...] = _rms(x1_ref[...] + moe, fin_ref[...])


def _combine(x1, y2, wts, fin, tm):
    n, d = x1.shape
    n_blocks = n // tm
    return pl.pallas_call(
        _combine_kernel,
        out_shape=jax.ShapeDtypeStruct((n, d), F32),
        grid=(n_blocks,),
        in_specs=[pl.BlockSpec((tm, d), lambda i: (i, 0)),
                  pl.BlockSpec((tm, d), lambda i: (i, 0)),
                  pl.BlockSpec((tm, d), lambda i: (i + n_blocks, 0)),
                  pl.BlockSpec((tm, LANES), lambda i: (i, 0)),
                  pl.BlockSpec((1, d), lambda i: (0, 0))],
        out_specs=pl.BlockSpec((tm, d), lambda i: (i, 0)),
        compiler_params=_params(("arbitrary",)),
        name="combine",
    )(x1, y2, y2, wts, fin)


def _routing_plan(ids, tm):
    n = ids.shape[0]
    e_flat = ids[:, :2].reshape(-1)
    onehot = (e_flat[:, None] == jnp.arange(N_EXPERTS, dtype=jnp.int32)[None, :]).astype(jnp.int32)
    csum = jnp.cumsum(onehot, axis=0)
    counts = csum[-1]
    padded = (counts + tm - 1) // tm * tm
    ends = jnp.cumsum(padded)
    pos = jnp.sum(onehot * (csum - 1 + (ends - padded)[None, :]), axis=1)
    total = 2 * n + N_EXPERTS * tm
    inv = jnp.full((total,), -1, jnp.int32).at[pos].set(jnp.arange(2 * n, dtype=jnp.int32))
    tile_start = jnp.arange(total // tm, dtype=jnp.int32) * tm
    tile_expert = jnp.sum((tile_start[:, None] >= ends[None, :]).astype(jnp.int32), axis=1)
    return tile_expert, inv


def _ffn_moe(x2d, merged, wo, g, rw, wg, wu, wd, fin, tm, tm_e):
    x1, ids, wts = _router(x2d, merged, wo, g, rw, tm)
    tile_expert, inv = _routing_plan(ids, tm_e)
    y2 = _experts(x1, g, wg, wu, wd, tile_expert, inv, tm_e)
    return _combine(x1, y2, wts, fin, tm)


def _prep_layer_weights(l, w_in, conv_w, shift_mu, decay_w0, decay_w2, aaa_a0, aaa_a2, gate_g2,
                        key_k, key_a, bonus_r_k, lnx_w, lnx_b, w_out):
    zero = jnp.zeros((DECAY_LORA, D_RWKV), F32)
    wwa = jnp.concatenate([jnp.concatenate([decay_w2[l], zero], axis=1),
                           jnp.concatenate([zero, aaa_a2[l]], axis=1)], axis=0).astype(BF16)
    vecs = jnp.stack([decay_w0[l], aaa_a0[l], key_k[l], key_a[l], bonus_r_k[l], lnx_w[l], lnx_b[l],
                      jnp.zeros((D_RWKV,), F32)])
    return dict(w_in=w_in[l].astype(BF16), convw=conv_w[l], mu=shift_mu[l][None], vecs=vecs,
                wwa=wwa, g2=gate_g2[l].astype(BF16), w_out=w_out[l].astype(BF16))


def _row_tile(n, want):
    return want if n % want == 0 else n


def _trunk(x, conv_st, shift_st, wkv_st, layers, ffn_norm, dense, moe, final_norm):
    batch, seq, d = x.shape
    n = batch * seq
    x2d = x.reshape(n, d)
    tm = _row_tile(n, 512)
    convs, shifts, wkvs = [], [], []
    for l, lw in enumerate(layers):
        p = _norm_proj(x2d, lw['norm'], lw['w_in'], _row_tile(n, 1024), D_PROJ // 6)
        merged, c, s, w = _mixer(p, conv_st[l], shift_st[l][:, None], wkv_st[l],
                                 lw['convw'], lw['mu'], lw['vecs'], lw['wwa'], lw['g2'],
                                 batch, seq)
        convs.append(c)
        shifts.append(s[:, 0])
        wkvs.append(w)
        if l % 2 == 0:
            wg, wu, wd = dense
            x2d = _ffn_dense(x2d, merged, lw['w_out'], ffn_norm[l][None], wg, wu, wd, tm,
                             wg.shape[1] // 2)
        else:
            rw, wg, wu, wd = moe
            x2d = _ffn_moe(x2d, merged, lw['w_out'], ffn_norm[l][None], rw, wg, wu, wd,
                           final_norm[None], tm, _row_tile(2 * n, 512))
    return x2d.reshape(batch, seq, d), jnp.stack(convs), jnp.stack(shifts), jnp.stack(wkvs)


def kernel(x_prompt, x_sample, state_conv, state_shift, state_wkv, mix_norm, w_in, conv_w, shift_mu,
           decay_w0, decay_w2, aaa_a0, aaa_a2, gate_g2, key_k, key_a, bonus_r_k, lnx_w, lnx_b, w_out,
           ffn_norm, ffn_w_gate, ffn_w_up, ffn_w_down, router_w, moe_w_gate, moe_w_up, moe_w_down,
           final_norm):
    depth = w_in.shape[0]
    assert depth == 2 and ffn_w_gate.shape[0] == 1 and moe_w_gate.shape[0] == 1
    layers = []
    for l in range(depth):
        lw = _prep_layer_weights(l, w_in, conv_w, shift_mu, decay_w0, decay_w2, aaa_a0, aaa_a2,
                                 gate_g2, key_k, key_a, bonus_r_k, lnx_w, lnx_b, w_out)
        lw['norm'] = mix_norm[l][None]
        layers.append(lw)
    dense = (ffn_w_gate[0].astype(BF16), ffn_w_up[0].astype(BF16), ffn_w_down[0].astype(BF16))
    rw = jnp.pad(router_w[0], ((0, 0), (0, LANES - N_EXPERTS)))
    moe = (rw, moe_w_gate[0].astype(BF16), moe_w_up[0].astype(BF16), moe_w_down[0].astype(BF16))

    b = x_prompt.shape[0]
    zero_conv = jnp.zeros((depth, b) + state_conv.shape[2:], state_conv.dtype)
    zero_shift = jnp.zeros((depth, b) + state_shift.shape[2:], state_shift.dtype)
    zero_wkv = jnp.zeros((depth, b) + state_wkv.shape[2:], state_wkv.dtype)
    run = functools.partial(_trunk, layers=layers, ffn_norm=ffn_norm, dense=dense, moe=moe,
                            final_norm=final_norm)
    y_s, conv_s, shift_s, wkv_s = run(x_sample, state_conv, state_shift, state_wkv)
    y_p, conv_p, shift_p, wkv_p = run(x_prompt, zero_conv, zero_shift, zero_wkv)
    return (y_p, y_s, conv_p, shift_p, wkv_p, conv_s, shift_s, wkv_s)
```

```python
import functools

import jax
import jax.numpy as jnp
from jax import lax
from jax.experimental import pallas as pl
from jax.experimental.pallas import tpu as pltpu

F32 = jnp.float32
BF16 = jnp.bfloat16

D_MODEL = 1024
N_HEADS = 16
HEAD_DIM = 64
D_RWKV = N_HEADS * HEAD_DIM
D_CONV = 1024
CONV_W = 3
DECAY_LORA = 64
AAA_LORA = 64
GATE_LORA = 128
D_SHIFT = 3 * D_RWKV + DECAY_LORA + AAA_LORA + GATE_LORA
D_PROJ = 2 * D_MODEL + 3 * D_CONV + D_SHIFT
N_EXPERTS = 8
RMS_EPS = 1e-5
GN_EPS = 64e-5
L2_EPS = 1e-12

LANES = 128
SUBLANES = 8
CHUNK = 64
HEADS_PER_GROUP = LANES // HEAD_DIM
SEQS_PER_STEP = 2
N_GROUPS = N_HEADS // HEADS_PER_GROUP
VMEM_LIMIT = 56 * 1024 * 1024

_ZA, _ZB, _CB, _CC, _CHH, _PS = (0, D_MODEL, 2 * D_MODEL, 2 * D_MODEL + D_CONV,
                                 2 * D_MODEL + 2 * D_CONV, 2 * D_MODEL + 3 * D_CONV)
_R, _K, _V, _WA, _G = 0, D_RWKV, 2 * D_RWKV, 3 * D_RWKV, 3 * D_RWKV + DECAY_LORA + AAA_LORA


def _params(semantics):
    return pltpu.CompilerParams(dimension_semantics=semantics, vmem_limit_bytes=VMEM_LIMIT)


def _rms(x, g):
    ms = jnp.mean(x * x, axis=-1, keepdims=True)
    return x * lax.rsqrt(ms + RMS_EPS) * g


def _split(x):
    hi = x.astype(BF16)
    lo = (x - hi.astype(F32)).astype(BF16)
    return hi, lo


_NN = (((1,), (0,)), ((), ()))
_NT = (((1,), (1,)), ((), ()))


def _dot(a, b, dims=_NN):
    return lax.dot_general(a, b, dims, preferred_element_type=F32)


def _mm1(a, b, dims=_NN):
    return _dot(a.astype(BF16), b.astype(BF16), dims)


def _mm3(a, b, dims=_NN):
    a1, a2 = _split(a)
    b1, b2 = _split(b)
    return _dot(a1, b1, dims) + (_dot(a1, b2, dims) + _dot(a2, b1, dims))


def _segment_sums(xs, seg_ones):
    rows = xs[0].shape[0]
    parts = [half for x in xs for half in _split(x)]
    out = _dot(jnp.concatenate(parts, axis=0), seg_ones)
    return [out[2 * i * rows:(2 * i + 1) * rows] + out[(2 * i + 1) * rows:(2 * i + 2) * rows]
            for i in range(len(xs))]


def _norm_proj_kernel(x_ref, g_ref, w_ref, o_ref, xn_ref):
    @pl.when(pl.program_id(1) == 0)
    def _():
        xn_ref[...] = _rms(x_ref[...], g_ref[...]).astype(BF16)

    o_ref[...] = _dot(xn_ref[...], w_ref[...])


def _norm_proj(x2d, g, w_bf16, tm, tn):
    n, d = x2d.shape
    dp = w_bf16.shape[1]
    return pl.pallas_call(
        _norm_proj_kernel,
        out_shape=jax.ShapeDtypeStruct((n, dp), F32),
        grid=(n // tm, dp // tn),
        in_specs=[pl.BlockSpec((tm, d), lambda i, j: (i, 0)),
                  pl.BlockSpec((1, d), lambda i, j: (0, 0)),
                  pl.BlockSpec((d, tn), lambda i, j: (0, j))],
        out_specs=pl.BlockSpec((tm, tn), lambda i, j: (i, j)),
        scratch_shapes=[pltpu.VMEM((tm, d), BF16)],
        compiler_params=_params(("arbitrary", "arbitrary")),
        name="norm_proj",
    )(x2d, g, w_bf16)


def _block_diag(x, head_masks):
    x = x.astype(BF16)
    return jnp.concatenate([x * m for m in head_masks], axis=0)


def _wkv_chunk(r, k, v, kkn, a, ld, cum, s_prev, head_masks, causal2, bd_mask):
    groups = range(len(r))
    bd = functools.partial(_block_diag, head_masks=head_masks)
    n_steps = CHUNK.bit_length() - 1
    g_b, g_k, g_s, bd_v, b_h, k_h, cend = [], [], [], [], [], [], []
    for g in groups:
        ce = cum[g][CHUNK - 1:CHUNK, :]
        w_prev = jnp.exp(cum[g] - ld[g])
        w_t = jnp.exp(cum[g])
        w_inv = jnp.exp(-cum[g])
        w_rest = jnp.exp(ce - cum[g])
        b = kkn[g] * a[g]
        lhs2 = jnp.concatenate([-kkn[g] * w_prev, r[g] * w_t], axis=0)
        rhs = jnp.concatenate([bd(b * w_inv), bd(k[g] * w_inv), s_prev[g].astype(BF16)], axis=0)
        gram = _mm1(lhs2, rhs, _NT)
        g_b.append(jnp.where(causal2, gram[:, :LANES], 0.0))
        g_k.append(jnp.where(causal2, gram[:, LANES:2 * LANES], 0.0))
        g_s.append(gram[:, 2 * LANES:])
        bd_v.append(bd(v[g]))
        b_h.append(b * w_rest)
        k_h.append(k[g] * w_rest)
        cend.append(ce)

    u = [g_s[g][:CHUNK] + _mm1(g_k[g][:CHUNK], bd_v[g]) for g in groups]
    pw = [g_b[g][:CHUNK] for g in groups]
    for i in range(n_steps):
        for g in groups:
            if i + 1 < n_steps:
                res = _mm1(pw[g], jnp.concatenate([bd(u[g]), bd(pw[g])], axis=1))
                u[g] = u[g] + res[:, :LANES]
                pw[g] = res[:, LANES:]
            else:
                u[g] = u[g] + _mm1(pw[g], bd(u[g]))

    y, s_new = [], []
    for g in groups:
        l_r = jnp.concatenate([g_b[g][CHUNK:], g_k[g][CHUNK:]], axis=1)
        y.append(g_s[g][CHUNK:] + _mm1(l_r, jnp.concatenate([bd(u[g]), bd_v[g]], axis=0)))
    for g in groups:
        uv_t = jnp.concatenate([u[g], v[g]], axis=0).T
        upd = _mm1(uv_t, jnp.concatenate([b_h[g], k_h[g]], axis=0))
        s_new.append(s_prev[g] * jnp.exp(cend[g]) + jnp.where(bd_mask, upd, 0.0))
    return y, s_new


def _mixer_kernel(p_ref, conv0_ref, shift0_ref, wkv0_ref, convw_ref, mu_ref, vec_ref, wwa_ref,
                  g2_ref, merged_ref, nconv_ref, nshift_ref, nwkv_ref, cbuf, sbuf, state,
                  *, n_chunks, n_seq):
    t = pl.program_id(1)
    seqs = range(n_seq)
    groups = range(N_GROUPS)
    pairs = [(s, g) for s in seqs for g in groups]

    @pl.when(t == 0)
    def _():
        zero = jnp.zeros((HEAD_DIM, HEAD_DIM), F32)
        for s in seqs:
            cbuf[s, 0:SUBLANES, :] = jnp.zeros((SUBLANES, D_CONV), F32)
            cbuf[s, SUBLANES - (CONV_W - 1):SUBLANES, :] = conv0_ref[s]
            sbuf[s, 0:SUBLANES, :] = jnp.zeros((SUBLANES, D_SHIFT), F32)
            sbuf[s, SUBLANES - 1:SUBLANES, :] = shift0_ref[s]
            for g in groups:
                h0 = g * HEADS_PER_GROUP
                state[s * N_GROUPS + g] = jnp.concatenate(
                    [jnp.concatenate([wkv0_ref[s, h0 + h] if hh == h else zero
                                      for hh in range(HEADS_PER_GROUP)], axis=1)
                     for h in range(HEADS_PER_GROUP)], axis=0)

    w0, a0, k_k, k_a, r_k, lnx_w, lnx_b = (vec_ref[i:i + 1, :] for i in range(7))
    lane = lax.broadcasted_iota(jnp.int32, (CHUNK, LANES), 1)
    row = lax.broadcasted_iota(jnp.int32, (CHUNK, CHUNK), 0)
    col = lax.broadcasted_iota(jnp.int32, (CHUNK, CHUNK), 1)
    tri = jnp.where(col <= row, 1.0, 0.0).astype(BF16)
    head_shift = HEAD_DIM.bit_length() - 1
    head_masks = [jnp.where((lane >> head_shift) == h, 1.0, 0.0).astype(BF16)
                  for h in range(HEADS_PER_GROUP)]
    row2 = lax.broadcasted_iota(jnp.int32, (2 * CHUNK, LANES), 0)
    lane2 = lax.broadcasted_iota(jnp.int32, (2 * CHUNK, LANES), 1)
    causal2 = (lane2 & (HEAD_DIM - 1)) < (row2 & (CHUNK - 1)) + (row2 >> (CHUNK.bit_length() - 1))
    rowl = lax.broadcasted_iota(jnp.int32, (LANES, LANES), 0)
    lanel = lax.broadcasted_iota(jnp.int32, (LANES, LANES), 1)
    bd_mask = (rowl >> head_shift) == (lanel >> head_shift)
    seg_ones = jnp.where(bd_mask, 1.0, 0.0).astype(BF16)
    sls = [slice(g * LANES, (g + 1) * LANES) for g in groups]

    c0 = SUBLANES
    y_a, gate, new_conv, new_shift = [], [], [], []
    r, k, v, a, ld, cum = [], [], [], [], [], []
    for s in seqs:
        ch = p_ref[s, :, _CC:_CC + D_CONV] * p_ref[s, :, _CHH:_CHH + D_CONV]
        cbuf[s, c0:c0 + CHUNK, :] = ch
        conv = (cbuf[s, c0 - 2:c0 - 2 + CHUNK, :] * convw_ref[0:1, :]
                + cbuf[s, c0 - 1:c0 - 1 + CHUNK, :] * convw_ref[1:2, :]
                + ch * convw_ref[2:3, :])
        y_a.append(jax.nn.sigmoid(p_ref[s, :, _ZA:_ZA + D_MODEL])
                   * (p_ref[s, :, _CB:_CB + D_CONV] * conv))
        new_conv.append(cbuf[s, c0 + CHUNK - (CONV_W - 1):c0 + CHUNK, :])
        cbuf[s, 0:SUBLANES, :] = cbuf[s, CHUNK:CHUNK + SUBLANES, :]

        ps = p_ref[s, :, _PS:_PS + D_SHIFT]
        sbuf[s, c0:c0 + CHUNK, :] = ps
        prev = sbuf[s, c0 - 1:c0 - 1 + CHUNK, :]
        xm = ps + (prev - ps) * mu_ref[...]
        new_shift.append(sbuf[s, c0 + CHUNK - 1:c0 + CHUNK, :])
        sbuf[s, 0:SUBLANES, :] = sbuf[s, CHUNK:CHUNK + SUBLANES, :]

        wa_in = xm[:, _WA:_WA + LANES]
        wa_in = jnp.where(lane < DECAY_LORA, jnp.tanh(wa_in), wa_in)
        wa = _mm1(wa_in, wwa_ref[...])
        gate.append(_mm1(jax.nn.sigmoid(xm[:, _G:_G + GATE_LORA]), g2_ref[...]))
        ld_all = -jnp.exp(-0.5) * jax.nn.sigmoid(w0 + wa[:, :D_RWKV])
        a_all = jax.nn.sigmoid(a0 + wa[:, D_RWKV:])
        ld1 = ld_all.astype(BF16)
        ld2 = (ld_all - ld1.astype(F32)).astype(BF16)
        ld3 = (ld_all - ld1.astype(F32) - ld2.astype(F32)).astype(BF16)
        cum_all = _dot(tri, ld1) + (_dot(tri, ld2) + _dot(tri, ld3))
        for g in groups:
            r.append(xm[:, _R + g * LANES:_R + (g + 1) * LANES])
            k.append(xm[:, _K + g * LANES:_K + (g + 1) * LANES])
            v.append(xm[:, _V + g * LANES:_V + (g + 1) * LANES])
            a.append(a_all[:, sls[g]])
            ld.append(ld_all[:, sls[g]])
            cum.append(cum_all[:, sls[g]])

    n_pairs = len(pairs)
    kk = [k[i] * k_k[:, sls[g]] for i, (s, g) in enumerate(pairs)]
    kf = [k[i] * (1.0 + (a[i] - 1.0) * k_a[:, sls[g]]) for i, (s, g) in enumerate(pairs)]
    sums = _segment_sums([x * x for x in kk]
                         + [r[i] * kf[i] * r_k[:, sls[g]] for i, (s, g) in enumerate(pairs)],
                         seg_ones)
    kkn = [kk[i] / jnp.maximum(jnp.sqrt(sums[i]), L2_EPS) for i in range(n_pairs)]
    bonus = [sums[n_pairs + i] * v[i] for i in range(n_pairs)]
    y, s_new = _wkv_chunk(r, kf, v, kkn, a, ld, cum, [state[i] for i in range(n_pairs)],
                          head_masks, causal2, bd_mask)
    for i in range(n_pairs):
        state[i] = s_new[i]
    mean = [m * (1.0 / HEAD_DIM) for m in _segment_sums(y, seg_ones)]
    dev = [y[i] - mean[i] for i in range(n_pairs)]
    var = [m * (1.0 / HEAD_DIM) for m in _segment_sums([d * d for d in dev], seg_ones)]
    for i, (s, g) in enumerate(pairs):
        sl = sls[g]
        yn = dev[i] * lax.rsqrt(var[i] + GN_EPS) * lnx_w[:, sl] + lnx_b[:, sl]
        y_b = (yn + bonus[i]) * gate[s][:, sl]
        z_b = p_ref[s, :, _ZB + g * LANES:_ZB + (g + 1) * LANES]
        merged_ref[s, :, sl] = (y_a[s][:, sl] + jax.nn.sigmoid(z_b) * y_b).astype(merged_ref.dtype)

    @pl.when(t == n_chunks - 1)
    def _():
        for s in seqs:
            nconv_ref[s] = new_conv[s]
            nshift_ref[s] = new_shift[s]
            for h in range(N_HEADS):
                o = (h % HEADS_PER_GROUP) * HEAD_DIM
                nwkv_ref[s, h] = state[s * N_GROUPS + h // HEADS_PER_GROUP,
                                       o:o + HEAD_DIM, o:o + HEAD_DIM]


def _mixer(p2d, conv0, shift0, wkv0, convw, mu, vecs, wwa, g2, batch, seq):
    n_chunks = seq // CHUNK
    n_seq = SEQS_PER_STEP if batch % SEQS_PER_STEP == 0 else 1
    kern = functools.partial(_mixer_kernel, n_chunks=n_chunks, n_seq=n_seq)
    const = lambda b, t: (0, 0)
    per_b3 = lambda b, t: (b, 0, 0)
    per_b4 = lambda b, t: (b, 0, 0, 0)
    merged, conv, shift, wkv = pl.pallas_call(
        kern,
        out_shape=(jax.ShapeDtypeStruct((batch, seq, D_MODEL), BF16),
                   jax.ShapeDtypeStruct((batch, CONV_W - 1, D_CONV), F32),
                   jax.ShapeDtypeStruct((batch, 1, D_SHIFT), F32),
                   jax.ShapeDtypeStruct((batch, N_HEADS, HEAD_DIM, HEAD_DIM), F32)),
        grid=(batch // n_seq, n_chunks),
        in_specs=[pl.BlockSpec((n_seq, CHUNK, D_PROJ), lambda b, t: (b, t, 0)),
                  pl.BlockSpec((n_seq, CONV_W - 1, D_CONV), per_b3),
                  pl.BlockSpec((n_seq, 1, D_SHIFT), per_b3),
                  pl.BlockSpec((n_seq, N_HEADS, HEAD_DIM, HEAD_DIM), per_b4),
                  pl.BlockSpec((CONV_W, D_CONV), const),
                  pl.BlockSpec((1, D_SHIFT), const),
                  pl.BlockSpec((SUBLANES, D_RWKV), const),
                  pl.BlockSpec((LANES, 2 * D_RWKV), const),
                  pl.BlockSpec((GATE_LORA, D_RWKV), const)],
        out_specs=(pl.BlockSpec((n_seq, CHUNK, D_MODEL), lambda b, t: (b, t, 0)),
                   pl.BlockSpec((n_seq, CONV_W - 1, D_CONV), per_b3),
                   pl.BlockSpec((n_seq, 1, D_SHIFT), per_b3),
                   pl.BlockSpec((n_seq, N_HEADS, HEAD_DIM, HEAD_DIM), per_b4)),
        scratch_shapes=[pltpu.VMEM((n_seq, CHUNK + SUBLANES, D_CONV), F32),
                        pltpu.VMEM((n_seq, CHUNK + SUBLANES, D_SHIFT), F32),
                        pltpu.VMEM((n_seq * N_GROUPS, LANES, LANES), F32)],
        compiler_params=_params(("arbitrary", "arbitrary")),
        name="mixer",
    )(p2d.reshape(batch, seq, D_PROJ), conv0, shift0, wkv0, convw, mu, vecs, wwa, g2)
    return merged.reshape(batch * seq, D_MODEL), conv, shift, wkv


def _silu_mul(gate, up):
    return gate * jax.nn.sigmoid(gate) * up


def _ffn_dense_kernel(x_ref, m_ref, wo_ref, g_ref, wg_ref, wu_ref, wd_ref, o_ref, h_ref):
    @pl.when(pl.program_id(1) == 0)
    def _():
        x1 = x_ref[...] + _dot(m_ref[...], wo_ref[...])
        o_ref[...] = x1
        h_ref[...] = _rms(x1, g_ref[...]).astype(BF16)

    h = h_ref[...]
    act = _silu_mul(_dot(h, wg_ref[...]), _dot(h, wu_ref[...])).astype(BF16)
    o_ref[...] += _dot(act, wd_ref[...])


def _ffn_dense(x2d, merged, wo, g, wg, wu, wd, tm, tf):
    n, d = x2d.shape
    f = wg.shape[1]
    return pl.pallas_call(
        _ffn_dense_kernel,
        out_shape=jax.ShapeDtypeStruct((n, d), F32),
        grid=(n // tm, f // tf),
        in_specs=[pl.BlockSpec((tm, d), lambda i, j: (i, 0)),
                  pl.BlockSpec((tm, d), lambda i, j: (i, 0)),
                  pl.BlockSpec((d, d), lambda i, j: (0, 0)),
                  pl.BlockSpec((1, d), lambda i, j: (0, 0)),
                  pl.BlockSpec((d, tf), lambda i, j: (0, j)),
                  pl.BlockSpec((d, tf), lambda i, j: (0, j)),
                  pl.BlockSpec((tf, d), lambda i, j: (j, 0))],
        out_specs=pl.BlockSpec((tm, d), lambda i, j: (i, 0)),
        scratch_shapes=[pltpu.VMEM((tm, d), BF16)],
        compiler_params=_params(("arbitrary", "arbitrary")),
        name="ffn_dense",
    )(x2d, merged, wo, g, wg, wu, wd)


def _top2(logits):
    lane_i = lax.broadcasted_iota(jnp.int32, logits.shape, 1)
    lane = lane_i.astype(F32)
    neg = jnp.float32(-jnp.inf)
    lg = jnp.where(lane_i < N_EXPERTS, logits, neg)
    m1 = jnp.max(lg, axis=-1, keepdims=True)
    i1 = jnp.min(jnp.where(lg == m1, lane, float(LANES)), axis=-1, keepdims=True)
    lg2 = jnp.where(lane == i1, neg, lg)
    m2 = jnp.max(lg2, axis=-1, keepdims=True)
    i2 = jnp.min(jnp.where(lg2 == m2, lane, float(LANES)), axis=-1, keepdims=True)
    e2 = jnp.exp(m2 - m1)
    den = 1.0 + e2
    ids = jnp.where(lane_i == 0, i1, jnp.where(lane_i == 1, i2, 0.0)).astype(jnp.int32)
    wts = jnp.where(lane_i == 0, 1.0 / den, jnp.where(lane_i == 1, e2 / den, 0.0))
    return ids, wts


def _router_kernel(x_ref, m_ref, wo_ref, g_ref, rw_ref, x1_ref, ids_ref, wts_ref):
    x1 = x_ref[...] + _dot(m_ref[...], wo_ref[...])
    x1_ref[...] = x1
    ids_ref[...], wts_ref[...] = _top2(_mm3(_rms(x1, g_ref[...]), rw_ref[...]))


def _router(x2d, merged, wo, g, rw, tm):
    n, d = x2d.shape
    row = lambda i: (i, 0)
    const = lambda i: (0, 0)
    return pl.pallas_call(
        _router_kernel,
        out_shape=(jax.ShapeDtypeStruct((n, d), F32),
                   jax.ShapeDtypeStruct((n, LANES), jnp.int32),
                   jax.ShapeDtypeStruct((n, LANES), F32)),
        grid=(n // tm,),
        in_specs=[pl.BlockSpec((tm, d), row), pl.BlockSpec((tm, d), row),
                  pl.BlockSpec((d, d), const), pl.BlockSpec((1, d), const),
                  pl.BlockSpec((d, LANES), const)],
        out_specs=(pl.BlockSpec((tm, d), row), pl.BlockSpec((tm, LANES), row),
                   pl.BlockSpec((tm, LANES), row)),
        compiler_params=_params(("arbitrary",)),
        name="router",
    )(x2d, merged, wo, g, rw)


def _experts_kernel(te_ref, nv_ref, inv_ref, inv_next_ref, x1_hbm, g_ref, wg_ref, wu_ref, wd_ref,
                    y_hbm, xbuf, obuf, gather_sem, scatter_sem, *, tm, n_tok, f_split):
    t = pl.program_id(0)
    n_used = nv_ref[0]
    cur = t & 1

    def start_gather(idx_ref, buf):
        def body(r, c):
            tok = jnp.maximum(idx_ref[0, 0, r], 0) >> 1
            pltpu.make_async_copy(x1_hbm.at[pl.ds(tok, 1)], xbuf.at[buf, pl.ds(r, 1)],
                                  gather_sem.at[buf]).start()
            return c
        lax.fori_loop(0, tm, body, 0, unroll=8)

    def wait_scatter():
        pltpu.make_async_copy(obuf, y_hbm.at[pl.ds(0, tm)], scatter_sem).wait()

    @pl.when(t == 0)
    def _():
        obuf[...] = jnp.zeros_like(obuf)
        spare = pltpu.make_async_copy(obuf, y_hbm.at[pl.ds(2 * n_tok, tm)], scatter_sem)
        spare.start()
        spare.wait()
        start_gather(inv_ref, 0)

    @pl.when(t < n_used)
    def _():
        pltpu.make_async_copy(x1_hbm.at[pl.ds(0, tm)], xbuf.at[cur], gather_sem.at[cur]).wait()

        @pl.when(t + 1 < n_used)
        def _():
            start_gather(inv_next_ref, 1 - cur)

        h = _rms(xbuf[cur], g_ref[...]).astype(BF16)
        f = wg_ref.shape[2]
        fs = f // f_split
        acc = None
        for i in range(f_split):
            act = _silu_mul(_dot(h, wg_ref[0, :, i * fs:(i + 1) * fs]),
                            _dot(h, wu_ref[0, :, i * fs:(i + 1) * fs])).astype(BF16)
            part = _dot(act, wd_ref[0, i * fs:(i + 1) * fs, :])
            acc = part if acc is None else acc + part

        @pl.when(t > 0)
        def _():
            wait_scatter()

        obuf[...] = acc

        def scatter_start(r, c):
            a = inv_ref[0, 0, r]
            row = jnp.where(a >= 0, (a & 1) * n_tok + (a >> 1), 2 * n_tok + r)
            pltpu.make_async_copy(obuf.at[pl.ds(r, 1)], y_hbm.at[pl.ds(row, 1)], scatter_sem).start()
            return c

        lax.fori_loop(0, tm, scatter_start, 0, unroll=8)

        @pl.when(t == n_used - 1)
        def _():
            wait_scatter()


def _experts(x1, g, wg, wu, wd, tile_expert, n_used, inv, tm):
    n, d = x1.shape
    n_tiles = tile_expert.shape[0]
    f = wg.shape[2]
    expert_w = lambda t, te, nv: (jnp.minimum(te[t], N_EXPERTS - 1), 0, 0)
    single = pl.Buffered(1)
    inv3 = inv.reshape(n_tiles, 1, tm)
    grid_spec = pltpu.PrefetchScalarGridSpec(
        num_scalar_prefetch=2,
        grid=(n_tiles,),
        in_specs=[pl.BlockSpec((1, 1, tm), lambda t, te, nv: (t, 0, 0), memory_space=pltpu.SMEM),
                  pl.BlockSpec((1, 1, tm), lambda t, te, nv: (jnp.minimum(t + 1, n_tiles - 1), 0, 0),
                               memory_space=pltpu.SMEM),
                  pl.BlockSpec(memory_space=pl.ANY),
                  pl.BlockSpec((1, d), lambda t, te, nv: (0, 0)),
                  pl.BlockSpec((1, d, f), expert_w, pipeline_mode=single),
                  pl.BlockSpec((1, d, f), expert_w, pipeline_mode=single),
                  pl.BlockSpec((1, f, d), expert_w, pipeline_mode=single)],
        out_specs=pl.BlockSpec(memory_space=pl.ANY),
        scratch_shapes=[pltpu.VMEM((2, tm, d), F32), pltpu.VMEM((tm, d), F32),
                        pltpu.SemaphoreType.DMA((2,)), pltpu.SemaphoreType.DMA(())])
    return pl.pallas_call(
        functools.partial(_experts_kernel, tm=tm, n_tok=n, f_split=2),
        out_shape=jax.ShapeDtypeStruct((2 * n + tm, d), F32),
        grid_spec=grid_spec,
        compiler_params=_params(("arbitrary",)),
        name="experts",
    )(tile_expert, n_used, inv3, inv3, x1, g, wg, wu, wd)


def _combine_kernel(x1_ref, y0_ref, y1_ref, wts_ref, fin_ref, o_ref):
    w = wts_ref[...]
    moe = w[:, 0:1] * y0_ref[...] + w[:, 1:2] * y1_ref[...]
    o_ref[...] = _rms(x1_ref[...] + moe, fin_ref[...])


def _combine(x1, y2, wts, fin, tm):
    n, d = x1.shape
    n_blocks = n // tm
    return pl.pallas_call(
        _combine_kernel,
        out_shape=jax.ShapeDtypeStruct((n, d), F32),
        grid=(n_blocks,),
        in_specs=[pl.BlockSpec((tm, d), lambda i: (i, 0)),
                  pl.BlockSpec((tm, d), lambda i: (i, 0)),
                  pl.BlockSpec((tm, d), lambda i: (i + n_blocks, 0)),
                  pl.BlockSpec((tm, LANES), lambda i: (i, 0)),
                  pl.BlockSpec((1, d), lambda i: (0, 0))],
        out_specs=pl.BlockSpec((tm, d), lambda i: (i, 0)),
        compiler_params=_params(("arbitrary",)),
        name="combine",
    )(x1, y2, y2, wts, fin)


def _routing_plan(ids, tm):
    n = ids.shape[0]
    e_flat = ids[:, :2].reshape(-1)
    onehot = (e_flat[:, None] == jnp.arange(N_EXPERTS, dtype=jnp.int32)[None, :]).astype(jnp.int32)
    csum = jnp.cumsum(onehot, axis=0)
    counts = csum[-1]
    padded = (counts + tm - 1) // tm * tm
    ends = jnp.cumsum(padded)
    pos = jnp.sum(onehot * (csum - 1 + (ends - padded)[None, :]), axis=1)
    total = 2 * n + N_EXPERTS * tm
    inv = jnp.full((total,), -1, jnp.int32).at[pos].set(jnp.arange(2 * n, dtype=jnp.int32))
    tile_start = jnp.arange(total // tm, dtype=jnp.int32) * tm
    tile_expert = jnp.sum((tile_start[:, None] >= ends[None, :]).astype(jnp.int32), axis=1)
    return tile_expert, (ends[-1:] // tm).astype(jnp.int32), inv


def _ffn_moe(x2d, merged, wo, g, rw, wg, wu, wd, fin, tm, tm_e):
    x1, ids, wts = _router(x2d, merged, wo, g, rw, tm)
    tile_expert, n_used, inv = _routing_plan(ids, tm_e)
    y2 = _experts(x1, g, wg, wu, wd, tile_expert, n_used, inv, tm_e)
    return _combine(x1, y2, wts, fin, tm)


def _prep_layer_weights(l, w_in, conv_w, shift_mu, decay_w0, decay_w2, aaa_a0, aaa_a2, gate_g2,
                        key_k, key_a, bonus_r_k, lnx_w, lnx_b, w_out):
    zero = jnp.zeros((DECAY_LORA, D_RWKV), F32)
    wwa = jnp.concatenate([jnp.concatenate([decay_w2[l], zero], axis=1),
                           jnp.concatenate([zero, aaa_a2[l]], axis=1)], axis=0).astype(BF16)
    vecs = jnp.stack([decay_w0[l], aaa_a0[l], key_k[l], key_a[l], bonus_r_k[l], lnx_w[l], lnx_b[l],
                      jnp.zeros((D_RWKV,), F32)])
    return dict(w_in=w_in[l].astype(BF16), convw=conv_w[l], mu=shift_mu[l][None], vecs=vecs,
                wwa=wwa, g2=gate_g2[l].astype(BF16), w_out=w_out[l].astype(BF16))


def _row_tile(n, want):
    return want if n % want == 0 else n


def _tiles(n):
    big_experts = 2 * n >= 4 * 512 * N_EXPERTS
    return _row_tile(n, 512), _row_tile(n, 1024), _row_tile(2 * n, 512 if big_experts else 256)


def _trunk(x, conv_st, shift_st, wkv_st, layers, ffn_norm, dense, moe, final_norm):
    batch, seq, d = x.shape
    n = batch * seq
    x2d = x.reshape(n, d)
    tm, tm_proj, tm_expert = _tiles(n)
    convs, shifts, wkvs = [], [], []
    for l, lw in enumerate(layers):
        p = _norm_proj(x2d, lw['norm'], lw['w_in'], tm_proj, D_PROJ // 6)
        merged, c, s, w = _mixer(p, conv_st[l], shift_st[l][:, None], wkv_st[l],
                                 lw['convw'], lw['mu'], lw['vecs'], lw['wwa'], lw['g2'],
                                 batch, seq)
        convs.append(c)
        shifts.append(s[:, 0])
        wkvs.append(w)
        if l % 2 == 0:
            wg, wu, wd = dense
            x2d = _ffn_dense(x2d, merged, lw['w_out'], ffn_norm[l][None], wg, wu, wd, tm,
                             wg.shape[1] // 2)
        else:
            rw, wg, wu, wd = moe
            x2d = _ffn_moe(x2d, merged, lw['w_out'], ffn_norm[l][None], rw, wg, wu, wd,
                           final_norm[None], tm, tm_expert)
    return x2d.reshape(batch, seq, d), jnp.stack(convs), jnp.stack(shifts), jnp.stack(wkvs)


def kernel(x_prompt, x_sample, state_conv, state_shift, state_wkv, mix_norm, w_in, conv_w, shift_mu,
           decay_w0, decay_w2, aaa_a0, aaa_a2, gate_g2, key_k, key_a, bonus_r_k, lnx_w, lnx_b, w_out,
           ffn_norm, ffn_w_gate, ffn_w_up, ffn_w_down, router_w, moe_w_gate, moe_w_up, moe_w_down,
           final_norm):
    depth = w_in.shape[0]
    assert depth == 2 and ffn_w_gate.shape[0] == 1 and moe_w_gate.shape[0] == 1
    layers = []
    for l in range(depth):
        lw = _prep_layer_weights(l, w_in, conv_w, shift_mu, decay_w0, decay_w2, aaa_a0, aaa_a2,
                                 gate_g2, key_k, key_a, bonus_r_k, lnx_w, lnx_b, w_out)
        lw['norm'] = mix_norm[l][None]
        layers.append(lw)
    dense = (ffn_w_gate[0].astype(BF16), ffn_w_up[0].astype(BF16), ffn_w_down[0].astype(BF16))
    rw = jnp.pad(router_w[0], ((0, 0), (0, LANES - N_EXPERTS)))
    moe = (rw, moe_w_gate[0].astype(BF16), moe_w_up[0].astype(BF16), moe_w_down[0].astype(BF16))

    b = x_prompt.shape[0]
    zero_conv = jnp.zeros((depth, b) + state_conv.shape[2:], state_conv.dtype)
    zero_shift = jnp.zeros((depth, b) + state_shift.shape[2:], state_shift.dtype)
    zero_wkv = jnp.zeros((depth, b) + state_wkv.shape[2:], state_wkv.dtype)
    run = functools.partial(_trunk, layers=layers, ffn_norm=ffn_norm, dense=dense, moe=moe,
                            final_norm=final_norm)
    y_s, conv_s, shift_s, wkv_s = run(x_sample, state_conv, state_shift, state_wkv)
    y_p, conv_p, shift_p, wkv_p = run(x_prompt, zero_conv, zero_shift, zero_wkv)
    return (y_p, y_s, conv_p, shift_p, wkv_p, conv_s, shift_s, wkv_s)
```

```python
import functools

import jax
import jax.numpy as jnp
from jax import lax
from jax.experimental import pallas as pl
from jax.experimental.pallas import tpu as pltpu

F32 = jnp.float32
BF16 = jnp.bfloat16

D_MODEL = 1024
N_HEADS = 16
HEAD_DIM = 64
D_RWKV = N_HEADS * HEAD_DIM
D_CONV = 1024
CONV_W = 3
DECAY_LORA = 64
AAA_LORA = 64
GATE_LORA = 128
D_SHIFT = 3 * D_RWKV + DECAY_LORA + AAA_LORA + GATE_LORA
D_PROJ = 2 * D_MODEL + 3 * D_CONV + D_SHIFT
N_EXPERTS = 8
RMS_EPS = 1e-5
GN_EPS = 64e-5
L2_EPS = 1e-12

LANES = 128
SUBLANES = 8
CHUNK = 64
HEADS_PER_GROUP = LANES // HEAD_DIM
SEQS_PER_STEP = 2
N_GROUPS = N_HEADS // HEADS_PER_GROUP
VMEM_LIMIT = 56 * 1024 * 1024

_ZA, _ZB, _CB, _CC, _CHH, _PS = (0, D_MODEL, 2 * D_MODEL, 2 * D_MODEL + D_CONV,
                                 2 * D_MODEL + 2 * D_CONV, 2 * D_MODEL + 3 * D_CONV)
_R, _K, _V, _WA, _G = 0, D_RWKV, 2 * D_RWKV, 3 * D_RWKV, 3 * D_RWKV + DECAY_LORA + AAA_LORA


def _params(semantics):
    return pltpu.CompilerParams(dimension_semantics=semantics, vmem_limit_bytes=VMEM_LIMIT)


def _rms(x, g):
    ms = jnp.mean(x * x, axis=-1, keepdims=True)
    return x * lax.rsqrt(ms + RMS_EPS) * g


def _split(x):
    hi = x.astype(BF16)
    lo = (x - hi.astype(F32)).astype(BF16)
    return hi, lo


_NN = (((1,), (0,)), ((), ()))
_NT = (((1,), (1,)), ((), ()))


def _dot(a, b, dims=_NN):
    return lax.dot_general(a, b, dims, preferred_element_type=F32)


def _mm1(a, b, dims=_NN):
    return _dot(a.astype(BF16), b.astype(BF16), dims)


def _mm3(a, b, dims=_NN):
    a1, a2 = _split(a)
    b1, b2 = _split(b)
    return _dot(a1, b1, dims) + (_dot(a1, b2, dims) + _dot(a2, b1, dims))


def _segment_sums(xs, seg_ones):
    rows = xs[0].shape[0]
    parts = [half for x in xs for half in _split(x)]
    out = _dot(jnp.concatenate(parts, axis=0), seg_ones)
    return [out[2 * i * rows:(2 * i + 1) * rows] + out[(2 * i + 1) * rows:(2 * i + 2) * rows]
            for i in range(len(xs))]


def _norm_proj_kernel(x_ref, g_ref, w_ref, o_ref, xn_ref):
    @pl.when(pl.program_id(1) == 0)
    def _():
        xn_ref[...] = _rms(x_ref[...], g_ref[...]).astype(BF16)

    o_ref[...] = _dot(xn_ref[...], w_ref[...])


def _norm_proj(x2d, g, w_bf16, tm, tn):
    n, d = x2d.shape
    dp = w_bf16.shape[1]
    return pl.pallas_call(
        _norm_proj_kernel,
        out_shape=jax.ShapeDtypeStruct((n, dp), F32),
        grid=(n // tm, dp // tn),
        in_specs=[pl.BlockSpec((tm, d), lambda i, j: (i, 0)),
                  pl.BlockSpec((1, d), lambda i, j: (0, 0)),
                  pl.BlockSpec((d, tn), lambda i, j: (0, j))],
        out_specs=pl.BlockSpec((tm, tn), lambda i, j: (i, j)),
        scratch_shapes=[pltpu.VMEM((tm, d), BF16)],
        compiler_params=_params(("arbitrary", "arbitrary")),
        name="norm_proj",
    )(x2d, g, w_bf16)


def _block_diag(x, head_masks):
    x = x.astype(BF16)
    return jnp.concatenate([x * m for m in head_masks], axis=0)


def _wkv_chunk(r, k, v, kkn, a, ld, cum, s_prev, head_masks, causal2, bd_mask):
    groups = range(len(r))
    bd = functools.partial(_block_diag, head_masks=head_masks)
    n_steps = CHUNK.bit_length() - 1
    g_b, g_k, g_s, bd_v, b_h, k_h, cend = [], [], [], [], [], [], []
    for g in groups:
        ce = cum[g][CHUNK - 1:CHUNK, :]
        w_prev = jnp.exp(cum[g] - ld[g])
        w_t = jnp.exp(cum[g])
        w_inv = jnp.exp(-cum[g])
        w_rest = jnp.exp(ce - cum[g])
        b = kkn[g] * a[g]
        lhs2 = jnp.concatenate([-kkn[g] * w_prev, r[g] * w_t], axis=0)
        rhs = jnp.concatenate([bd(b * w_inv), bd(k[g] * w_inv), s_prev[g].astype(BF16)], axis=0)
        gram = _mm1(lhs2, rhs, _NT)
        g_b.append(jnp.where(causal2, gram[:, :LANES], 0.0))
        g_k.append(jnp.where(causal2, gram[:, LANES:2 * LANES], 0.0))
        g_s.append(gram[:, 2 * LANES:])
        bd_v.append(bd(v[g]))
        b_h.append(b * w_rest)
        k_h.append(k[g] * w_rest)
        cend.append(ce)

    u = [g_s[g][:CHUNK] + _mm1(g_k[g][:CHUNK], bd_v[g]) for g in groups]
    pw = [g_b[g][:CHUNK] for g in groups]
    for i in range(n_steps):
        for g in groups:
            if i + 1 < n_steps:
                res = _mm1(pw[g], jnp.concatenate([bd(u[g]), bd(pw[g])], axis=1))
                u[g] = u[g] + res[:, :LANES]
                pw[g] = res[:, LANES:]
            else:
                u[g] = u[g] + _mm1(pw[g], bd(u[g]))

    y, s_new = [], []
    for g in groups:
        l_r = jnp.concatenate([g_b[g][CHUNK:], g_k[g][CHUNK:]], axis=1)
        y.append(g_s[g][CHUNK:] + _mm1(l_r, jnp.concatenate([bd(u[g]), bd_v[g]], axis=0)))
    for g in groups:
        uv_t = jnp.concatenate([u[g], v[g]], axis=0).T
        upd = _mm1(uv_t, jnp.concatenate([b_h[g], k_h[g]], axis=0))
        s_new.append(s_prev[g] * jnp.exp(cend[g]) + jnp.where(bd_mask, upd, 0.0))
    return y, s_new


def _mixer_kernel(p_ref, conv0_ref, shift0_ref, wkv0_ref, convw_ref, mu_ref, vec_ref, wwa_ref,
                  g2_ref, merged_ref, nconv_ref, nshift_ref, nwkv_ref, cbuf, sbuf, state,
                  *, n_chunks, n_seq):
    t = pl.program_id(1)
    seqs = range(n_seq)
    groups = range(N_GROUPS)
    pairs = [(s, g) for s in seqs for g in groups]

    @pl.when(t == 0)
    def _():
        zero = jnp.zeros((HEAD_DIM, HEAD_DIM), F32)
        for s in seqs:
            cbuf[s, 0:SUBLANES, :] = jnp.zeros((SUBLANES, D_CONV), F32)
            cbuf[s, SUBLANES - (CONV_W - 1):SUBLANES, :] = conv0_ref[s]
            sbuf[s, 0:SUBLANES, :] = jnp.zeros((SUBLANES, D_SHIFT), F32)
            sbuf[s, SUBLANES - 1:SUBLANES, :] = shift0_ref[s]
            for g in groups:
                h0 = g * HEADS_PER_GROUP
                state[s * N_GROUPS + g] = jnp.concatenate(
                    [jnp.concatenate([wkv0_ref[s, h0 + h] if hh == h else zero
                                      for hh in range(HEADS_PER_GROUP)], axis=1)
                     for h in range(HEADS_PER_GROUP)], axis=0)

    w0, a0, k_k, k_a, r_k, lnx_w, lnx_b = (vec_ref[i:i + 1, :] for i in range(7))
    lane = lax.broadcasted_iota(jnp.int32, (CHUNK, LANES), 1)
    row = lax.broadcasted_iota(jnp.int32, (CHUNK, CHUNK), 0)
    col = lax.broadcasted_iota(jnp.int32, (CHUNK, CHUNK), 1)
    tri = jnp.where(col <= row, 1.0, 0.0).astype(BF16)
    head_shift = HEAD_DIM.bit_length() - 1
    head_masks = [jnp.where((lane >> head_shift) == h, 1.0, 0.0).astype(BF16)
                  for h in range(HEADS_PER_GROUP)]
    row2 = lax.broadcasted_iota(jnp.int32, (2 * CHUNK, LANES), 0)
    lane2 = lax.broadcasted_iota(jnp.int32, (2 * CHUNK, LANES), 1)
    causal2 = (lane2 & (HEAD_DIM - 1)) < (row2 & (CHUNK - 1)) + (row2 >> (CHUNK.bit_length() - 1))
    rowl = lax.broadcasted_iota(jnp.int32, (LANES, LANES), 0)
    lanel = lax.broadcasted_iota(jnp.int32, (LANES, LANES), 1)
    bd_mask = (rowl >> head_shift) == (lanel >> head_shift)
    seg_ones = jnp.where(bd_mask, 1.0, 0.0).astype(BF16)
    sls = [slice(g * LANES, (g + 1) * LANES) for g in groups]

    c0 = SUBLANES
    y_a, gate, new_conv, new_shift = [], [], [], []
    r, k, v, a, ld, cum = [], [], [], [], [], []
    for s in seqs:
        ch = p_ref[s, :, _CC:_CC + D_CONV] * p_ref[s, :, _CHH:_CHH + D_CONV]
        cbuf[s, c0:c0 + CHUNK, :] = ch
        conv = (cbuf[s, c0 - 2:c0 - 2 + CHUNK, :] * convw_ref[0:1, :]
                + cbuf[s, c0 - 1:c0 - 1 + CHUNK, :] * convw_ref[1:2, :]
                + ch * convw_ref[2:3, :])
        y_a.append(jax.nn.sigmoid(p_ref[s, :, _ZA:_ZA + D_MODEL])
                   * (p_ref[s, :, _CB:_CB + D_CONV] * conv))
        new_conv.append(cbuf[s, c0 + CHUNK - (CONV_W - 1):c0 + CHUNK, :])
        cbuf[s, 0:SUBLANES, :] = cbuf[s, CHUNK:CHUNK + SUBLANES, :]

        ps = p_ref[s, :, _PS:_PS + D_SHIFT]
        sbuf[s, c0:c0 + CHUNK, :] = ps
        prev = sbuf[s, c0 - 1:c0 - 1 + CHUNK, :]
        xm = ps + (prev - ps) * mu_ref[...]
        new_shift.append(sbuf[s, c0 + CHUNK - 1:c0 + CHUNK, :])
        sbuf[s, 0:SUBLANES, :] = sbuf[s, CHUNK:CHUNK + SUBLANES, :]

        wa_in = xm[:, _WA:_WA + LANES]
        wa_in = jnp.where(lane < DECAY_LORA, jnp.tanh(wa_in), wa_in)
        wa = _mm1(wa_in, wwa_ref[...])
        gate.append(_mm1(jax.nn.sigmoid(xm[:, _G:_G + GATE_LORA]), g2_ref[...]))
        ld_all = -jnp.exp(-0.5) * jax.nn.sigmoid(w0 + wa[:, :D_RWKV])
        a_all = jax.nn.sigmoid(a0 + wa[:, D_RWKV:])
        ld1 = ld_all.astype(BF16)
        ld2 = (ld_all - ld1.astype(F32)).astype(BF16)
        ld3 = (ld_all - ld1.astype(F32) - ld2.astype(F32)).astype(BF16)
        cum_all = _dot(tri, ld1) + (_dot(tri, ld2) + _dot(tri, ld3))
        for g in groups:
            r.append(xm[:, _R + g * LANES:_R + (g + 1) * LANES])
            k.append(xm[:, _K + g * LANES:_K + (g + 1) * LANES])
            v.append(xm[:, _V + g * LANES:_V + (g + 1) * LANES])
            a.append(a_all[:, sls[g]])
            ld.append(ld_all[:, sls[g]])
            cum.append(cum_all[:, sls[g]])

    n_pairs = len(pairs)
    kk = [k[i] * k_k[:, sls[g]] for i, (s, g) in enumerate(pairs)]
    kf = [k[i] * (1.0 + (a[i] - 1.0) * k_a[:, sls[g]]) for i, (s, g) in enumerate(pairs)]
    sums = _segment_sums([x * x for x in kk]
                         + [r[i] * kf[i] * r_k[:, sls[g]] for i, (s, g) in enumerate(pairs)],
                         seg_ones)
    kkn = [kk[i] / jnp.maximum(jnp.sqrt(sums[i]), L2_EPS) for i in range(n_pairs)]
    bonus = [sums[n_pairs + i] * v[i] for i in range(n_pairs)]
    y, s_new = _wkv_chunk(r, kf, v, kkn, a, ld, cum, [state[i] for i in range(n_pairs)],
                          head_masks, causal2, bd_mask)
    for i in range(n_pairs):
        state[i] = s_new[i]
    mean = [m * (1.0 / HEAD_DIM) for m in _segment_sums(y, seg_ones)]
    dev = [y[i] - mean[i] for i in range(n_pairs)]
    var = [m * (1.0 / HEAD_DIM) for m in _segment_sums([d * d for d in dev], seg_ones)]
    for i, (s, g) in enumerate(pairs):
        sl = sls[g]
        yn = dev[i] * lax.rsqrt(var[i] + GN_EPS) * lnx_w[:, sl] + lnx_b[:, sl]
        y_b = (yn + bonus[i]) * gate[s][:, sl]
        z_b = p_ref[s, :, _ZB + g * LANES:_ZB + (g + 1) * LANES]
        merged_ref[s, :, sl] = (y_a[s][:, sl] + jax.nn.sigmoid(z_b) * y_b).astype(merged_ref.dtype)

    @pl.when(t == n_chunks - 1)
    def _():
        for s in seqs:
            nconv_ref[s] = new_conv[s]
            nshift_ref[s] = new_shift[s]
            for h in range(N_HEADS):
                o = (h % HEADS_PER_GROUP) * HEAD_DIM
                nwkv_ref[s, h] = state[s * N_GROUPS + h // HEADS_PER_GROUP,
                                       o:o + HEAD_DIM, o:o + HEAD_DIM]


def _mixer(p2d, conv0, shift0, wkv0, convw, mu, vecs, wwa, g2, batch, seq):
    n_chunks = seq // CHUNK
    n_seq = SEQS_PER_STEP if batch % SEQS_PER_STEP == 0 else 1
    kern = functools.partial(_mixer_kernel, n_chunks=n_chunks, n_seq=n_seq)
    const = lambda b, t: (0, 0)
    per_b3 = lambda b, t: (b, 0, 0)
    per_b4 = lambda b, t: (b, 0, 0, 0)
    merged, conv, shift, wkv = pl.pallas_call(
        kern,
        out_shape=(jax.ShapeDtypeStruct((batch, seq, D_MODEL), BF16),
                   jax.ShapeDtypeStruct((batch, CONV_W - 1, D_CONV), F32),
                   jax.ShapeDtypeStruct((batch, 1, D_SHIFT), F32),
                   jax.ShapeDtypeStruct((batch, N_HEADS, HEAD_DIM, HEAD_DIM), F32)),
        grid=(batch // n_seq, n_chunks),
        in_specs=[pl.BlockSpec((n_seq, CHUNK, D_PROJ), lambda b, t: (b, t, 0)),
                  pl.BlockSpec((n_seq, CONV_W - 1, D_CONV), per_b3),
                  pl.BlockSpec((n_seq, 1, D_SHIFT), per_b3),
                  pl.BlockSpec((n_seq, N_HEADS, HEAD_DIM, HEAD_DIM), per_b4),
                  pl.BlockSpec((CONV_W, D_CONV), const),
                  pl.BlockSpec((1, D_SHIFT), const),
                  pl.BlockSpec((SUBLANES, D_RWKV), const),
                  pl.BlockSpec((LANES, 2 * D_RWKV), const),
                  pl.BlockSpec((GATE_LORA, D_RWKV), const)],
        out_specs=(pl.BlockSpec((n_seq, CHUNK, D_MODEL), lambda b, t: (b, t, 0)),
                   pl.BlockSpec((n_seq, CONV_W - 1, D_CONV), per_b3),
                   pl.BlockSpec((n_seq, 1, D_SHIFT), per_b3),
                   pl.BlockSpec((n_seq, N_HEADS, HEAD_DIM, HEAD_DIM), per_b4)),
        scratch_shapes=[pltpu.VMEM((n_seq, CHUNK + SUBLANES, D_CONV), F32),
                        pltpu.VMEM((n_seq, CHUNK + SUBLANES, D_SHIFT), F32),
                        pltpu.VMEM((n_seq * N_GROUPS, LANES, LANES), F32)],
        compiler_params=_params(("arbitrary", "arbitrary")),
        name="mixer",
    )(p2d.reshape(batch, seq, D_PROJ), conv0, shift0, wkv0, convw, mu, vecs, wwa, g2)
    return merged.reshape(batch * seq, D_MODEL), conv, shift, wkv


def _silu_mul(gate, up):
    return gate * jax.nn.sigmoid(gate) * up


def _ffn_dense_kernel(x_ref, m_ref, wo_ref, g_ref, wg_ref, wu_ref, wd_ref, o_ref, h_ref):
    @pl.when(pl.program_id(1) == 0)
    def _():
        x1 = x_ref[...] + _dot(m_ref[...], wo_ref[...])
        o_ref[...] = x1
        h_ref[...] = _rms(x1, g_ref[...]).astype(BF16)

    h = h_ref[...]
    act = _silu_mul(_dot(h, wg_ref[...]), _dot(h, wu_ref[...])).astype(BF16)
    o_ref[...] += _dot(act, wd_ref[...])


def _ffn_dense(x2d, merged, wo, g, wg, wu, wd, tm, tf):
    n, d = x2d.shape
    f = wg.shape[1]
    return pl.pallas_call(
        _ffn_dense_kernel,
        out_shape=jax.ShapeDtypeStruct((n, d), F32),
        grid=(n // tm, f // tf),
        in_specs=[pl.BlockSpec((tm, d), lambda i, j: (i, 0)),
                  pl.BlockSpec((tm, d), lambda i, j: (i, 0)),
                  pl.BlockSpec((d, d), lambda i, j: (0, 0)),
                  pl.BlockSpec((1, d), lambda i, j: (0, 0)),
                  pl.BlockSpec((d, tf), lambda i, j: (0, j)),
                  pl.BlockSpec((d, tf), lambda i, j: (0, j)),
                  pl.BlockSpec((tf, d), lambda i, j: (j, 0))],
        out_specs=pl.BlockSpec((tm, d), lambda i, j: (i, 0)),
        scratch_shapes=[pltpu.VMEM((tm, d), BF16)],
        compiler_params=_params(("arbitrary", "arbitrary")),
        name="ffn_dense",
    )(x2d, merged, wo, g, wg, wu, wd)


def _top2(logits):
    lane_i = lax.broadcasted_iota(jnp.int32, logits.shape, 1)
    lane = lane_i.astype(F32)
    neg = jnp.float32(-jnp.inf)
    lg = jnp.where(lane_i < N_EXPERTS, logits, neg)
    m1 = jnp.max(lg, axis=-1, keepdims=True)
    i1 = jnp.min(jnp.where(lg == m1, lane, float(LANES)), axis=-1, keepdims=True)
    lg2 = jnp.where(lane == i1, neg, lg)
    m2 = jnp.max(lg2, axis=-1, keepdims=True)
    i2 = jnp.min(jnp.where(lg2 == m2, lane, float(LANES)), axis=-1, keepdims=True)
    e2 = jnp.exp(m2 - m1)
    den = 1.0 + e2
    ids = jnp.where(lane_i == 0, i1, jnp.where(lane_i == 1, i2, 0.0)).astype(jnp.int32)
    wts = jnp.where(lane_i == 0, 1.0 / den, jnp.where(lane_i == 1, e2 / den, 0.0))
    return ids, wts


def _router_kernel(x_ref, m_ref, wo_ref, g_ref, rw_ref, x1_ref, ids_ref, wts_ref):
    x1 = x_ref[...] + _dot(m_ref[...], wo_ref[...])
    x1_ref[...] = x1
    ids_ref[...], wts_ref[...] = _top2(_mm3(_rms(x1, g_ref[...]), rw_ref[...]))


def _router(x2d, merged, wo, g, rw, tm):
    n, d = x2d.shape
    row = lambda i: (i, 0)
    const = lambda i: (0, 0)
    return pl.pallas_call(
        _router_kernel,
        out_shape=(jax.ShapeDtypeStruct((n, d), F32),
                   jax.ShapeDtypeStruct((n, LANES), jnp.int32),
                   jax.ShapeDtypeStruct((n, LANES), F32)),
        grid=(n // tm,),
        in_specs=[pl.BlockSpec((tm, d), row), pl.BlockSpec((tm, d), row),
                  pl.BlockSpec((d, d), const), pl.BlockSpec((1, d), const),
                  pl.BlockSpec((d, LANES), const)],
        out_specs=(pl.BlockSpec((tm, d), row), pl.BlockSpec((tm, LANES), row),
                   pl.BlockSpec((tm, LANES), row)),
        compiler_params=_params(("arbitrary",)),
        name="router",
    )(x2d, merged, wo, g, rw)


def _experts_kernel(te_ref, nv_ref, inv_prev_ref, inv_ref, inv_next_ref, x1_hbm, g_ref,
                    wg_ref, wu_ref, wd_ref, y_hbm, xbuf, obuf, gather_sem, scatter_sem,
                    *, tm, n_tok, f_split):
    t = pl.program_id(0)
    n_used = nv_ref[0]
    cur = t & 1
    nxt = 1 - cur

    def gather_row(idx_ref, buf, r):
        tok = jnp.maximum(idx_ref[0, 0, r], 0) >> 1
        pltpu.make_async_copy(x1_hbm.at[pl.ds(tok, 1)], xbuf.at[buf, pl.ds(r, 1)],
                              gather_sem.at[buf]).start()

    def gather_loop(idx_ref, buf):
        def body(r, c):
            gather_row(idx_ref, buf, r)
            return c
        lax.fori_loop(0, tm, body, 0, unroll=8)

    def wait_gather(buf):
        pltpu.make_async_copy(x1_hbm.at[pl.ds(0, tm)], xbuf.at[buf], gather_sem.at[buf]).wait()

    def scatter_row(idx_ref, buf, r, real):
        a = idx_ref[0, 0, r]
        row = jnp.where((a >= 0) & real, (a & 1) * n_tok + (a >> 1), 2 * n_tok + r)
        pltpu.make_async_copy(obuf.at[buf, pl.ds(r, 1)], y_hbm.at[pl.ds(row, 1)], scatter_sem).start()

    def wait_scatter(buf):
        pltpu.make_async_copy(obuf.at[buf], y_hbm.at[pl.ds(0, tm)], scatter_sem).wait()

    @pl.when(t == 0)
    def _():
        obuf[...] = jnp.zeros_like(obuf)
        gather_loop(inv_ref, 0)

    wait_gather(cur)

    @pl.when(t < n_used)
    def _():
        h = _rms(xbuf[cur], g_ref[...]).astype(BF16)
        fs = wg_ref.shape[2] // f_split
        rows_per = -(-tm // f_split)
        acc = None
        for i in range(f_split):
            act = _silu_mul(_dot(h, wg_ref[0, :, i * fs:(i + 1) * fs]),
                            _dot(h, wu_ref[0, :, i * fs:(i + 1) * fs])).astype(BF16)
            part = _dot(act, wd_ref[0, i * fs:(i + 1) * fs, :])
            acc = part if acc is None else acc + part
            for r in range(i * rows_per, min((i + 1) * rows_per, tm)):
                gather_row(inv_next_ref, nxt, r)
                scatter_row(inv_prev_ref, nxt, r, t > 0)
        wait_scatter(nxt)
        obuf[cur] = acc

        @pl.when(t == n_used - 1)
        def _():
            def body(r, c):
                scatter_row(inv_ref, cur, r, True)
                return c
            lax.fori_loop(0, tm, body, 0, unroll=8)
            wait_scatter(cur)

    @pl.when(t >= n_used)
    def _():
        gather_loop(inv_next_ref, nxt)

    @pl.when(t == pl.num_programs(0) - 1)
    def _():
        wait_gather(nxt)


def _experts(x1, g, wg, wu, wd, tile_expert, n_used, inv, tm):
    n, d = x1.shape
    n_tiles = tile_expert.shape[0]
    f = wg.shape[2]
    expert_w = lambda t, te, nv: (jnp.minimum(te[t], N_EXPERTS - 1), 0, 0)
    single = pl.Buffered(1)
    inv3 = inv.reshape(n_tiles, 1, tm)
    idx_spec = lambda shift: pl.BlockSpec(
        (1, 1, tm), lambda t, te, nv: (jnp.clip(t + shift, 0, n_tiles - 1), 0, 0),
        memory_space=pltpu.SMEM)
    grid_spec = pltpu.PrefetchScalarGridSpec(
        num_scalar_prefetch=2,
        grid=(n_tiles,),
        in_specs=[idx_spec(-1), idx_spec(0), idx_spec(1),
                  pl.BlockSpec(memory_space=pl.ANY),
                  pl.BlockSpec((1, d), lambda t, te, nv: (0, 0)),
                  pl.BlockSpec((1, d, f), expert_w, pipeline_mode=single),
                  pl.BlockSpec((1, d, f), expert_w, pipeline_mode=single),
                  pl.BlockSpec((1, f, d), expert_w, pipeline_mode=single)],
        out_specs=pl.BlockSpec(memory_space=pl.ANY),
        scratch_shapes=[pltpu.VMEM((2, tm, d), F32), pltpu.VMEM((2, tm, d), F32),
                        pltpu.SemaphoreType.DMA((2,)), pltpu.SemaphoreType.DMA(())])
    return pl.pallas_call(
        functools.partial(_experts_kernel, tm=tm, n_tok=n, f_split=f // (2 * LANES)),
        out_shape=jax.ShapeDtypeStruct((2 * n + tm, d), F32),
        grid_spec=grid_spec,
        compiler_params=_params(("arbitrary",)),
        name="experts",
    )(tile_expert, n_used, inv3, inv3, inv3, x1, g, wg, wu, wd)


def _combine_kernel(x1_ref, y0_ref, y1_ref, wts_ref, fin_ref, o_ref):
    w = wts_ref[...]
    moe = w[:, 0:1] * y0_ref[...] + w[:, 1:2] * y1_ref[...]
    o_ref[...] = _rms(x1_ref[...] + moe, fin_ref[...])


def _combine(x1, y2, wts, fin, tm):
    n, d = x1.shape
    n_blocks = n // tm
    return pl.pallas_call(
        _combine_kernel,
        out_shape=jax.ShapeDtypeStruct((n, d), F32),
        grid=(n_blocks,),
        in_specs=[pl.BlockSpec((tm, d), lambda i: (i, 0)),
                  pl.BlockSpec((tm, d), lambda i: (i, 0)),
                  pl.BlockSpec((tm, d), lambda i: (i + n_blocks, 0)),
                  pl.BlockSpec((tm, LANES), lambda i: (i, 0)),
                  pl.BlockSpec((1, d), lambda i: (0, 0))],
        out_specs=pl.BlockSpec((tm, d), lambda i: (i, 0)),
        compiler_params=_params(("arbitrary",)),
        name="combine",
    )(x1, y2, y2, wts, fin)


def _routing_plan(ids, tm):
    n = ids.shape[0]
    e_flat = ids[:, :2].reshape(-1)
    onehot = (e_flat[:, None] == jnp.arange(N_EXPERTS, dtype=jnp.int32)[None, :]).astype(jnp.int32)
    csum = jnp.cumsum(onehot, axis=0)
    counts = csum[-1]
    padded = (counts + tm - 1) // tm * tm
    ends = jnp.cumsum(padded)
    pos = jnp.sum(onehot * (csum - 1 + (ends - padded)[None, :]), axis=1)
    total = 2 * n + N_EXPERTS * tm
    inv = jnp.full((total,), -1, jnp.int32).at[pos].set(jnp.arange(2 * n, dtype=jnp.int32))
    tile_start = jnp.arange(total // tm, dtype=jnp.int32) * tm
    tile_expert = jnp.sum((tile_start[:, None] >= ends[None, :]).astype(jnp.int32), axis=1)
    return tile_expert, (ends[-1:] // tm).astype(jnp.int32), inv


def _ffn_moe(x2d, merged, wo, g, rw, wg, wu, wd, fin, tm, tm_e):
    x1, ids, wts = _router(x2d, merged, wo, g, rw, tm)
    tile_expert, n_used, inv = _routing_plan(ids, tm_e)
    y2 = _experts(x1, g, wg, wu, wd, tile_expert, n_used, inv, tm_e)
    return _combine(x1, y2, wts, fin, tm)


def _prep_layer_weights(l, w_in, conv_w, shift_mu, decay_w0, decay_w2, aaa_a0, aaa_a2, gate_g2,
                        key_k, key_a, bonus_r_k, lnx_w, lnx_b, w_out):
    zero = jnp.zeros((DECAY_LORA, D_RWKV), F32)
    wwa = jnp.concatenate([jnp.concatenate([decay_w2[l], zero], axis=1),
                           jnp.concatenate([zero, aaa_a2[l]], axis=1)], axis=0).astype(BF16)
    vecs = jnp.stack([decay_w0[l], aaa_a0[l], key_k[l], key_a[l], bonus_r_k[l], lnx_w[l], lnx_b[l],
                      jnp.zeros((D_RWKV,), F32)])
    return dict(w_in=w_in[l].astype(BF16), convw=conv_w[l], mu=shift_mu[l][None], vecs=vecs,
                wwa=wwa, g2=gate_g2[l].astype(BF16), w_out=w_out[l].astype(BF16))


def _row_tile(n, want):
    return want if n % want == 0 else n


def _tiles(n):
    big_experts = 2 * n >= 4 * 512 * N_EXPERTS
    return _row_tile(n, 512), _row_tile(n, 1024), _row_tile(2 * n, 512 if big_experts else 256)


def _trunk(x, conv_st, shift_st, wkv_st, layers, ffn_norm, dense, moe, final_norm):
    batch, seq, d = x.shape
    n = batch * seq
    x2d = x.reshape(n, d)
    tm, tm_proj, tm_expert = _tiles(n)
    convs, shifts, wkvs = [], [], []
    for l, lw in enumerate(layers):
        p = _norm_proj(x2d, lw['norm'], lw['w_in'], tm_proj, D_PROJ // 6)
        merged, c, s, w = _mixer(p, conv_st[l], shift_st[l][:, None], wkv_st[l],
                                 lw['convw'], lw['mu'], lw['vecs'], lw['wwa'], lw['g2'],
                                 batch, seq)
        convs.append(c)
        shifts.append(s[:, 0])
        wkvs.append(w)
        if l % 2 == 0:
            wg, wu, wd = dense
            x2d = _ffn_dense(x2d, merged, lw['w_out'], ffn_norm[l][None], wg, wu, wd, tm,
                             wg.shape[1] // 2)
        else:
            rw, wg, wu, wd = moe
            x2d = _ffn_moe(x2d, merged, lw['w_out'], ffn_norm[l][None], rw, wg, wu, wd,
                           final_norm[None], tm, tm_expert)
    return x2d.reshape(batch, seq, d), jnp.stack(convs), jnp.stack(shifts), jnp.stack(wkvs)


def kernel(x_prompt, x_sample, state_conv, state_shift, state_wkv, mix_norm, w_in, conv_w, shift_mu,
           decay_w0, decay_w2, aaa_a0, aaa_a2, gate_g2, key_k, key_a, bonus_r_k, lnx_w, lnx_b, w_out,
           ffn_norm, ffn_w_gate, ffn_w_up, ffn_w_down, router_w, moe_w_gate, moe_w_up, moe_w_down,
           final_norm):
    depth = w_in.shape[0]
    assert depth == 2 and ffn_w_gate.shape[0] == 1 and moe_w_gate.shape[0] == 1
    layers = []
    for l in range(depth):
        lw = _prep_layer_weights(l, w_in, conv_w, shift_mu, decay_w0, decay_w2, aaa_a0, aaa_a2,
                                 gate_g2, key_k, key_a, bonus_r_k, lnx_w, lnx_b, w_out)
        lw['norm'] = mix_norm[l][None]
        layers.append(lw)
    dense = (ffn_w_gate[0].astype(BF16), ffn_w_up[0].astype(BF16), ffn_w_down[0].astype(BF16))
    rw = jnp.pad(router_w[0], ((0, 0), (0, LANES - N_EXPERTS)))
    moe = (rw, moe_w_gate[0].astype(BF16), moe_w_up[0].astype(BF16), moe_w_down[0].astype(BF16))

    b = x_prompt.shape[0]
    zero_conv = jnp.zeros((depth, b) + state_conv.shape[2:], state_conv.dtype)
    zero_shift = jnp.zeros((depth, b) + state_shift.shape[2:], state_shift.dtype)
    zero_wkv = jnp.zeros((depth, b) + state_wkv.shape[2:], state_wkv.dtype)
    run = functools.partial(_trunk, layers=layers, ffn_norm=ffn_norm, dense=dense, moe=moe,
                            final_norm=final_norm)
    y_s, conv_s, shift_s, wkv_s = run(x_sample, state_conv, state_shift, state_wkv)
    y_p, conv_p, shift_p, wkv_p = run(x_prompt, zero_conv, zero_shift, zero_wkv)
    return (y_p, y_s, conv_p, shift_p, wkv_p, conv_s, shift_s, wkv_s)
```

```python
import functools

import jax
import jax.numpy as jnp
from jax import lax
from jax.experimental import pallas as pl
from jax.experimental.pallas import tpu as pltpu

F32 = jnp.float32
BF16 = jnp.bfloat16

D_MODEL = 1024
N_HEADS = 16
HEAD_DIM = 64
D_RWKV = N_HEADS * HEAD_DIM
D_CONV = 1024
CONV_W = 3
DECAY_LORA = 64
AAA_LORA = 64
GATE_LORA = 128
D_SHIFT = 3 * D_RWKV + DECAY_LORA + AAA_LORA + GATE_LORA
D_PROJ = 2 * D_MODEL + 3 * D_CONV + D_SHIFT
N_EXPERTS = 8
RMS_EPS = 1e-5
GN_EPS = 64e-5
L2_EPS = 1e-12

LANES = 128
SUBLANES = 8
CHUNK = 64
HEADS_PER_GROUP = LANES // HEAD_DIM
SEQS_PER_STEP = 2
N_GROUPS = N_HEADS // HEADS_PER_GROUP
VMEM_LIMIT = 56 * 1024 * 1024

_ZA, _ZB, _CB, _CC, _CHH, _PS = (0, D_MODEL, 2 * D_MODEL, 2 * D_MODEL + D_CONV,
                                 2 * D_MODEL + 2 * D_CONV, 2 * D_MODEL + 3 * D_CONV)
_R, _K, _V, _WA, _G = 0, D_RWKV, 2 * D_RWKV, 3 * D_RWKV, 3 * D_RWKV + DECAY_LORA + AAA_LORA


def _params(semantics):
    return pltpu.CompilerParams(dimension_semantics=semantics, vmem_limit_bytes=VMEM_LIMIT)


def _rms(x, g):
    ms = jnp.mean(x * x, axis=-1, keepdims=True)
    return x * lax.rsqrt(ms + RMS_EPS) * g


def _split(x):
    hi = x.astype(BF16)
    lo = (x - hi.astype(F32)).astype(BF16)
    return hi, lo


_NN = (((1,), (0,)), ((), ()))
_NT = (((1,), (1,)), ((), ()))


def _dot(a, b, dims=_NN):
    return lax.dot_general(a, b, dims, preferred_element_type=F32)


def _mm1(a, b, dims=_NN):
    return _dot(a.astype(BF16), b.astype(BF16), dims)


def _mm3(a, b, dims=_NN):
    a1, a2 = _split(a)
    b1, b2 = _split(b)
    return _dot(a1, b1, dims) + (_dot(a1, b2, dims) + _dot(a2, b1, dims))


def _segment_sums(xs, seg_ones):
    rows = xs[0].shape[0]
    parts = [half for x in xs for half in _split(x)]
    out = _dot(jnp.concatenate(parts, axis=0), seg_ones)
    return [out[2 * i * rows:(2 * i + 1) * rows] + out[(2 * i + 1) * rows:(2 * i + 2) * rows]
            for i in range(len(xs))]


def _norm_proj_kernel(x_ref, g_ref, w_ref, o_ref, *, n_col):
    xn = _rms(x_ref[...], g_ref[...]).astype(BF16)
    tn = w_ref.shape[1] // n_col
    for j in range(n_col):
        o_ref[:, j * tn:(j + 1) * tn] = _dot(xn, w_ref[:, j * tn:(j + 1) * tn])


def _norm_proj(x2d, g, w_bf16, tm, n_col):
    n, d = x2d.shape
    dp = w_bf16.shape[1]
    return pl.pallas_call(
        functools.partial(_norm_proj_kernel, n_col=n_col),
        out_shape=jax.ShapeDtypeStruct((n, dp), F32),
        grid=(n // tm,),
        in_specs=[pl.BlockSpec((tm, d), lambda i: (i, 0)),
                  pl.BlockSpec((1, d), lambda i: (0, 0)),
                  pl.BlockSpec((d, dp), lambda i: (0, 0), pipeline_mode=pl.Buffered(1))],
        out_specs=pl.BlockSpec((tm, dp), lambda i: (i, 0)),
        compiler_params=_params(("arbitrary",)),
        name="norm_proj",
    )(x2d, g, w_bf16)


def _block_diag(x, head_masks):
    x = x.astype(BF16)
    return jnp.concatenate([x * m for m in head_masks], axis=0)


def _wkv_chunk(r, k, v, kkn, a, ld, cum, s_prev, head_masks, causal2, bd_mask):
    groups = range(len(r))
    bd = functools.partial(_block_diag, head_masks=head_masks)
    n_steps = CHUNK.bit_length() - 1
    g_b, g_k, g_s, bd_v, b_h, k_h, cend = [], [], [], [], [], [], []
    for g in groups:
        ce = cum[g][CHUNK - 1:CHUNK, :]
        w_prev = jnp.exp(cum[g] - ld[g])
        w_t = jnp.exp(cum[g])
        w_inv = jnp.exp(-cum[g])
        w_rest = jnp.exp(ce - cum[g])
        b = kkn[g] * a[g]
        lhs2 = jnp.concatenate([-kkn[g] * w_prev, r[g] * w_t], axis=0)
        rhs = jnp.concatenate([bd(b * w_inv), bd(k[g] * w_inv), s_prev[g].astype(BF16)], axis=0)
        gram = _mm1(lhs2, rhs, _NT)
        g_b.append(jnp.where(causal2, gram[:, :LANES], 0.0))
        g_k.append(jnp.where(causal2, gram[:, LANES:2 * LANES], 0.0))
        g_s.append(gram[:, 2 * LANES:])
        bd_v.append(bd(v[g]))
        b_h.append(b * w_rest)
        k_h.append(k[g] * w_rest)
        cend.append(ce)

    u = [g_s[g][:CHUNK] + _mm1(g_k[g][:CHUNK], bd_v[g]) for g in groups]
    pw = [g_b[g][:CHUNK] for g in groups]
    for i in range(n_steps):
        for g in groups:
            if i + 1 < n_steps:
                res = _mm1(pw[g], jnp.concatenate([bd(u[g]), bd(pw[g])], axis=1))
                u[g] = u[g] + res[:, :LANES]
                pw[g] = res[:, LANES:]
            else:
                u[g] = u[g] + _mm1(pw[g], bd(u[g]))

    y, s_new = [], []
    for g in groups:
        l_r = jnp.concatenate([g_b[g][CHUNK:], g_k[g][CHUNK:]], axis=1)
        y.append(g_s[g][CHUNK:] + _mm1(l_r, jnp.concatenate([bd(u[g]), bd_v[g]], axis=0)))
    for g in groups:
        uv_t = jnp.concatenate([u[g], v[g]], axis=0).T
        upd = _mm1(uv_t, jnp.concatenate([b_h[g], k_h[g]], axis=0))
        s_new.append(s_prev[g] * jnp.exp(cend[g]) + jnp.where(bd_mask, upd, 0.0))
    return y, s_new


def _mixer_kernel(p_ref, conv0_ref, shift0_ref, wkv0_ref, convw_ref, mu_ref, vec_ref, wwa_ref,
                  g2_ref, merged_ref, nconv_ref, nshift_ref, nwkv_ref, cbuf, sbuf, state,
                  *, n_chunks, n_seq):
    t = pl.program_id(1)
    seqs = range(n_seq)
    groups = range(N_GROUPS)
    pairs = [(s, g) for s in seqs for g in groups]

    @pl.when(t == 0)
    def _():
        zero = jnp.zeros((HEAD_DIM, HEAD_DIM), F32)
        for s in seqs:
            cbuf[s, 0:SUBLANES, :] = jnp.zeros((SUBLANES, D_CONV), F32)
            cbuf[s, SUBLANES - (CONV_W - 1):SUBLANES, :] = conv0_ref[s]
            sbuf[s, 0:SUBLANES, :] = jnp.zeros((SUBLANES, D_SHIFT), F32)
            sbuf[s, SUBLANES - 1:SUBLANES, :] = shift0_ref[s]
            for g in groups:
                h0 = g * HEADS_PER_GROUP
                state[s * N_GROUPS + g] = jnp.concatenate(
                    [jnp.concatenate([wkv0_ref[s, h0 + h] if hh == h else zero
                                      for hh in range(HEADS_PER_GROUP)], axis=1)
                     for h in range(HEADS_PER_GROUP)], axis=0)

    w0, a0, k_k, k_a, r_k, lnx_w, lnx_b = (vec_ref[i:i + 1, :] for i in range(7))
    lane = lax.broadcasted_iota(jnp.int32, (CHUNK, LANES), 1)
    row = lax.broadcasted_iota(jnp.int32, (CHUNK, CHUNK), 0)
    col = lax.broadcasted_iota(jnp.int32, (CHUNK, CHUNK), 1)
    tri = jnp.where(col <= row, 1.0, 0.0).astype(BF16)
    head_shift = HEAD_DIM.bit_length() - 1
    head_masks = [jnp.where((lane >> head_shift) == h, 1.0, 0.0).astype(BF16)
                  for h in range(HEADS_PER_GROUP)]
    row2 = lax.broadcasted_iota(jnp.int32, (2 * CHUNK, LANES), 0)
    lane2 = lax.broadcasted_iota(jnp.int32, (2 * CHUNK, LANES), 1)
    causal2 = (lane2 & (HEAD_DIM - 1)) < (row2 & (CHUNK - 1)) + (row2 >> (CHUNK.bit_length() - 1))
    rowl = lax.broadcasted_iota(jnp.int32, (LANES, LANES), 0)
    lanel = lax.broadcasted_iota(jnp.int32, (LANES, LANES), 1)
    bd_mask = (rowl >> head_shift) == (lanel >> head_shift)
    seg_ones = jnp.where(bd_mask, 1.0, 0.0).astype(BF16)
    sls = [slice(g * LANES, (g + 1) * LANES) for g in groups]

    c0 = SUBLANES
    y_a, gate, new_conv, new_shift = [], [], [], []
    r, k, v, a, ld, cum = [], [], [], [], [], []
    for s in seqs:
        ch = p_ref[s, :, _CC:_CC + D_CONV] * p_ref[s, :, _CHH:_CHH + D_CONV]
        cbuf[s, c0:c0 + CHUNK, :] = ch
        conv = (cbuf[s, c0 - 2:c0 - 2 + CHUNK, :] * convw_ref[0:1, :]
                + cbuf[s, c0 - 1:c0 - 1 + CHUNK, :] * convw_ref[1:2, :]
                + ch * convw_ref[2:3, :])
        y_a.append(jax.nn.sigmoid(p_ref[s, :, _ZA:_ZA + D_MODEL])
                   * (p_ref[s, :, _CB:_CB + D_CONV] * conv))
        new_conv.append(cbuf[s, c0 + CHUNK - (CONV_W - 1):c0 + CHUNK, :])
        cbuf[s, 0:SUBLANES, :] = cbuf[s, CHUNK:CHUNK + SUBLANES, :]

        ps = p_ref[s, :, _PS:_PS + D_SHIFT]
        sbuf[s, c0:c0 + CHUNK, :] = ps
        prev = sbuf[s, c0 - 1:c0 - 1 + CHUNK, :]
        xm = ps + (prev - ps) * mu_ref[...]
        new_shift.append(sbuf[s, c0 + CHUNK - 1:c0 + CHUNK, :])
        sbuf[s, 0:SUBLANES, :] = sbuf[s, CHUNK:CHUNK + SUBLANES, :]

        wa_in = xm[:, _WA:_WA + LANES]
        wa_in = jnp.where(lane < DECAY_LORA, jnp.tanh(wa_in), wa_in)
        wa = _mm1(wa_in, wwa_ref[...])
        gate.append(_mm1(jax.nn.sigmoid(xm[:, _G:_G + GATE_LORA]), g2_ref[...]))
        ld_all = -jnp.exp(-0.5) * jax.nn.sigmoid(w0 + wa[:, :D_RWKV])
        a_all = jax.nn.sigmoid(a0 + wa[:, D_RWKV:])
        ld1 = ld_all.astype(BF16)
        ld2 = (ld_all - ld1.astype(F32)).astype(BF16)
        ld3 = (ld_all - ld1.astype(F32) - ld2.astype(F32)).astype(BF16)
        cum_all = _dot(tri, ld1) + (_dot(tri, ld2) + _dot(tri, ld3))
        for g in groups:
            r.append(xm[:, _R + g * LANES:_R + (g + 1) * LANES])
            k.append(xm[:, _K + g * LANES:_K + (g + 1) * LANES])
            v.append(xm[:, _V + g * LANES:_V + (g + 1) * LANES])
            a.append(a_all[:, sls[g]])
            ld.append(ld_all[:, sls[g]])
            cum.append(cum_all[:, sls[g]])

    n_pairs = len(pairs)
    kk = [k[i] * k_k[:, sls[g]] for i, (s, g) in enumerate(pairs)]
    kf = [k[i] * (1.0 + (a[i] - 1.0) * k_a[:, sls[g]]) for i, (s, g) in enumerate(pairs)]
    sums = _segment_sums([x * x for x in kk]
                         + [r[i] * kf[i] * r_k[:, sls[g]] for i, (s, g) in enumerate(pairs)],
                         seg_ones)
    kkn = [kk[i] / jnp.maximum(jnp.sqrt(sums[i]), L2_EPS) for i in range(n_pairs)]
    bonus = [sums[n_pairs + i] * v[i] for i in range(n_pairs)]
    y, s_new = _wkv_chunk(r, kf, v, kkn, a, ld, cum, [state[i] for i in range(n_pairs)],
                          head_masks, causal2, bd_mask)
    for i in range(n_pairs):
        state[i] = s_new[i]
    mean = [m * (1.0 / HEAD_DIM) for m in _segment_sums(y, seg_ones)]
    dev = [y[i] - mean[i] for i in range(n_pairs)]
    var = [m * (1.0 / HEAD_DIM) for m in _segment_sums([d * d for d in dev], seg_ones)]
    for i, (s, g) in enumerate(pairs):
        sl = sls[g]
        yn = dev[i] * lax.rsqrt(var[i] + GN_EPS) * lnx_w[:, sl] + lnx_b[:, sl]
        y_b = (yn + bonus[i]) * gate[s][:, sl]
        z_b = p_ref[s, :, _ZB + g * LANES:_ZB + (g + 1) * LANES]
        merged_ref[s, :, sl] = (y_a[s][:, sl] + jax.nn.sigmoid(z_b) * y_b).astype(merged_ref.dtype)

    @pl.when(t == n_chunks - 1)
    def _():
        for s in seqs:
            nconv_ref[s] = new_conv[s]
            nshift_ref[s] = new_shift[s]
            for h in range(N_HEADS):
                o = (h % HEADS_PER_GROUP) * HEAD_DIM
                nwkv_ref[s, h] = state[s * N_GROUPS + h // HEADS_PER_GROUP,
                                       o:o + HEAD_DIM, o:o + HEAD_DIM]


def _mixer(p2d, conv0, shift0, wkv0, convw, mu, vecs, wwa, g2, batch, seq):
    n_chunks = seq // CHUNK
    n_seq = SEQS_PER_STEP if batch % SEQS_PER_STEP == 0 else 1
    kern = functools.partial(_mixer_kernel, n_chunks=n_chunks, n_seq=n_seq)
    const = lambda b, t: (0, 0)
    per_b3 = lambda b, t: (b, 0, 0)
    per_b4 = lambda b, t: (b, 0, 0, 0)
    merged, conv, shift, wkv = pl.pallas_call(
        kern,
        out_shape=(jax.ShapeDtypeStruct((batch, seq, D_MODEL), BF16),
                   jax.ShapeDtypeStruct((batch, CONV_W - 1, D_CONV), F32),
                   jax.ShapeDtypeStruct((batch, 1, D_SHIFT), F32),
                   jax.ShapeDtypeStruct((batch, N_HEADS, HEAD_DIM, HEAD_DIM), F32)),
        grid=(batch // n_seq, n_chunks),
        in_specs=[pl.BlockSpec((n_seq, CHUNK, D_PROJ), lambda b, t: (b, t, 0)),
                  pl.BlockSpec((n_seq, CONV_W - 1, D_CONV), per_b3),
                  pl.BlockSpec((n_seq, 1, D_SHIFT), per_b3),
                  pl.BlockSpec((n_seq, N_HEADS, HEAD_DIM, HEAD_DIM), per_b4),
                  pl.BlockSpec((CONV_W, D_CONV), const),
                  pl.BlockSpec((1, D_SHIFT), const),
                  pl.BlockSpec((SUBLANES, D_RWKV), const),
                  pl.BlockSpec((LANES, 2 * D_RWKV), const),
                  pl.BlockSpec((GATE_LORA, D_RWKV), const)],
        out_specs=(pl.BlockSpec((n_seq, CHUNK, D_MODEL), lambda b, t: (b, t, 0)),
                   pl.BlockSpec((n_seq, CONV_W - 1, D_CONV), per_b3),
                   pl.BlockSpec((n_seq, 1, D_SHIFT), per_b3),
                   pl.BlockSpec((n_seq, N_HEADS, HEAD_DIM, HEAD_DIM), per_b4)),
        scratch_shapes=[pltpu.VMEM((n_seq, CHUNK + SUBLANES, D_CONV), F32),
                        pltpu.VMEM((n_seq, CHUNK + SUBLANES, D_SHIFT), F32),
                        pltpu.VMEM((n_seq * N_GROUPS, LANES, LANES), F32)],
        compiler_params=_params(("arbitrary", "arbitrary")),
        name="mixer",
    )(p2d.reshape(batch, seq, D_PROJ), conv0, shift0, wkv0, convw, mu, vecs, wwa, g2)
    return merged.reshape(batch * seq, D_MODEL), conv, shift, wkv


def _silu_mul(gate, up):
    return gate * jax.nn.sigmoid(gate) * up


def _ffn_dense_kernel(x_ref, m_ref, wo_ref, g_ref, wg_ref, wu_ref, wd_ref, o_ref, *, n_f):
    x1 = x_ref[...] + _dot(m_ref[...], wo_ref[...])
    h = _rms(x1, g_ref[...]).astype(BF16)
    tf = wg_ref.shape[1] // n_f
    acc = x1
    for j in range(n_f):
        act = _silu_mul(_dot(h, wg_ref[:, j * tf:(j + 1) * tf]),
                        _dot(h, wu_ref[:, j * tf:(j + 1) * tf])).astype(BF16)
        acc = acc + _dot(act, wd_ref[j * tf:(j + 1) * tf, :])
    o_ref[...] = acc


def _ffn_dense(x2d, merged, wo, g, wg, wu, wd, tm, n_f):
    n, d = x2d.shape
    f = wg.shape[1]
    row = lambda i: (i, 0)
    resident = lambda shape: pl.BlockSpec(shape, lambda i: (0, 0), pipeline_mode=pl.Buffered(1))
    return pl.pallas_call(
        functools.partial(_ffn_dense_kernel, n_f=n_f),
        out_shape=jax.ShapeDtypeStruct((n, d), F32),
        grid=(n // tm,),
        in_specs=[pl.BlockSpec((tm, d), row), pl.BlockSpec((tm, d), row),
                  resident((d, d)), pl.BlockSpec((1, d), lambda i: (0, 0)),
                  resident((d, f)), resident((d, f)), resident((f, d))],
        out_specs=pl.BlockSpec((tm, d), row),
        compiler_params=_params(("arbitrary",)),
        name="ffn_dense",
    )(x2d, merged, wo, g, wg, wu, wd)


def _top2(logits):
    lane_i = lax.broadcasted_iota(jnp.int32, logits.shape, 1)
    lane = lane_i.astype(F32)
    neg = jnp.float32(-jnp.inf)
    lg = jnp.where(lane_i < N_EXPERTS, logits, neg)
    m1 = jnp.max(lg, axis=-1, keepdims=True)
    i1 = jnp.min(jnp.where(lg == m1, lane, float(LANES)), axis=-1, keepdims=True)
    lg2 = jnp.where(lane == i1, neg, lg)
    m2 = jnp.max(lg2, axis=-1, keepdims=True)
    i2 = jnp.min(jnp.where(lg2 == m2, lane, float(LANES)), axis=-1, keepdims=True)
    e2 = jnp.exp(m2 - m1)
    den = 1.0 + e2
    ids = jnp.where(lane_i == 0, i1, jnp.where(lane_i == 1, i2, 0.0)).astype(jnp.int32)
    wts = jnp.where(lane_i == 0, 1.0 / den, jnp.where(lane_i == 1, e2 / den, 0.0))
    return ids, wts


def _router_kernel(x_ref, m_ref, wo_ref, g_ref, rw_ref, x1_ref, ids_ref, wts_ref):
    x1 = x_ref[...] + _dot(m_ref[...], wo_ref[...])
    x1_ref[...] = x1
    ids_ref[...], wts_ref[...] = _top2(_mm3(_rms(x1, g_ref[...]), rw_ref[...]))


def _router(x2d, merged, wo, g, rw, tm):
    n, d = x2d.shape
    row = lambda i: (i, 0)
    const = lambda i: (0, 0)
    return pl.pallas_call(
        _router_kernel,
        out_shape=(jax.ShapeDtypeStruct((n, d), F32),
                   jax.ShapeDtypeStruct((n, LANES), jnp.int32),
                   jax.ShapeDtypeStruct((n, LANES), F32)),
        grid=(n // tm,),
        in_specs=[pl.BlockSpec((tm, d), row), pl.BlockSpec((tm, d), row),
                  pl.BlockSpec((d, d), const), pl.BlockSpec((1, d), const),
                  pl.BlockSpec((d, LANES), const)],
        out_specs=(pl.BlockSpec((tm, d), row), pl.BlockSpec((tm, LANES), row),
                   pl.BlockSpec((tm, LANES), row)),
        compiler_params=_params(("arbitrary",)),
        name="router",
    )(x2d, merged, wo, g, rw)


def _experts_kernel(te_ref, nv_ref, inv_prev_ref, inv_ref, inv_next_ref, x1_hbm, g_ref,
                    wg_ref, wu_ref, wd_ref, y_hbm, xbuf, obuf, gather_sem, scatter_sem,
                    *, tm, n_tok, f_split):
    t = pl.program_id(0)
    n_used = nv_ref[0]
    cur = t & 1
    nxt = 1 - cur

    def gather_row(idx_ref, buf, r):
        tok = jnp.maximum(idx_ref[0, 0, r], 0) >> 1
        pltpu.make_async_copy(x1_hbm.at[pl.ds(tok, 1)], xbuf.at[buf, pl.ds(r, 1)],
                              gather_sem.at[buf]).start()

    def gather_loop(idx_ref, buf):
        def body(r, c):
            gather_row(idx_ref, buf, r)
            return c
        lax.fori_loop(0, tm, body, 0, unroll=8)

    def wait_gather(buf):
        pltpu.make_async_copy(x1_hbm.at[pl.ds(0, tm)], xbuf.at[buf], gather_sem.at[buf]).wait()

    def scatter_row(idx_ref, buf, r, real):
        a = idx_ref[0, 0, r]
        row = jnp.where((a >= 0) & real, (a & 1) * n_tok + (a >> 1), 2 * n_tok + r)
        pltpu.make_async_copy(obuf.at[buf, pl.ds(r, 1)], y_hbm.at[pl.ds(row, 1)], scatter_sem).start()

    def wait_scatter(buf):
        pltpu.make_async_copy(obuf.at[buf], y_hbm.at[pl.ds(0, tm)], scatter_sem).wait()

    @pl.when(t == 0)
    def _():
        obuf[...] = jnp.zeros_like(obuf)
        gather_loop(inv_ref, 0)

    wait_gather(cur)

    @pl.when(t < n_used)
    def _():
        h = _rms(xbuf[cur], g_ref[...]).astype(BF16)
        fs = wg_ref.shape[2] // f_split
        rows_per = -(-tm // max(1, (2 * f_split) // 3))
        acc = None
        for i in range(f_split):
            act = _silu_mul(_dot(h, wg_ref[0, :, i * fs:(i + 1) * fs]),
                            _dot(h, wu_ref[0, :, i * fs:(i + 1) * fs])).astype(BF16)
            part = _dot(act, wd_ref[0, i * fs:(i + 1) * fs, :])
            acc = part if acc is None else acc + part
            for r in range(i * rows_per, min((i + 1) * rows_per, tm)):
                gather_row(inv_next_ref, nxt, r)
                scatter_row(inv_prev_ref, nxt, r, t > 0)
        wait_scatter(nxt)
        obuf[cur] = acc

        @pl.when(t == n_used - 1)
        def _():
            def body(r, c):
                scatter_row(inv_ref, cur, r, True)
                return c
            lax.fori_loop(0, tm, body, 0, unroll=8)
            wait_scatter(cur)

    @pl.when(t >= n_used)
    def _():
        gather_loop(inv_next_ref, nxt)

    @pl.when(t == pl.num_programs(0) - 1)
    def _():
        wait_gather(nxt)


def _experts(x1, g, wg, wu, wd, tile_expert, n_used, inv, tm):
    n, d = x1.shape
    n_tiles = tile_expert.shape[0]
    f = wg.shape[2]
    expert_w = lambda t, te, nv: (jnp.minimum(te[t], N_EXPERTS - 1), 0, 0)
    single = pl.Buffered(1)
    inv3 = inv.reshape(n_tiles, 1, tm)
    idx_spec = lambda shift: pl.BlockSpec(
        (1, 1, tm), lambda t, te, nv: (jnp.clip(t + shift, 0, n_tiles - 1), 0, 0),
        memory_space=pltpu.SMEM)
    grid_spec = pltpu.PrefetchScalarGridSpec(
        num_scalar_prefetch=2,
        grid=(n_tiles,),
        in_specs=[idx_spec(-1), idx_spec(0), idx_spec(1),
                  pl.BlockSpec(memory_space=pl.ANY),
                  pl.BlockSpec((1, d), lambda t, te, nv: (0, 0)),
                  pl.BlockSpec((1, d, f), expert_w, pipeline_mode=single),
                  pl.BlockSpec((1, d, f), expert_w, pipeline_mode=single),
                  pl.BlockSpec((1, f, d), expert_w, pipeline_mode=single)],
        out_specs=pl.BlockSpec(memory_space=pl.ANY),
        scratch_shapes=[pltpu.VMEM((2, tm, d), F32), pltpu.VMEM((2, tm, d), F32),
                        pltpu.SemaphoreType.DMA((2,)), pltpu.SemaphoreType.DMA(())])
    return pl.pallas_call(
        functools.partial(_experts_kernel, tm=tm, n_tok=n, f_split=f // (2 * LANES)),
        out_shape=jax.ShapeDtypeStruct((2 * n + tm, d), F32),
        grid_spec=grid_spec,
        compiler_params=_params(("arbitrary",)),
        name="experts",
    )(tile_expert, n_used, inv3, inv3, inv3, x1, g, wg, wu, wd)


def _combine_kernel(x1_ref, y0_ref, y1_ref, wts_ref, fin_ref, o_ref):
    w = wts_ref[...]
    moe = w[:, 0:1] * y0_ref[...] + w[:, 1:2] * y1_ref[...]
    o_ref[...] = _rms(x1_ref[...] + moe, fin_ref[...])


def _combine(x1, y2, wts, fin, tm):
    n, d = x1.shape
    n_blocks = n // tm
    return pl.pallas_call(
        _combine_kernel,
        out_shape=jax.ShapeDtypeStruct((n, d), F32),
        grid=(n_blocks,),
        in_specs=[pl.BlockSpec((tm, d), lambda i: (i, 0)),
                  pl.BlockSpec((tm, d), lambda i: (i, 0)),
                  pl.BlockSpec((tm, d), lambda i: (i + n_blocks, 0)),
                  pl.BlockSpec((tm, LANES), lambda i: (i, 0)),
                  pl.BlockSpec((1, d), lambda i: (0, 0))],
        out_specs=pl.BlockSpec((tm, d), lambda i: (i, 0)),
        compiler_params=_params(("arbitrary",)),
        name="combine",
    )(x1, y2, y2, wts, fin)


def _routing_plan(ids, tm):
    n = ids.shape[0]
    e_flat = ids[:, :2].reshape(-1)
    onehot = (e_flat[:, None] == jnp.arange(N_EXPERTS, dtype=jnp.int32)[None, :]).astype(jnp.int32)
    csum = jnp.cumsum(onehot, axis=0)
    counts = csum[-1]
    padded = (counts + tm - 1) // tm * tm
    ends = jnp.cumsum(padded)
    pos = jnp.sum(onehot * (csum - 1 + (ends - padded)[None, :]), axis=1)
    total = 2 * n + N_EXPERTS * tm
    inv = jnp.full((total,), -1, jnp.int32).at[pos].set(jnp.arange(2 * n, dtype=jnp.int32))
    tile_start = jnp.arange(total // tm, dtype=jnp.int32) * tm
    tile_expert = jnp.sum((tile_start[:, None] >= ends[None, :]).astype(jnp.int32), axis=1)
    return tile_expert, (ends[-1:] // tm).astype(jnp.int32), inv


def _ffn_moe(x2d, merged, wo, g, rw, wg, wu, wd, fin, tm, tm_e):
    x1, ids, wts = _router(x2d, merged, wo, g, rw, tm)
    tile_expert, n_used, inv = _routing_plan(ids, tm_e)
    y2 = _experts(x1, g, wg, wu, wd, tile_expert, n_used, inv, tm_e)
    return _combine(x1, y2, wts, fin, tm)


def _prep_layer_weights(l, w_in, conv_w, shift_mu, decay_w0, decay_w2, aaa_a0, aaa_a2, gate_g2,
                        key_k, key_a, bonus_r_k, lnx_w, lnx_b, w_out):
    zero = jnp.zeros((DECAY_LORA, D_RWKV), F32)
    wwa = jnp.concatenate([jnp.concatenate([decay_w2[l], zero], axis=1),
                           jnp.concatenate([zero, aaa_a2[l]], axis=1)], axis=0).astype(BF16)
    vecs = jnp.stack([decay_w0[l], aaa_a0[l], key_k[l], key_a[l], bonus_r_k[l], lnx_w[l], lnx_b[l],
                      jnp.zeros((D_RWKV,), F32)])
    return dict(w_in=w_in[l].astype(BF16), convw=conv_w[l], mu=shift_mu[l][None], vecs=vecs,
                wwa=wwa, g2=gate_g2[l].astype(BF16), w_out=w_out[l].astype(BF16))


def _row_tile(n, want):
    return want if n % want == 0 else n


def _tiles(n):
    big_experts = 2 * n >= 4 * 512 * N_EXPERTS
    return _row_tile(n, 512), _row_tile(n, 256), _row_tile(2 * n, 512 if big_experts else 256)


def _trunk(x, conv_st, shift_st, wkv_st, layers, ffn_norm, dense, moe, final_norm):
    batch, seq, d = x.shape
    n = batch * seq
    x2d = x.reshape(n, d)
    tm, tm_proj, tm_expert = _tiles(n)
    convs, shifts, wkvs = [], [], []
    for l, lw in enumerate(layers):
        p = _norm_proj(x2d, lw['norm'], lw['w_in'], tm_proj, 6)
        merged, c, s, w = _mixer(p, conv_st[l], shift_st[l][:, None], wkv_st[l],
                                 lw['convw'], lw['mu'], lw['vecs'], lw['wwa'], lw['g2'],
                                 batch, seq)
        convs.append(c)
        shifts.append(s[:, 0])
        wkvs.append(w)
        if l % 2 == 0:
            wg, wu, wd = dense
            x2d = _ffn_dense(x2d, merged, lw['w_out'], ffn_norm[l][None], wg, wu, wd, tm,
                             2)
        else:
            rw, wg, wu, wd = moe
            x2d = _ffn_moe(x2d, merged, lw['w_out'], ffn_norm[l][None], rw, wg, wu, wd,
                           final_norm[None], tm, tm_expert)
    return x2d.reshape(batch, seq, d), jnp.stack(convs), jnp.stack(shifts), jnp.stack(wkvs)


def kernel(x_prompt, x_sample, state_conv, state_shift, state_wkv, mix_norm, w_in, conv_w, shift_mu,
           decay_w0, decay_w2, aaa_a0, aaa_a2, gate_g2, key_k, key_a, bonus_r_k, lnx_w, lnx_b, w_out,
           ffn_norm, ffn_w_gate, ffn_w_up, ffn_w_down, router_w, moe_w_gate, moe_w_up, moe_w_down,
           final_norm):
    depth = w_in.shape[0]
    assert depth == 2 and ffn_w_gate.shape[0] == 1 and moe_w_gate.shape[0] == 1
    layers = []
    for l in range(depth):
        lw = _prep_layer_weights(l, w_in, conv_w, shift_mu, decay_w0, decay_w2, aaa_a0, aaa_a2,
                                 gate_g2, key_k, key_a, bonus_r_k, lnx_w, lnx_b, w_out)
        lw['norm'] = mix_norm[l][None]
        layers.append(lw)
    dense = (ffn_w_gate[0].astype(BF16), ffn_w_up[0].astype(BF16), ffn_w_down[0].astype(BF16))
    rw = jnp.pad(router_w[0], ((0, 0), (0, LANES - N_EXPERTS)))
    moe = (rw, moe_w_gate[0].astype(BF16), moe_w_up[0].astype(BF16), moe_w_down[0].astype(BF16))

    b = x_prompt.shape[0]
    zero_conv = jnp.zeros((depth, b) + state_conv.shape[2:], state_conv.dtype)
    zero_shift = jnp.zeros((depth, b) + state_shift.shape[2:], state_shift.dtype)
    zero_wkv = jnp.zeros((depth, b) + state_wkv.shape[2:], state_wkv.dtype)
    run = functools.partial(_trunk, layers=layers, ffn_norm=ffn_norm, dense=dense, moe=moe,
                            final_norm=final_norm)
    y_s, conv_s, shift_s, wkv_s = run(x_sample, state_conv, state_shift, state_wkv)
    y_p, conv_p, shift_p, wkv_p = run(x_prompt, zero_conv, zero_shift, zero_wkv)
    return (y_p, y_s, conv_p, shift_p, wkv_p, conv_s, shift_s, wkv_s)
```

```python
import functools

import jax
import jax.numpy as jnp
from jax import lax
from jax.experimental import pallas as pl
from jax.experimental.pallas import tpu as pltpu

F32 = jnp.float32
BF16 = jnp.bfloat16

D_MODEL = 1024
N_HEADS = 16
HEAD_DIM = 64
D_RWKV = N_HEADS * HEAD_DIM
D_CONV = 1024
CONV_W = 3
DECAY_LORA = 64
AAA_LORA = 64
GATE_LORA = 128
D_SHIFT = 3 * D_RWKV + DECAY_LORA + AAA_LORA + GATE_LORA
D_PROJ = 2 * D_MODEL + 3 * D_CONV + D_SHIFT
N_EXPERTS = 8
RMS_EPS = 1e-5
GN_EPS = 64e-5
L2_EPS = 1e-12

LANES = 128
SUBLANES = 8
CHUNK = 64
HEADS_PER_GROUP = LANES // HEAD_DIM
SEQS_PER_STEP = 2
CHUNKS_PER_STEP = 2
PIECES = 4
N_GROUPS = N_HEADS // HEADS_PER_GROUP
VMEM_LIMIT = 56 * 1024 * 1024

_ZA, _ZB, _CB, _CC, _CHH, _PS = (0, D_MODEL, 2 * D_MODEL, 2 * D_MODEL + D_CONV,
                                 2 * D_MODEL + 2 * D_CONV, 2 * D_MODEL + 3 * D_CONV)
_R, _K, _V, _WA, _G = 0, D_RWKV, 2 * D_RWKV, 3 * D_RWKV, 3 * D_RWKV + DECAY_LORA + AAA_LORA


def _params(semantics):
    return pltpu.CompilerParams(dimension_semantics=semantics, vmem_limit_bytes=VMEM_LIMIT)


def _rms(x, g):
    ms = jnp.mean(x * x, axis=-1, keepdims=True)
    return x * lax.rsqrt(ms + RMS_EPS) * g


def _split(x):
    hi = x.astype(BF16)
    lo = (x - hi.astype(F32)).astype(BF16)
    return hi, lo


_NN = (((1,), (0,)), ((), ()))
_NT = (((1,), (1,)), ((), ()))


def _dot(a, b, dims=_NN):
    return lax.dot_general(a, b, dims, preferred_element_type=F32)


def _mm1(a, b, dims=_NN):
    return _dot(a.astype(BF16), b.astype(BF16), dims)


def _mm3(a, b, dims=_NN):
    a1, a2 = _split(a)
    b1, b2 = _split(b)
    return _dot(a1, b1, dims) + (_dot(a1, b2, dims) + _dot(a2, b1, dims))


def _segment_sums(xs, seg_ones):
    rows = xs[0].shape[0]
    parts = [half for x in xs for half in _split(x)]
    out = _dot(jnp.concatenate(parts, axis=0), seg_ones)
    return [out[2 * i * rows:(2 * i + 1) * rows] + out[(2 * i + 1) * rows:(2 * i + 2) * rows]
            for i in range(len(xs))]


def _norm_proj_kernel(x_ref, g_ref, w_ref, o_ref, *, n_col):
    xn = _rms(x_ref[...], g_ref[...]).astype(BF16)
    tn = w_ref.shape[1] // n_col
    for j in range(n_col):
        o_ref[:, j * tn:(j + 1) * tn] = _dot(xn, w_ref[:, j * tn:(j + 1) * tn])


def _norm_proj(x2d, g, w_bf16, tm, n_col):
    n, d = x2d.shape
    dp = w_bf16.shape[1]
    return pl.pallas_call(
        functools.partial(_norm_proj_kernel, n_col=n_col),
        out_shape=jax.ShapeDtypeStruct((n, dp), F32),
        grid=(n // tm,),
        in_specs=[pl.BlockSpec((tm, d), lambda i: (i, 0)),
                  pl.BlockSpec((1, d), lambda i: (0, 0)),
                  pl.BlockSpec((d, dp), lambda i: (0, 0), pipeline_mode=pl.Buffered(1))],
        out_specs=pl.BlockSpec((tm, dp), lambda i: (i, 0)),
        compiler_params=_params(("arbitrary",)),
        name="norm_proj",
    )(x2d, g, w_bf16)


def _block_diag(x, head_masks):
    x = x.astype(BF16)
    return jnp.concatenate([x * m for m in head_masks], axis=0)


def _wkv_prepare(r, k, v, kkn, a, ld, cum, head_masks):
    bd = functools.partial(_block_diag, head_masks=head_masks)
    cend = cum[CHUNK - 1:CHUNK, :]
    w_prev = jnp.exp(cum - ld)
    w_t = jnp.exp(cum)
    w_inv = jnp.exp(-cum)
    w_rest = jnp.exp(cend - cum)
    b = kkn * a
    return dict(
        lhs2=jnp.concatenate([-kkn * w_prev, r * w_t], axis=0).astype(BF16),
        rhs=jnp.concatenate([bd(b * w_inv), bd(k * w_inv)], axis=0),
        bd_v=bd(v), v=v, bk_h=jnp.concatenate([b * w_rest, k * w_rest], axis=0).astype(BF16),
        decay=jnp.exp(cend))


def _wkv_recurrence(ops, get_state, head_masks, causal2, bd_mask, out):
    groups = range(len(ops))
    bd = functools.partial(_block_diag, head_masks=head_masks)
    n_steps = CHUNK.bit_length() - 1
    s_prev = get_state()
    g_b, g_k, g_s = [], [], []
    for g in groups:
        gram = _mm1(ops[g]['lhs2'],
                    jnp.concatenate([ops[g]['rhs'], s_prev[g].astype(BF16)], axis=0), _NT)
        g_b.append(jnp.where(causal2, gram[:, :LANES], 0.0))
        g_k.append(jnp.where(causal2, gram[:, LANES:2 * LANES], 0.0))
        g_s.append(gram[:, 2 * LANES:])
    yield
    u = [g_s[g][:CHUNK] + _mm1(g_k[g][:CHUNK], ops[g]['bd_v']) for g in groups]
    pw = [g_b[g][:CHUNK] for g in groups]
    yield
    for i in range(n_steps):
        for g in groups:
            if i + 1 < n_steps:
                res = _mm1(pw[g], jnp.concatenate([bd(u[g]), bd(pw[g])], axis=1))
                u[g] = u[g] + res[:, :LANES]
                pw[g] = res[:, LANES:]
            else:
                u[g] = u[g] + _mm1(pw[g], bd(u[g]))
        yield
    y = []
    for g in groups:
        l_r = jnp.concatenate([g_b[g][CHUNK:], g_k[g][CHUNK:]], axis=1)
        y.append(g_s[g][CHUNK:] + _mm1(l_r, jnp.concatenate([bd(u[g]), ops[g]['bd_v']], axis=0)))
    out['y'] = y
    yield
    s_new = []
    for g in groups:
        uv_t = jnp.concatenate([u[g], ops[g]['v']], axis=0).T
        upd = _mm1(uv_t, ops[g]['bk_h'])
        s_new.append(s_prev[g] * ops[g]['decay'] + jnp.where(bd_mask, upd, 0.0))
    out['s_new'] = s_new
    yield


WKV_STAGES = 4 + CHUNK.bit_length() - 1


def _run_tasks(tasks):
    live = list(tasks)
    rnd = 0
    while live:
        for task in list(live):
            if task[0] <= rnd:
                try:
                    next(task[1])
                except StopIteration:
                    live.remove(task)
        rnd += 1


def _mixer_kernel(p_ref, conv0_ref, shift0_ref, wkv0_ref, convw_ref, mu_ref, vec_ref, wwa_ref,
                  g2_ref, merged_ref, nconv_ref, nshift_ref, nwkv_ref, cbuf, sbuf, state,
                  *, n_steps, n_seq, n_ch):
    t = pl.program_id(1)
    seqs = range(n_seq)
    groups = range(N_GROUPS)
    pairs = [(s, g) for s in seqs for g in groups]
    n_pairs = len(pairs)
    rows = n_ch * CHUNK

    @pl.when(t == 0)
    def _():
        zero = jnp.zeros((HEAD_DIM, HEAD_DIM), F32)
        for s in seqs:
            cbuf[s, 0:SUBLANES, :] = jnp.zeros((SUBLANES, D_CONV), F32)
            cbuf[s, SUBLANES - (CONV_W - 1):SUBLANES, :] = conv0_ref[s]
            sbuf[s, 0:SUBLANES, :] = jnp.zeros((SUBLANES, D_SHIFT), F32)
            sbuf[s, SUBLANES - 1:SUBLANES, :] = shift0_ref[s]
            for g in groups:
                h0 = g * HEADS_PER_GROUP
                state[s * N_GROUPS + g] = jnp.concatenate(
                    [jnp.concatenate([wkv0_ref[s, h0 + h] if hh == h else zero
                                      for hh in range(HEADS_PER_GROUP)], axis=1)
                     for h in range(HEADS_PER_GROUP)], axis=0)

    w0, a0, k_k, k_a, r_k, lnx_w, lnx_b = (vec_ref[i:i + 1, :] for i in range(7))
    lane = lax.broadcasted_iota(jnp.int32, (CHUNK, LANES), 1)
    lane_r = lax.broadcasted_iota(jnp.int32, (rows, LANES), 1)
    row = lax.broadcasted_iota(jnp.int32, (rows, rows), 0)
    col = lax.broadcasted_iota(jnp.int32, (rows, rows), 1)
    chunk_shift = CHUNK.bit_length() - 1
    tri = jnp.where((col <= row) & ((col >> chunk_shift) == (row >> chunk_shift)),
                    1.0, 0.0).astype(BF16)
    head_shift = HEAD_DIM.bit_length() - 1
    head_masks = [jnp.where((lane >> head_shift) == h, 1.0, 0.0).astype(BF16)
                  for h in range(HEADS_PER_GROUP)]
    row2 = lax.broadcasted_iota(jnp.int32, (2 * CHUNK, LANES), 0)
    lane2 = lax.broadcasted_iota(jnp.int32, (2 * CHUNK, LANES), 1)
    causal2 = (lane2 & (HEAD_DIM - 1)) < (row2 & (CHUNK - 1)) + (row2 >> chunk_shift)
    rowl = lax.broadcasted_iota(jnp.int32, (LANES, LANES), 0)
    lanel = lax.broadcasted_iota(jnp.int32, (LANES, LANES), 1)
    bd_mask = (rowl >> head_shift) == (lanel >> head_shift)
    seg_ones = jnp.where(bd_mask, 1.0, 0.0).astype(BF16)
    sls = [slice(g * LANES, (g + 1) * LANES) for g in groups]

    c0 = SUBLANES
    y_a, gate, new_conv, new_shift = [], [], [], []
    r, k, v, a, ld, cum = [], [], [], [], [], []
    for s in seqs:
        ch = p_ref[s, :, _CC:_CC + D_CONV] * p_ref[s, :, _CHH:_CHH + D_CONV]
        cbuf[s, c0:c0 + rows, :] = ch
        conv = (cbuf[s, c0 - 2:c0 - 2 + rows, :] * convw_ref[0:1, :]
                + cbuf[s, c0 - 1:c0 - 1 + rows, :] * convw_ref[1:2, :]
                + ch * convw_ref[2:3, :])
        y_a.append(jax.nn.sigmoid(p_ref[s, :, _ZA:_ZA + D_MODEL])
                   * (p_ref[s, :, _CB:_CB + D_CONV] * conv))
        new_conv.append(cbuf[s, c0 + rows - (CONV_W - 1):c0 + rows, :])
        cbuf[s, 0:SUBLANES, :] = cbuf[s, rows:rows + SUBLANES, :]

        ps = p_ref[s, :, _PS:_PS + D_SHIFT]
        sbuf[s, c0:c0 + rows, :] = ps
        prev = sbuf[s, c0 - 1:c0 - 1 + rows, :]
        xm = ps + (prev - ps) * mu_ref[...]
        new_shift.append(sbuf[s, c0 + rows - 1:c0 + rows, :])
        sbuf[s, 0:SUBLANES, :] = sbuf[s, rows:rows + SUBLANES, :]

        wa_in = xm[:, _WA:_WA + LANES]
        wa_in = jnp.where(lane_r < DECAY_LORA, jnp.tanh(wa_in), wa_in)
        wa = _mm1(wa_in, wwa_ref[...])
        gate.append(_mm1(jax.nn.sigmoid(xm[:, _G:_G + GATE_LORA]), g2_ref[...]))
        ld_all = -jnp.exp(-0.5) * jax.nn.sigmoid(w0 + wa[:, :D_RWKV])
        a_all = jax.nn.sigmoid(a0 + wa[:, D_RWKV:])
        ld1 = ld_all.astype(BF16)
        ld2 = (ld_all - ld1.astype(F32)).astype(BF16)
        ld3 = (ld_all - ld1.astype(F32) - ld2.astype(F32)).astype(BF16)
        cum_all = _dot(tri, ld1) + (_dot(tri, ld2) + _dot(tri, ld3))
        r.append(xm[:, _R:_R + D_RWKV])
        k.append(xm[:, _K:_K + D_RWKV])
        v.append(xm[:, _V:_V + D_RWKV])
        a.append(a_all)
        ld.append(ld_all)
        cum.append(cum_all)

    states = [[state[i] for i in range(n_pairs)]]
    pieces = [list(range(j, n_pairs, PIECES)) for j in range(PIECES)]
    ops = [[None] * n_pairs for _ in range(n_ch)]
    bonus = [[None] * n_pairs for _ in range(n_ch)]
    results = [dict() for _ in range(n_ch)]

    def prepare(c, idx):
        rs = slice(c * CHUNK, (c + 1) * CHUNK)
        cut = lambda xs, i: xs[pairs[i][0]][rs, sls[pairs[i][1]]]
        kk = [cut(k, i) * k_k[:, sls[pairs[i][1]]] for i in idx]
        kf = [cut(k, i) * (1.0 + (cut(a, i) - 1.0) * k_a[:, sls[pairs[i][1]]]) for i in idx]
        sums = _segment_sums([x * x for x in kk]
                             + [cut(r, i) * kf[n] * r_k[:, sls[pairs[i][1]]]
                                for n, i in enumerate(idx)], seg_ones)
        yield
        for n, i in enumerate(idx):
            kkn = kk[n] / jnp.maximum(jnp.sqrt(sums[n]), L2_EPS)
            bonus[c][i] = sums[len(idx) + n] * cut(v, i)
            ops[c][i] = _wkv_prepare(cut(r, i), kf[n], cut(v, i), kkn, cut(a, i), cut(ld, i),
                                     cut(cum, i), head_masks)

    def recur(c):
        yield from _wkv_recurrence(ops[c], lambda: states[c], head_masks, causal2, bd_mask,
                                   results[c])
        states.append(results[c]['s_new'])

    def finish(c, idx):
        rs = slice(c * CHUNK, (c + 1) * CHUNK)
        y = [results[c]['y'][i] for i in idx]
        mean = [m * (1.0 / HEAD_DIM) for m in _segment_sums(y, seg_ones)]
        yield
        dev = [y[n] - mean[n] for n in range(len(idx))]
        var = [m * (1.0 / HEAD_DIM) for m in _segment_sums([d * d for d in dev], seg_ones)]
        yield
        for n, i in enumerate(idx):
            s, g = pairs[i]
            sl = sls[g]
            yn = dev[n] * lax.rsqrt(var[n] + GN_EPS) * lnx_w[:, sl] + lnx_b[:, sl]
            y_b = (yn + bonus[c][i]) * gate[s][rs, sl]
            z_b = p_ref[s, rs, _ZB + g * LANES:_ZB + (g + 1) * LANES]
            merged_ref[s, rs, sl] = (y_a[s][rs, sl] + jax.nn.sigmoid(z_b) * y_b).astype(merged_ref.dtype)

    base = lambda c: 2 + WKV_STAGES * c
    tasks = []
    for c in range(n_ch):
        for j, idx in enumerate(pieces):
            tasks.append((0 if c == 0 else base(c - 1) + 2 * j, prepare(c, idx)))
    for c in range(n_ch):
        tasks.append((base(c), recur(c)))
        for j, idx in enumerate(pieces):
            tasks.append((base(c + 1) + (2 * j if c + 1 < n_ch else 0), finish(c, idx)))
    _run_tasks(sorted(tasks, key=lambda task: task[0]))
    for i in range(n_pairs):
        state[i] = states[n_ch][i]

    @pl.when(t == n_steps - 1)
    def _():
        for s in seqs:
            nconv_ref[s] = new_conv[s]
            nshift_ref[s] = new_shift[s]
            for h in range(N_HEADS):
                o = (h % HEADS_PER_GROUP) * HEAD_DIM
                nwkv_ref[s, h] = state[s * N_GROUPS + h // HEADS_PER_GROUP,
                                       o:o + HEAD_DIM, o:o + HEAD_DIM]


def _mixer(p2d, conv0, shift0, wkv0, convw, mu, vecs, wwa, g2, batch, seq):
    n_seq = SEQS_PER_STEP if batch % SEQS_PER_STEP == 0 else 1
    n_ch = CHUNKS_PER_STEP if seq % (CHUNKS_PER_STEP * CHUNK) == 0 else 1
    rows = n_ch * CHUNK
    n_steps = seq // rows
    kern = functools.partial(_mixer_kernel, n_steps=n_steps, n_seq=n_seq, n_ch=n_ch)
    const = lambda b, t: (0, 0)
    per_b3 = lambda b, t: (b, 0, 0)
    per_b4 = lambda b, t: (b, 0, 0, 0)
    merged, conv, shift, wkv = pl.pallas_call(
        kern,
        out_shape=(jax.ShapeDtypeStruct((batch, seq, D_MODEL), BF16),
                   jax.ShapeDtypeStruct((batch, CONV_W - 1, D_CONV), F32),
                   jax.ShapeDtypeStruct((batch, 1, D_SHIFT), F32),
                   jax.ShapeDtypeStruct((batch, N_HEADS, HEAD_DIM, HEAD_DIM), F32)),
        grid=(batch // n_seq, n_steps),
        in_specs=[pl.BlockSpec((n_seq, rows, D_PROJ), lambda b, t: (b, t, 0)),
                  pl.BlockSpec((n_seq, CONV_W - 1, D_CONV), per_b3),
                  pl.BlockSpec((n_seq, 1, D_SHIFT), per_b3),
                  pl.BlockSpec((n_seq, N_HEADS, HEAD_DIM, HEAD_DIM), per_b4),
                  pl.BlockSpec((CONV_W, D_CONV), const),
                  pl.BlockSpec((1, D_SHIFT), const),
                  pl.BlockSpec((SUBLANES, D_RWKV), const),
                  pl.BlockSpec((LANES, 2 * D_RWKV), const),
                  pl.BlockSpec((GATE_LORA, D_RWKV), const)],
        out_specs=(pl.BlockSpec((n_seq, rows, D_MODEL), lambda b, t: (b, t, 0)),
                   pl.BlockSpec((n_seq, CONV_W - 1, D_CONV), per_b3),
                   pl.BlockSpec((n_seq, 1, D_SHIFT), per_b3),
                   pl.BlockSpec((n_seq, N_HEADS, HEAD_DIM, HEAD_DIM), per_b4)),
        scratch_shapes=[pltpu.VMEM((n_seq, rows + SUBLANES, D_CONV), F32),
                        pltpu.VMEM((n_seq, rows + SUBLANES, D_SHIFT), F32),
                        pltpu.VMEM((n_seq * N_GROUPS, LANES, LANES), F32)],
        compiler_params=_params(("arbitrary", "arbitrary")),
        name="mixer",
    )(p2d.reshape(batch, seq, D_PROJ), conv0, shift0, wkv0, convw, mu, vecs, wwa, g2)
    return merged.reshape(batch * seq, D_MODEL), conv, shift, wkv


def _silu_mul(gate, up):
    return gate * jax.nn.sigmoid(gate) * up


def _ffn_dense_kernel(x_ref, m_ref, wo_ref, g_ref, wg_ref, wu_ref, wd_ref, o_ref, *, n_f):
    x1 = x_ref[...] + _dot(m_ref[...], wo_ref[...])
    h = _rms(x1, g_ref[...]).astype(BF16)
    tf = wg_ref.shape[1] // n_f
    acc = x1
    for j in range(n_f):
        act = _silu_mul(_dot(h, wg_ref[:, j * tf:(j + 1) * tf]),
                        _dot(h, wu_ref[:, j * tf:(j + 1) * tf])).astype(BF16)
        acc = acc + _dot(act, wd_ref[j * tf:(j + 1) * tf, :])
    o_ref[...] = acc


def _ffn_dense(x2d, merged, wo, g, wg, wu, wd, tm, n_f):
    n, d = x2d.shape
    f = wg.shape[1]
    row = lambda i: (i, 0)
    resident = lambda shape: pl.BlockSpec(shape, lambda i: (0, 0), pipeline_mode=pl.Buffered(1))
    return pl.pallas_call(
        functools.partial(_ffn_dense_kernel, n_f=n_f),
        out_shape=jax.ShapeDtypeStruct((n, d), F32),
        grid=(n // tm,),
        in_specs=[pl.BlockSpec((tm, d), row), pl.BlockSpec((tm, d), row),
                  resident((d, d)), pl.BlockSpec((1, d), lambda i: (0, 0)),
                  resident((d, f)), resident((d, f)), resident((f, d))],
        out_specs=pl.BlockSpec((tm, d), row),
        compiler_params=_params(("arbitrary",)),
        name="ffn_dense",
    )(x2d, merged, wo, g, wg, wu, wd)


def _top2(logits):
    lane_i = lax.broadcasted_iota(jnp.int32, logits.shape, 1)
    lane = lane_i.astype(F32)
    neg = jnp.float32(-jnp.inf)
    lg = jnp.where(lane_i < N_EXPERTS, logits, neg)
    m1 = jnp.max(lg, axis=-1, keepdims=True)
    i1 = jnp.min(jnp.where(lg == m1, lane, float(LANES)), axis=-1, keepdims=True)
    lg2 = jnp.where(lane == i1, neg, lg)
    m2 = jnp.max(lg2, axis=-1, keepdims=True)
    i2 = jnp.min(jnp.where(lg2 == m2, lane, float(LANES)), axis=-1, keepdims=True)
    e2 = jnp.exp(m2 - m1)
    den = 1.0 + e2
    ids = jnp.where(lane_i == 0, i1, jnp.where(lane_i == 1, i2, 0.0)).astype(jnp.int32)
    wts = jnp.where(lane_i == 0, 1.0 / den, jnp.where(lane_i == 1, e2 / den, 0.0))
    return ids, wts


def _router_kernel(x_ref, m_ref, wo_ref, g_ref, rw_ref, x1_ref, ids_ref, wts_ref):
    x1 = x_ref[...] + _dot(m_ref[...], wo_ref[...])
    x1_ref[...] = x1
    ids_ref[...], wts_ref[...] = _top2(_mm3(_rms(x1, g_ref[...]), rw_ref[...]))


def _router(x2d, merged, wo, g, rw, tm):
    n, d = x2d.shape
    row = lambda i: (i, 0)
    const = lambda i: (0, 0)
    return pl.pallas_call(
        _router_kernel,
        out_shape=(jax.ShapeDtypeStruct((n, d), F32),
                   jax.ShapeDtypeStruct((n, LANES), jnp.int32),
                   jax.ShapeDtypeStruct((n, LANES), F32)),
        grid=(n // tm,),
        in_specs=[pl.BlockSpec((tm, d), row), pl.BlockSpec((tm, d), row),
                  pl.BlockSpec((d, d), const), pl.BlockSpec((1, d), const),
                  pl.BlockSpec((d, LANES), const)],
        out_specs=(pl.BlockSpec((tm, d), row), pl.BlockSpec((tm, LANES), row),
                   pl.BlockSpec((tm, LANES), row)),
        compiler_params=_params(("arbitrary",)),
        name="router",
    )(x2d, merged, wo, g, rw)


def _experts_kernel(te_ref, nv_ref, inv_prev_ref, inv_ref, inv_next_ref, x1_hbm, g_ref,
                    wg_ref, wu_ref, wd_ref, y_hbm, xbuf, obuf, gather_sem, scatter_sem,
                    *, tm, n_tok, f_split):
    t = pl.program_id(0)
    n_used = nv_ref[0]
    cur = t & 1
    nxt = 1 - cur

    def gather_row(idx_ref, buf, r):
        tok = jnp.maximum(idx_ref[0, 0, r], 0) >> 1
        pltpu.make_async_copy(x1_hbm.at[pl.ds(tok, 1)], xbuf.at[buf, pl.ds(r, 1)],
                              gather_sem.at[buf]).start()

    def gather_loop(idx_ref, buf):
        def body(r, c):
            gather_row(idx_ref, buf, r)
            return c
        lax.fori_loop(0, tm, body, 0, unroll=8)

    def wait_gather(buf):
        pltpu.make_async_copy(x1_hbm.at[pl.ds(0, tm)], xbuf.at[buf], gather_sem.at[buf]).wait()

    def scatter_row(idx_ref, buf, r, real):
        a = idx_ref[0, 0, r]
        row = jnp.where((a >= 0) & real, (a & 1) * n_tok + (a >> 1), 2 * n_tok + r)
        pltpu.make_async_copy(obuf.at[buf, pl.ds(r, 1)], y_hbm.at[pl.ds(row, 1)], scatter_sem).start()

    def wait_scatter(buf):
        pltpu.make_async_copy(obuf.at[buf], y_hbm.at[pl.ds(0, tm)], scatter_sem).wait()

    @pl.when(t == 0)
    def _():
        obuf[...] = jnp.zeros_like(obuf)
        gather_loop(inv_ref, 0)

    wait_gather(cur)

    @pl.when(t < n_used)
    def _():
        h = _rms(xbuf[cur], g_ref[...]).astype(BF16)
        fs = wg_ref.shape[2] // f_split
        rows_per = -(-tm // max(1, (2 * f_split) // 3))
        acc = None
        for i in range(f_split):
            act = _silu_mul(_dot(h, wg_ref[0, :, i * fs:(i + 1) * fs]),
                            _dot(h, wu_ref[0, :, i * fs:(i + 1) * fs])).astype(BF16)
            part = _dot(act, wd_ref[0, i * fs:(i + 1) * fs, :])
            acc = part if acc is None else acc + part
            for r in range(i * rows_per, min((i + 1) * rows_per, tm)):
                gather_row(inv_next_ref, nxt, r)
                scatter_row(inv_prev_ref, nxt, r, t > 0)
        wait_scatter(nxt)
        obuf[cur] = acc

        @pl.when(t == n_used - 1)
        def _():
            def body(r, c):
                scatter_row(inv_ref, cur, r, True)
                return c
            lax.fori_loop(0, tm, body, 0, unroll=8)
            wait_scatter(cur)

    @pl.when(t >= n_used)
    def _():
        gather_loop(inv_next_ref, nxt)

    @pl.when(t == pl.num_programs(0) - 1)
    def _():
        wait_gather(nxt)


def _experts(x1, g, wg, wu, wd, tile_expert, n_used, inv, tm):
    n, d = x1.shape
    n_tiles = tile_expert.shape[0]
    f = wg.shape[2]
    expert_w = lambda t, te, nv: (jnp.minimum(te[t], N_EXPERTS - 1), 0, 0)
    single = pl.Buffered(1)
    inv3 = inv.reshape(n_tiles, 1, tm)
    idx_spec = lambda shift: pl.BlockSpec(
        (1, 1, tm), lambda t, te, nv: (jnp.clip(t + shift, 0, n_tiles - 1), 0, 0),
        memory_space=pltpu.SMEM)
    grid_spec = pltpu.PrefetchScalarGridSpec(
        num_scalar_prefetch=2,
        grid=(n_tiles,),
        in_specs=[idx_spec(-1), idx_spec(0), idx_spec(1),
                  pl.BlockSpec(memory_space=pl.ANY),
                  pl.BlockSpec((1, d), lambda t, te, nv: (0, 0)),
                  pl.BlockSpec((1, d, f), expert_w, pipeline_mode=single),
                  pl.BlockSpec((1, d, f), expert_w, pipeline_mode=single),
                  pl.BlockSpec((1, f, d), expert_w, pipeline_mode=single)],
        out_specs=pl.BlockSpec(memory_space=pl.ANY),
        scratch_shapes=[pltpu.VMEM((2, tm, d), F32), pltpu.VMEM((2, tm, d), F32),
                        pltpu.SemaphoreType.DMA((2,)), pltpu.SemaphoreType.DMA(())])
    return pl.pallas_call(
        functools.partial(_experts_kernel, tm=tm, n_tok=n, f_split=f // (2 * LANES)),
        out_shape=jax.ShapeDtypeStruct((2 * n + tm, d), F32),
        grid_spec=grid_spec,
        compiler_params=_params(("arbitrary",)),
        name="experts",
    )(tile_expert, n_used, inv3, inv3, inv3, x1, g, wg, wu, wd)


def _combine_kernel(x1_ref, y0_ref, y1_ref, wts_ref, fin_ref, o_ref):
    w = wts_ref[...]
    moe = w[:, 0:1] * y0_ref[...] + w[:, 1:2] * y1_ref[...]
    o_ref[...] = _rms(x1_ref[...] + moe, fin_ref[...])


def _combine(x1, y2, wts, fin, tm):
    n, d = x1.shape
    n_blocks = n // tm
    return pl.pallas_call(
        _combine_kernel,
        out_shape=jax.ShapeDtypeStruct((n, d), F32),
        grid=(n_blocks,),
        in_specs=[pl.BlockSpec((tm, d), lambda i: (i, 0)),
                  pl.BlockSpec((tm, d), lambda i: (i, 0)),
                  pl.BlockSpec((tm, d), lambda i: (i + n_blocks, 0)),
                  pl.BlockSpec((tm, LANES), lambda i: (i, 0)),
                  pl.BlockSpec((1, d), lambda i: (0, 0))],
        out_specs=pl.BlockSpec((tm, d), lambda i: (i, 0)),
        compiler_params=_params(("arbitrary",)),
        name="combine",
    )(x1, y2, y2, wts, fin)


def _routing_plan(ids, tm):
    n = ids.shape[0]
    e_flat = ids[:, :2].reshape(-1)
    onehot = (e_flat[:, None] == jnp.arange(N_EXPERTS, dtype=jnp.int32)[None, :]).astype(jnp.int32)
    csum = jnp.cumsum(onehot, axis=0)
    counts = csum[-1]
    padded = (counts + tm - 1) // tm * tm
    ends = jnp.cumsum(padded)
    pos = jnp.sum(onehot * (csum - 1 + (ends - padded)[None, :]), axis=1)
    total = 2 * n + N_EXPERTS * tm
    inv = jnp.full((total,), -1, jnp.int32).at[pos].set(jnp.arange(2 * n, dtype=jnp.int32))
    tile_start = jnp.arange(total // tm, dtype=jnp.int32) * tm
    tile_expert = jnp.sum((tile_start[:, None] >= ends[None, :]).astype(jnp.int32), axis=1)
    return tile_expert, (ends[-1:] // tm).astype(jnp.int32), inv


def _ffn_moe(x2d, merged, wo, g, rw, wg, wu, wd, fin, tm, tm_router, tm_expert):
    x1, ids, wts = _router(x2d, merged, wo, g, rw, tm_router)
    tile_expert, n_used, inv = _routing_plan(ids, tm_expert)
    y2 = _experts(x1, g, wg, wu, wd, tile_expert, n_used, inv, tm_expert)
    return _combine(x1, y2, wts, fin, tm)


def _prep_layer_weights(l, w_in, conv_w, shift_mu, decay_w0, decay_w2, aaa_a0, aaa_a2, gate_g2,
                        key_k, key_a, bonus_r_k, lnx_w, lnx_b, w_out):
    zero = jnp.zeros((DECAY_LORA, D_RWKV), F32)
    wwa = jnp.concatenate([jnp.concatenate([decay_w2[l], zero], axis=1),
                           jnp.concatenate([zero, aaa_a2[l]], axis=1)], axis=0).astype(BF16)
    vecs = jnp.stack([decay_w0[l], aaa_a0[l], key_k[l], key_a[l], bonus_r_k[l], lnx_w[l], lnx_b[l],
                      jnp.zeros((D_RWKV,), F32)])
    return dict(w_in=w_in[l].astype(BF16), convw=conv_w[l], mu=shift_mu[l][None], vecs=vecs,
                wwa=wwa, g2=gate_g2[l].astype(BF16), w_out=w_out[l].astype(BF16))


def _row_tile(n, want):
    return want if n % want == 0 else n


def _tiles(n):
    big_experts = 2 * n >= 4 * 512 * N_EXPERTS
    return (_row_tile(n, 512), _row_tile(n, 256), _row_tile(n, 1024),
            _row_tile(2 * n, 512 if big_experts else 256))


def _trunk(x, conv_st, shift_st, wkv_st, layers, ffn_norm, dense, moe, final_norm):
    batch, seq, d = x.shape
    n = batch * seq
    x2d = x.reshape(n, d)
    tm, tm_proj, tm_router, tm_expert = _tiles(n)
    convs, shifts, wkvs = [], [], []
    for l, lw in enumerate(layers):
        p = _norm_proj(x2d, lw['norm'], lw['w_in'], tm_proj, 6)
        merged, c, s, w = _mixer(p, conv_st[l], shift_st[l][:, None], wkv_st[l],
                                 lw['convw'], lw['mu'], lw['vecs'], lw['wwa'], lw['g2'],
                                 batch, seq)
        convs.append(c)
        shifts.append(s[:, 0])
        wkvs.append(w)
        if l % 2 == 0:
            wg, wu, wd = dense
            x2d = _ffn_dense(x2d, merged, lw['w_out'], ffn_norm[l][None], wg, wu, wd, tm, 2)
        else:
            rw, wg, wu, wd = moe
            x2d = _ffn_moe(x2d, merged, lw['w_out'], ffn_norm[l][None], rw, wg, wu, wd,
                           final_norm[None], tm, tm_router, tm_expert)
    return x2d.reshape(batch, seq, d), jnp.stack(convs), jnp.stack(shifts), jnp.stack(wkvs)


def kernel(x_prompt, x_sample, state_conv, state_shift, state_wkv, mix_norm, w_in, conv_w, shift_mu,
           decay_w0, decay_w2, aaa_a0, aaa_a2, gate_g2, key_k, key_a, bonus_r_k, lnx_w, lnx_b, w_out,
           ffn_norm, ffn_w_gate, ffn_w_up, ffn_w_down, router_w, moe_w_gate, moe_w_up, moe_w_down,
           final_norm):
    depth = w_in.shape[0]
    assert depth == 2 and ffn_w_gate.shape[0] == 1 and moe_w_gate.shape[0] == 1
    layers = []
    for l in range(depth):
        lw = _prep_layer_weights(l, w_in, conv_w, shift_mu, decay_w0, decay_w2, aaa_a0, aaa_a2,
                                 gate_g2, key_k, key_a, bonus_r_k, lnx_w, lnx_b, w_out)
        lw['norm'] = mix_norm[l][None]
        layers.append(lw)
    dense = (ffn_w_gate[0].astype(BF16), ffn_w_up[0].astype(BF16), ffn_w_down[0].astype(BF16))
    rw = jnp.pad(router_w[0], ((0, 0), (0, LANES - N_EXPERTS)))
    moe = (rw, moe_w_gate[0].astype(BF16), moe_w_up[0].astype(BF16), moe_w_down[0].astype(BF16))

    b = x_prompt.shape[0]
    zero_conv = jnp.zeros((depth, b) + state_conv.shape[2:], state_conv.dtype)
    zero_shift = jnp.zeros((depth, b) + state_shift.shape[2:], state_shift.dtype)
    zero_wkv = jnp.zeros((depth, b) + state_wkv.shape[2:], state_wkv.dtype)
    run = functools.partial(_trunk, layers=layers, ffn_norm=ffn_norm, dense=dense, moe=moe,
                            final_norm=final_norm)
    y_s, conv_s, shift_s, wkv_s = run(x_sample, state_conv, state_shift, state_wkv)
    y_p, conv_p, shift_p, wkv_p = run(x_prompt, zero_conv, zero_shift, zero_wkv)
    return (y_p, y_s, conv_p, shift_p, wkv_p, conv_s, shift_s, wkv_s)
```

```python
import functools

import jax
import jax.numpy as jnp
from jax import lax
from jax.experimental import pallas as pl
from jax.experimental.pallas import tpu as pltpu

F32 = jnp.float32
BF16 = jnp.bfloat16

D_MODEL = 1024
N_HEADS = 16
HEAD_DIM = 64
D_RWKV = N_HEADS * HEAD_DIM
D_CONV = 1024
CONV_W = 3
DECAY_LORA = 64
AAA_LORA = 64
GATE_LORA = 128
D_SHIFT = 3 * D_RWKV + DECAY_LORA + AAA_LORA + GATE_LORA
D_PROJ = 2 * D_MODEL + 3 * D_CONV + D_SHIFT
N_EXPERTS = 8
RMS_EPS = 1e-5
GN_EPS = 64e-5
L2_EPS = 1e-12

LANES = 128
SUBLANES = 8
CHUNK = 64
HEADS_PER_GROUP = LANES // HEAD_DIM
SEQS_PER_STEP = 2
CHUNKS_PER_STEP = 2
PIECES = 4
N_GROUPS = N_HEADS // HEADS_PER_GROUP
VMEM_LIMIT = 56 * 1024 * 1024

_ZA, _ZB, _CB, _CC, _CHH, _PS = (0, D_MODEL, 2 * D_MODEL, 2 * D_MODEL + D_CONV,
                                 2 * D_MODEL + 2 * D_CONV, 2 * D_MODEL + 3 * D_CONV)
_R, _K, _V, _WA, _G = 0, D_RWKV, 2 * D_RWKV, 3 * D_RWKV, 3 * D_RWKV + DECAY_LORA + AAA_LORA


def _params(semantics):
    return pltpu.CompilerParams(dimension_semantics=semantics, vmem_limit_bytes=VMEM_LIMIT)


def _rms(x, g):
    ms = jnp.mean(x * x, axis=-1, keepdims=True)
    return x * lax.rsqrt(ms + RMS_EPS) * g


def _split(x):
    hi = x.astype(BF16)
    lo = (x - hi.astype(F32)).astype(BF16)
    return hi, lo


_NN = (((1,), (0,)), ((), ()))
_NT = (((1,), (1,)), ((), ()))


def _dot(a, b, dims=_NN):
    return lax.dot_general(a, b, dims, preferred_element_type=F32)


def _mm1(a, b, dims=_NN):
    return _dot(a.astype(BF16), b.astype(BF16), dims)


def _mm3(a, b, dims=_NN):
    a1, a2 = _split(a)
    b1, b2 = _split(b)
    return _dot(a1, b1, dims) + (_dot(a1, b2, dims) + _dot(a2, b1, dims))


def _segment_sums(xs, seg_ones, exact=True):
    rows = xs[0].shape[0]
    if not exact:
        out = _dot(jnp.concatenate([x.astype(BF16) for x in xs], axis=0), seg_ones)
        return [out[i * rows:(i + 1) * rows] for i in range(len(xs))]
    parts = [half for x in xs for half in _split(x)]
    out = _dot(jnp.concatenate(parts, axis=0), seg_ones)
    return [out[2 * i * rows:(2 * i + 1) * rows] + out[(2 * i + 1) * rows:(2 * i + 2) * rows]
            for i in range(len(xs))]


def _norm_proj_kernel(x_ref, g_ref, w_ref, o_ref, *, n_col):
    xn = _rms(x_ref[...], g_ref[...]).astype(BF16)
    tn = w_ref.shape[1] // n_col
    for j in range(n_col):
        o_ref[:, j * tn:(j + 1) * tn] = _dot(xn, w_ref[:, j * tn:(j + 1) * tn])


def _norm_proj(x2d, g, w_bf16, tm, n_col):
    n, d = x2d.shape
    dp = w_bf16.shape[1]
    return pl.pallas_call(
        functools.partial(_norm_proj_kernel, n_col=n_col),
        out_shape=jax.ShapeDtypeStruct((n, dp), F32),
        grid=(n // tm,),
        in_specs=[pl.BlockSpec((tm, d), lambda i: (i, 0)),
                  pl.BlockSpec((1, d), lambda i: (0, 0)),
                  pl.BlockSpec((d, dp), lambda i: (0, 0), pipeline_mode=pl.Buffered(1))],
        out_specs=pl.BlockSpec((tm, dp), lambda i: (i, 0)),
        compiler_params=_params(("arbitrary",)),
        name="norm_proj",
    )(x2d, g, w_bf16)


def _block_diag(x, head_masks):
    return jnp.concatenate([jnp.where(m, x, 0.0) for m in head_masks], axis=0).astype(BF16)


def _wkv_prepare(r, k, v, kkn, a, ld, cum, head_masks):
    bd = functools.partial(_block_diag, head_masks=head_masks)
    cend = cum[CHUNK - 1:CHUNK, :]
    w_prev = jnp.exp(cum - ld)
    w_t = jnp.exp(cum)
    w_inv = 1.0 / w_t
    w_rest = jnp.exp(cend - cum)
    b = kkn * a
    return dict(
        lhs2=jnp.concatenate([-kkn * w_prev, r * w_t], axis=0).astype(BF16),
        rhs=jnp.concatenate([bd(b * w_inv), bd(k * w_inv)], axis=0),
        bd_v=bd(v), v=v, bk_h=jnp.concatenate([b * w_rest, k * w_rest], axis=0).astype(BF16),
        decay=jnp.exp(cend))


def _wkv_recurrence(ops, get_state, head_masks, causal2, bd_mask, out):
    groups = range(len(ops))
    bd = functools.partial(_block_diag, head_masks=head_masks)
    n_steps = CHUNK.bit_length() - 1
    s_prev = get_state()
    g_b, g_k, g_s = [], [], []
    for g in groups:
        gram = _mm1(ops[g]['lhs2'],
                    jnp.concatenate([ops[g]['rhs'], s_prev[g].astype(BF16)], axis=0), _NT)
        g_b.append(jnp.where(causal2, gram[:, :LANES], 0.0))
        g_k.append(jnp.where(causal2, gram[:, LANES:2 * LANES], 0.0))
        g_s.append(gram[:, 2 * LANES:])
    yield
    u = [g_s[g][:CHUNK] + _mm1(g_k[g][:CHUNK], ops[g]['bd_v']) for g in groups]
    pw = [g_b[g][:CHUNK] for g in groups]
    yield
    for i in range(n_steps):
        for g in groups:
            if i + 1 < n_steps:
                res = _mm1(pw[g], jnp.concatenate([bd(u[g]), bd(pw[g])], axis=1))
                u[g] = u[g] + res[:, :LANES]
                pw[g] = res[:, LANES:]
            else:
                u[g] = u[g] + _mm1(pw[g], bd(u[g]))
        yield
    y = []
    for g in groups:
        l_r = jnp.concatenate([g_b[g][CHUNK:], g_k[g][CHUNK:]], axis=1)
        y.append(g_s[g][CHUNK:] + _mm1(l_r, jnp.concatenate([bd(u[g]), ops[g]['bd_v']], axis=0)))
    out['y'] = y
    yield
    s_new = []
    for g in groups:
        uv_t = jnp.concatenate([u[g], ops[g]['v']], axis=0).T
        upd = _mm1(uv_t, ops[g]['bk_h'])
        s_new.append(s_prev[g] * ops[g]['decay'] + jnp.where(bd_mask, upd, 0.0))
    out['s_new'] = s_new
    yield


WKV_STAGES = 4 + CHUNK.bit_length() - 1


def _run_tasks(tasks):
    live = list(tasks)
    rnd = 0
    while live:
        for task in list(live):
            if task[0] <= rnd:
                try:
                    next(task[1])
                except StopIteration:
                    live.remove(task)
        rnd += 1


def _mixer_kernel(p_ref, conv0_ref, shift0_ref, wkv0_ref, convw_ref, mu_ref, vec_ref, wwa_ref,
                  g2_ref, merged_ref, nconv_ref, nshift_ref, nwkv_ref, cbuf, sbuf, state,
                  *, n_steps, n_seq, n_ch):
    t = pl.program_id(1)
    seqs = range(n_seq)
    groups = range(N_GROUPS)
    pairs = [(s, g) for s in seqs for g in groups]
    n_pairs = len(pairs)
    rows = n_ch * CHUNK

    @pl.when(t == 0)
    def _():
        zero = jnp.zeros((HEAD_DIM, HEAD_DIM), F32)
        for s in seqs:
            cbuf[s, 0:SUBLANES, :] = jnp.zeros((SUBLANES, D_CONV), F32)
            cbuf[s, SUBLANES - (CONV_W - 1):SUBLANES, :] = conv0_ref[s]
            sbuf[s, 0:SUBLANES, :] = jnp.zeros((SUBLANES, D_SHIFT), F32)
            sbuf[s, SUBLANES - 1:SUBLANES, :] = shift0_ref[s]
            for g in groups:
                h0 = g * HEADS_PER_GROUP
                state[s * N_GROUPS + g] = jnp.concatenate(
                    [jnp.concatenate([wkv0_ref[s, h0 + h] if hh == h else zero
                                      for hh in range(HEADS_PER_GROUP)], axis=1)
                     for h in range(HEADS_PER_GROUP)], axis=0)

    w0, a0, k_k, k_a, r_k, lnx_w, lnx_b = (vec_ref[i:i + 1, :] for i in range(7))
    lane = lax.broadcasted_iota(jnp.int32, (CHUNK, LANES), 1)
    lane_r = lax.broadcasted_iota(jnp.int32, (rows, LANES), 1)
    row = lax.broadcasted_iota(jnp.int32, (rows, rows), 0)
    col = lax.broadcasted_iota(jnp.int32, (rows, rows), 1)
    chunk_shift = CHUNK.bit_length() - 1
    tri = jnp.where((col <= row) & ((col >> chunk_shift) == (row >> chunk_shift)),
                    1.0, 0.0).astype(BF16)
    head_shift = HEAD_DIM.bit_length() - 1
    head_masks = [(lane >> head_shift) == h for h in range(HEADS_PER_GROUP)]
    row2 = lax.broadcasted_iota(jnp.int32, (2 * CHUNK, LANES), 0)
    lane2 = lax.broadcasted_iota(jnp.int32, (2 * CHUNK, LANES), 1)
    causal2 = (lane2 & (HEAD_DIM - 1)) < (row2 & (CHUNK - 1)) + (row2 >> chunk_shift)
    rowl = lax.broadcasted_iota(jnp.int32, (LANES, LANES), 0)
    lanel = lax.broadcasted_iota(jnp.int32, (LANES, LANES), 1)
    bd_mask = (rowl >> head_shift) == (lanel >> head_shift)
    seg_ones = jnp.where(bd_mask, 1.0, 0.0).astype(BF16)
    sls = [slice(g * LANES, (g + 1) * LANES) for g in groups]

    c0 = SUBLANES
    y_a, gate, new_conv, new_shift = [], [], [], []
    r, k, v, a, ld, cum = [], [], [], [], [], []
    for s in seqs:
        ch = p_ref[s, :, _CC:_CC + D_CONV] * p_ref[s, :, _CHH:_CHH + D_CONV]
        cbuf[s, c0:c0 + rows, :] = ch
        conv = (cbuf[s, c0 - 2:c0 - 2 + rows, :] * convw_ref[0:1, :]
                + cbuf[s, c0 - 1:c0 - 1 + rows, :] * convw_ref[1:2, :]
                + ch * convw_ref[2:3, :])
        y_a.append(jax.nn.sigmoid(p_ref[s, :, _ZA:_ZA + D_MODEL])
                   * (p_ref[s, :, _CB:_CB + D_CONV] * conv))
        new_conv.append(cbuf[s, c0 + rows - (CONV_W - 1):c0 + rows, :])
        cbuf[s, 0:SUBLANES, :] = cbuf[s, rows:rows + SUBLANES, :]

        ps = p_ref[s, :, _PS:_PS + D_SHIFT]
        sbuf[s, c0:c0 + rows, :] = ps
        prev = sbuf[s, c0 - 1:c0 - 1 + rows, :]
        xm = ps + (prev - ps) * mu_ref[...]
        new_shift.append(sbuf[s, c0 + rows - 1:c0 + rows, :])
        sbuf[s, 0:SUBLANES, :] = sbuf[s, rows:rows + SUBLANES, :]

        wa_in = xm[:, _WA:_WA + LANES]
        wa_in = jnp.where(lane_r < DECAY_LORA, jnp.tanh(wa_in), wa_in)
        wa = _mm1(wa_in, wwa_ref[...])
        gate.append(_mm1(jax.nn.sigmoid(xm[:, _G:_G + GATE_LORA]), g2_ref[...]))
        ld_all = -jnp.exp(-0.5) * jax.nn.sigmoid(w0 + wa[:, :D_RWKV])
        a_all = jax.nn.sigmoid(a0 + wa[:, D_RWKV:])
        ld1, ld2 = _split(ld_all)
        cum_all = _dot(tri, ld1) + _dot(tri, ld2)
        r.append(xm[:, _R:_R + D_RWKV])
        k.append(xm[:, _K:_K + D_RWKV])
        v.append(xm[:, _V:_V + D_RWKV])
        a.append(a_all)
        ld.append(ld_all)
        cum.append(cum_all)

    states = [[state[i] for i in range(n_pairs)]]
    pieces = [list(range(j, n_pairs, PIECES)) for j in range(PIECES)]
    ops = [[None] * n_pairs for _ in range(n_ch)]
    bonus = [[None] * n_pairs for _ in range(n_ch)]
    results = [dict() for _ in range(n_ch)]

    def prepare(c, idx):
        rs = slice(c * CHUNK, (c + 1) * CHUNK)
        cut = lambda xs, i: xs[pairs[i][0]][rs, sls[pairs[i][1]]]
        kk = [cut(k, i) * k_k[:, sls[pairs[i][1]]] for i in idx]
        kf = [cut(k, i) * (1.0 + (cut(a, i) - 1.0) * k_a[:, sls[pairs[i][1]]]) for i in idx]
        sums = _segment_sums([x * x for x in kk]
                             + [cut(r, i) * kf[n] * r_k[:, sls[pairs[i][1]]]
                                for n, i in enumerate(idx)], seg_ones, exact=False)
        yield
        for n, i in enumerate(idx):
            kkn = kk[n] / jnp.maximum(jnp.sqrt(sums[n]), L2_EPS)
            bonus[c][i] = sums[len(idx) + n] * cut(v, i)
            ops[c][i] = _wkv_prepare(cut(r, i), kf[n], cut(v, i), kkn, cut(a, i), cut(ld, i),
                                     cut(cum, i), head_masks)

    def recur(c):
        yield from _wkv_recurrence(ops[c], lambda: states[c], head_masks, causal2, bd_mask,
                                   results[c])
        states.append(results[c]['s_new'])

    def finish(c, idx):
        rs = slice(c * CHUNK, (c + 1) * CHUNK)
        y = [results[c]['y'][i] for i in idx]
        mean = [m * (1.0 / HEAD_DIM) for m in _segment_sums(y, seg_ones)]
        yield
        dev = [y[n] - mean[n] for n in range(len(idx))]
        var = [m * (1.0 / HEAD_DIM) for m in _segment_sums([d * d for d in dev], seg_ones)]
        yield
        for n, i in enumerate(idx):
            s, g = pairs[i]
            sl = sls[g]
            yn = dev[n] * lax.rsqrt(var[n] + GN_EPS) * lnx_w[:, sl] + lnx_b[:, sl]
            y_b = (yn + bonus[c][i]) * gate[s][rs, sl]
            z_b = p_ref[s, rs, _ZB + g * LANES:_ZB + (g + 1) * LANES]
            merged_ref[s, rs, sl] = (y_a[s][rs, sl] + jax.nn.sigmoid(z_b) * y_b).astype(merged_ref.dtype)

    base = lambda c: 2 + WKV_STAGES * c
    tasks = []
    for c in range(n_ch):
        for j, idx in enumerate(pieces):
            tasks.append((0 if c == 0 else base(c - 1) + 2 * j, prepare(c, idx)))
    for c in range(n_ch):
        tasks.append((base(c), recur(c)))
        for j, idx in enumerate(pieces):
            tasks.append((base(c + 1) + (2 * j if c + 1 < n_ch else 0), finish(c, idx)))
    _run_tasks(sorted(tasks, key=lambda task: task[0]))
    for i in range(n_pairs):
        state[i] = states[n_ch][i]

    @pl.when(t == n_steps - 1)
    def _():
        for s in seqs:
            nconv_ref[s] = new_conv[s]
            nshift_ref[s] = new_shift[s]
            for h in range(N_HEADS):
                o = (h % HEADS_PER_GROUP) * HEAD_DIM
                nwkv_ref[s, h] = state[s * N_GROUPS + h // HEADS_PER_GROUP,
                                       o:o + HEAD_DIM, o:o + HEAD_DIM]


def _mixer(p2d, conv0, shift0, wkv0, convw, mu, vecs, wwa, g2, batch, seq):
    n_seq = SEQS_PER_STEP if batch % SEQS_PER_STEP == 0 else 1
    n_ch = CHUNKS_PER_STEP if seq % (CHUNKS_PER_STEP * CHUNK) == 0 else 1
    rows = n_ch * CHUNK
    n_steps = seq // rows
    kern = functools.partial(_mixer_kernel, n_steps=n_steps, n_seq=n_seq, n_ch=n_ch)
    const = lambda b, t: (0, 0)
    per_b3 = lambda b, t: (b, 0, 0)
    per_b4 = lambda b, t: (b, 0, 0, 0)
    merged, conv, shift, wkv = pl.pallas_call(
        kern,
        out_shape=(jax.ShapeDtypeStruct((batch, seq, D_MODEL), BF16),
                   jax.ShapeDtypeStruct((batch, CONV_W - 1, D_CONV), F32),
                   jax.ShapeDtypeStruct((batch, 1, D_SHIFT), F32),
                   jax.ShapeDtypeStruct((batch, N_HEADS, HEAD_DIM, HEAD_DIM), F32)),
        grid=(batch // n_seq, n_steps),
        in_specs=[pl.BlockSpec((n_seq, rows, D_PROJ), lambda b, t: (b, t, 0)),
                  pl.BlockSpec((n_seq, CONV_W - 1, D_CONV), per_b3),
                  pl.BlockSpec((n_seq, 1, D_SHIFT), per_b3),
                  pl.BlockSpec((n_seq, N_HEADS, HEAD_DIM, HEAD_DIM), per_b4),
                  pl.BlockSpec((CONV_W, D_CONV), const),
                  pl.BlockSpec((1, D_SHIFT), const),
                  pl.BlockSpec((SUBLANES, D_RWKV), const),
                  pl.BlockSpec((LANES, 2 * D_RWKV), const),
                  pl.BlockSpec((GATE_LORA, D_RWKV), const)],
        out_specs=(pl.BlockSpec((n_seq, rows, D_MODEL), lambda b, t: (b, t, 0)),
                   pl.BlockSpec((n_seq, CONV_W - 1, D_CONV), per_b3),
                   pl.BlockSpec((n_seq, 1, D_SHIFT), per_b3),
                   pl.BlockSpec((n_seq, N_HEADS, HEAD_DIM, HEAD_DIM), per_b4)),
        scratch_shapes=[pltpu.VMEM((n_seq, rows + SUBLANES, D_CONV), F32),
                        pltpu.VMEM((n_seq, rows + SUBLANES, D_SHIFT), F32),
                        pltpu.VMEM((n_seq * N_GROUPS, LANES, LANES), F32)],
        compiler_params=_params(("arbitrary", "arbitrary")),
        name="mixer",
    )(p2d.reshape(batch, seq, D_PROJ), conv0, shift0, wkv0, convw, mu, vecs, wwa, g2)
    return merged.reshape(batch * seq, D_MODEL), conv, shift, wkv


def _silu_mul(gate, up):
    return gate * jax.nn.sigmoid(gate) * up


def _ffn_dense_kernel(x_ref, m_ref, wo_ref, g_ref, wg_ref, wu_ref, wd_ref, o_ref, *, n_f):
    x1 = x_ref[...] + _dot(m_ref[...], wo_ref[...])
    h = _rms(x1, g_ref[...]).astype(BF16)
    tf = wg_ref.shape[1] // n_f
    acc = x1
    for j in range(n_f):
        act = _silu_mul(_dot(h, wg_ref[:, j * tf:(j + 1) * tf]),
                        _dot(h, wu_ref[:, j * tf:(j + 1) * tf])).astype(BF16)
        acc = acc + _dot(act, wd_ref[j * tf:(j + 1) * tf, :])
    o_ref[...] = acc


def _ffn_dense(x2d, merged, wo, g, wg, wu, wd, tm, n_f):
    n, d = x2d.shape
    f = wg.shape[1]
    row = lambda i: (i, 0)
    resident = lambda shape: pl.BlockSpec(shape, lambda i: (0, 0), pipeline_mode=pl.Buffered(1))
    return pl.pallas_call(
        functools.partial(_ffn_dense_kernel, n_f=n_f),
        out_shape=jax.ShapeDtypeStruct((n, d), F32),
        grid=(n // tm,),
        in_specs=[pl.BlockSpec((tm, d), row), pl.BlockSpec((tm, d), row),
                  resident((d, d)), pl.BlockSpec((1, d), lambda i: (0, 0)),
                  resident((d, f)), resident((d, f)), resident((f, d))],
        out_specs=pl.BlockSpec((tm, d), row),
        compiler_params=_params(("arbitrary",)),
        name="ffn_dense",
    )(x2d, merged, wo, g, wg, wu, wd)


def _top2(logits):
    lane_i = lax.broadcasted_iota(jnp.int32, logits.shape, 1)
    lane = lane_i.astype(F32)
    neg = jnp.float32(-jnp.inf)
    lg = jnp.where(lane_i < N_EXPERTS, logits, neg)
    m1 = jnp.max(lg, axis=-1, keepdims=True)
    i1 = jnp.min(jnp.where(lg == m1, lane, float(LANES)), axis=-1, keepdims=True)
    lg2 = jnp.where(lane == i1, neg, lg)
    m2 = jnp.max(lg2, axis=-1, keepdims=True)
    i2 = jnp.min(jnp.where(lg2 == m2, lane, float(LANES)), axis=-1, keepdims=True)
    e2 = jnp.exp(m2 - m1)
    den = 1.0 + e2
    ids = jnp.where(lane_i == 0, i1, jnp.where(lane_i == 1, i2, 0.0)).astype(jnp.int32)
    wts = jnp.where(lane_i == 0, 1.0 / den, jnp.where(lane_i == 1, e2 / den, 0.0))
    return ids, wts


def _router_kernel(x_ref, m_ref, wo_ref, g_ref, rw_ref, x1_ref, ids_ref, wts_ref):
    x1 = x_ref[...] + _dot(m_ref[...], wo_ref[...])
    x1_ref[...] = x1
    ids_ref[...], wts_ref[...] = _top2(_mm3(_rms(x1, g_ref[...]), rw_ref[...]))


def _router(x2d, merged, wo, g, rw, tm):
    n, d = x2d.shape
    row = lambda i: (i, 0)
    const = lambda i: (0, 0)
    return pl.pallas_call(
        _router_kernel,
        out_shape=(jax.ShapeDtypeStruct((n, d), F32),
                   jax.ShapeDtypeStruct((n, LANES), jnp.int32),
                   jax.ShapeDtypeStruct((n, LANES), F32)),
        grid=(n // tm,),
        in_specs=[pl.BlockSpec((tm, d), row), pl.BlockSpec((tm, d), row),
                  pl.BlockSpec((d, d), const), pl.BlockSpec((1, d), const),
                  pl.BlockSpec((d, LANES), const)],
        out_specs=(pl.BlockSpec((tm, d), row), pl.BlockSpec((tm, LANES), row),
                   pl.BlockSpec((tm, LANES), row)),
        compiler_params=_params(("arbitrary",)),
        name="router",
    )(x2d, merged, wo, g, rw)


def _experts_kernel(te_ref, nv_ref, inv_prev_ref, inv_ref, inv_next_ref, x1_hbm, g_ref,
                    wg_ref, wu_ref, wd_ref, y_hbm, xbuf, obuf, gather_sem, scatter_sem,
                    *, tm, n_tok, f_split):
    t = pl.program_id(0)
    n_used = nv_ref[0]
    cur = t & 1
    nxt = 1 - cur

    def gather_row(idx_ref, buf, r):
        tok = jnp.maximum(idx_ref[0, 0, r], 0) >> 1
        pltpu.make_async_copy(x1_hbm.at[pl.ds(tok, 1)], xbuf.at[buf, pl.ds(r, 1)],
                              gather_sem.at[buf]).start()

    def gather_loop(idx_ref, buf):
        def body(r, c):
            gather_row(idx_ref, buf, r)
            return c
        lax.fori_loop(0, tm, body, 0, unroll=8)

    def wait_gather(buf):
        pltpu.make_async_copy(x1_hbm.at[pl.ds(0, tm)], xbuf.at[buf], gather_sem.at[buf]).wait()

    def scatter_row(idx_ref, buf, r, real):
        a = idx_ref[0, 0, r]
        row = jnp.where((a >= 0) & real, (a & 1) * n_tok + (a >> 1), 2 * n_tok + r)
        pltpu.make_async_copy(obuf.at[buf, pl.ds(r, 1)], y_hbm.at[pl.ds(row, 1)], scatter_sem).start()

    def wait_scatter(buf):
        pltpu.make_async_copy(obuf.at[buf], y_hbm.at[pl.ds(0, tm)], scatter_sem).wait()

    @pl.when(t == 0)
    def _():
        obuf[...] = jnp.zeros_like(obuf)
        gather_loop(inv_ref, 0)

    wait_gather(cur)

    @pl.when(t < n_used)
    def _():
        h = _rms(xbuf[cur], g_ref[...]).astype(BF16)
        fs = wg_ref.shape[2] // f_split
        rows_per = -(-tm // max(1, (2 * f_split) // 3))
        acc = None
        for i in range(f_split):
            act = _silu_mul(_dot(h, wg_ref[0, :, i * fs:(i + 1) * fs]),
                            _dot(h, wu_ref[0, :, i * fs:(i + 1) * fs])).astype(BF16)
            part = _dot(act, wd_ref[0, i * fs:(i + 1) * fs, :])
            acc = part if acc is None else acc + part
            for r in range(i * rows_per, min((i + 1) * rows_per, tm)):
                gather_row(inv_next_ref, nxt, r)
                scatter_row(inv_prev_ref, nxt, r, t > 0)
        wait_scatter(nxt)
        obuf[cur] = acc

        @pl.when(t == n_used - 1)
        def _():
            def body(r, c):
                scatter_row(inv_ref, cur, r, True)
                return c
            lax.fori_loop(0, tm, body, 0, unroll=8)
            wait_scatter(cur)

    @pl.when(t >= n_used)
    def _():
        gather_loop(inv_next_ref, nxt)

    @pl.when(t == pl.num_programs(0) - 1)
    def _():
        wait_gather(nxt)


def _experts(x1, g, wg, wu, wd, tile_expert, n_used, inv, tm):
    n, d = x1.shape
    n_tiles = tile_expert.shape[0]
    f = wg.shape[2]
    expert_w = lambda t, te, nv: (jnp.minimum(te[t], N_EXPERTS - 1), 0, 0)
    single = pl.Buffered(1)
    inv3 = inv.reshape(n_tiles, 1, tm)
    idx_spec = lambda shift: pl.BlockSpec(
        (1, 1, tm), lambda t, te, nv: (jnp.clip(t + shift, 0, n_tiles - 1), 0, 0),
        memory_space=pltpu.SMEM)
    grid_spec = pltpu.PrefetchScalarGridSpec(
        num_scalar_prefetch=2,
        grid=(n_tiles,),
        in_specs=[idx_spec(-1), idx_spec(0), idx_spec(1),
                  pl.BlockSpec(memory_space=pl.ANY),
                  pl.BlockSpec((1, d), lambda t, te, nv: (0, 0)),
                  pl.BlockSpec((1, d, f), expert_w, pipeline_mode=single),
                  pl.BlockSpec((1, d, f), expert_w, pipeline_mode=single),
                  pl.BlockSpec((1, f, d), expert_w, pipeline_mode=single)],
        out_specs=pl.BlockSpec(memory_space=pl.ANY),
        scratch_shapes=[pltpu.VMEM((2, tm, d), F32), pltpu.VMEM((2, tm, d), F32),
                        pltpu.SemaphoreType.DMA((2,)), pltpu.SemaphoreType.DMA(())])
    return pl.pallas_call(
        functools.partial(_experts_kernel, tm=tm, n_tok=n, f_split=f // (2 * LANES)),
        out_shape=jax.ShapeDtypeStruct((2 * n + tm, d), F32),
        grid_spec=grid_spec,
        compiler_params=_params(("arbitrary",)),
        name="experts",
    )(tile_expert, n_used, inv3, inv3, inv3, x1, g, wg, wu, wd)


def _combine_kernel(x1_ref, y0_ref, y1_ref, wts_ref, fin_ref, o_ref):
    w = wts_ref[...]
    moe = w[:, 0:1] * y0_ref[...] + w[:, 1:2] * y1_ref[...]
    o_ref[...] = _rms(x1_ref[...] + moe, fin_ref[...])


def _combine(x1, y2, wts, fin, tm):
    n, d = x1.shape
    n_blocks = n // tm
    return pl.pallas_call(
        _combine_kernel,
        out_shape=jax.ShapeDtypeStruct((n, d), F32),
        grid=(n_blocks,),
        in_specs=[pl.BlockSpec((tm, d), lambda i: (i, 0)),
                  pl.BlockSpec((tm, d), lambda i: (i, 0)),
                  pl.BlockSpec((tm, d), lambda i: (i + n_blocks, 0)),
                  pl.BlockSpec((tm, LANES), lambda i: (i, 0)),
                  pl.BlockSpec((1, d), lambda i: (0, 0))],
        out_specs=pl.BlockSpec((tm, d), lambda i: (i, 0)),
        compiler_params=_params(("arbitrary",)),
        name="combine",
    )(x1, y2, y2, wts, fin)


def _routing_plan(ids, tm):
    n = ids.shape[0]
    e_flat = ids[:, :2].reshape(-1)
    onehot = (e_flat[:, None] == jnp.arange(N_EXPERTS, dtype=jnp.int32)[None, :]).astype(jnp.int32)
    csum = jnp.cumsum(onehot, axis=0)
    counts = csum[-1]
    padded = (counts + tm - 1) // tm * tm
    ends = jnp.cumsum(padded)
    pos = jnp.sum(onehot * (csum - 1 + (ends - padded)[None, :]), axis=1)
    total = 2 * n + N_EXPERTS * tm
    inv = jnp.full((total,), -1, jnp.int32).at[pos].set(jnp.arange(2 * n, dtype=jnp.int32))
    tile_start = jnp.arange(total // tm, dtype=jnp.int32) * tm
    tile_expert = jnp.sum((tile_start[:, None] >= ends[None, :]).astype(jnp.int32), axis=1)
    return tile_expert, (ends[-1:] // tm).astype(jnp.int32), inv


def _ffn_moe(x2d, merged, wo, g, rw, wg, wu, wd, fin, tm, tm_router, tm_expert):
    x1, ids, wts = _router(x2d, merged, wo, g, rw, tm_router)
    tile_expert, n_used, inv = _routing_plan(ids, tm_expert)
    y2 = _experts(x1, g, wg, wu, wd, tile_expert, n_used, inv, tm_expert)
    return _combine(x1, y2, wts, fin, tm)


def _prep_layer_weights(l, w_in, conv_w, shift_mu, decay_w0, decay_w2, aaa_a0, aaa_a2, gate_g2,
                        key_k, key_a, bonus_r_k, lnx_w, lnx_b, w_out):
    zero = jnp.zeros((DECAY_LORA, D_RWKV), F32)
    wwa = jnp.concatenate([jnp.concatenate([decay_w2[l], zero], axis=1),
                           jnp.concatenate([zero, aaa_a2[l]], axis=1)], axis=0).astype(BF16)
    vecs = jnp.stack([decay_w0[l], aaa_a0[l], key_k[l], key_a[l], bonus_r_k[l], lnx_w[l], lnx_b[l],
                      jnp.zeros((D_RWKV,), F32)])
    return dict(w_in=w_in[l].astype(BF16), convw=conv_w[l], mu=shift_mu[l][None], vecs=vecs,
                wwa=wwa, g2=gate_g2[l].astype(BF16), w_out=w_out[l].astype(BF16))


def _row_tile(n, want):
    return want if n % want == 0 else n


def _tiles(n):
    big_experts = 2 * n >= 4 * 512 * N_EXPERTS
    return (_row_tile(n, 512), _row_tile(n, 256), _row_tile(n, 1024),
            _row_tile(2 * n, 512 if big_experts else 256))


def _trunk(x, conv_st, shift_st, wkv_st, layers, ffn_norm, dense, moe, final_norm):
    batch, seq, d = x.shape
    n = batch * seq
    x2d = x.reshape(n, d)
    tm, tm_proj, tm_router, tm_expert = _tiles(n)
    convs, shifts, wkvs = [], [], []
    for l, lw in enumerate(layers):
        p = _norm_proj(x2d, lw['norm'], lw['w_in'], tm_proj, 6)
        merged, c, s, w = _mixer(p, conv_st[l], shift_st[l][:, None], wkv_st[l],
                                 lw['convw'], lw['mu'], lw['vecs'], lw['wwa'], lw['g2'],
                                 batch, seq)
        convs.append(c)
        shifts.append(s[:, 0])
        wkvs.append(w)
        if l % 2 == 0:
            wg, wu, wd = dense
            x2d = _ffn_dense(x2d, merged, lw['w_out'], ffn_norm[l][None], wg, wu, wd, tm, 2)
        else:
            rw, wg, wu, wd = moe
            x2d = _ffn_moe(x2d, merged, lw['w_out'], ffn_norm[l][None], rw, wg, wu, wd,
                           final_norm[None], tm, tm_router, tm_expert)
    return x2d.reshape(batch, seq, d), jnp.stack(convs), jnp.stack(shifts), jnp.stack(wkvs)


def kernel(x_prompt, x_sample, state_conv, state_shift, state_wkv, mix_norm, w_in, conv_w, shift_mu,
           decay_w0, decay_w2, aaa_a0, aaa_a2, gate_g2, key_k, key_a, bonus_r_k, lnx_w, lnx_b, w_out,
           ffn_norm, ffn_w_gate, ffn_w_up, ffn_w_down, router_w, moe_w_gate, moe_w_up, moe_w_down,
           final_norm):
    depth = w_in.shape[0]
    assert depth == 2 and ffn_w_gate.shape[0] == 1 and moe_w_gate.shape[0] == 1
    layers = []
    for l in range(depth):
        lw = _prep_layer_weights(l, w_in, conv_w, shift_mu, decay_w0, decay_w2, aaa_a0, aaa_a2,
                                 gate_g2, key_k, key_a, bonus_r_k, lnx_w, lnx_b, w_out)
        lw['norm'] = mix_norm[l][None]
        layers.append(lw)
    dense = (ffn_w_gate[0].astype(BF16), ffn_w_up[0].astype(BF16), ffn_w_down[0].astype(BF16))
    rw = jnp.pad(router_w[0], ((0, 0), (0, LANES - N_EXPERTS)))
    moe = (rw, moe_w_gate[0].astype(BF16), moe_w_up[0].astype(BF16), moe_w_down[0].astype(BF16))

    b = x_prompt.shape[0]
    zero_conv = jnp.zeros((depth, b) + state_conv.shape[2:], state_conv.dtype)
    zero_shift = jnp.zeros((depth, b) + state_shift.shape[2:], state_shift.dtype)
    zero_wkv = jnp.zeros((depth, b) + state_wkv.shape[2:], state_wkv.dtype)
    run = functools.partial(_trunk, layers=layers, ffn_norm=ffn_norm, dense=dense, moe=moe,
                            final_norm=final_norm)
    y_s, conv_s, shift_s, wkv_s = run(x_sample, state_conv, state_shift, state_wkv)
    y_p, conv_p, shift_p, wkv_p = run(x_prompt, zero_conv, zero_shift, zero_wkv)
    return (y_p, y_s, conv_p, shift_p, wkv_p, conv_s, shift_s, wkv_s)
```

```python
import functools

import jax
import jax.numpy as jnp
from jax import lax
from jax.experimental import pallas as pl
from jax.experimental.pallas import tpu as pltpu

F32 = jnp.float32
BF16 = jnp.bfloat16

D_MODEL = 1024
N_HEADS = 16
HEAD_DIM = 64
D_RWKV = N_HEADS * HEAD_DIM
D_CONV = 1024
CONV_W = 3
DECAY_LORA = 64
AAA_LORA = 64
GATE_LORA = 128
D_SHIFT = 3 * D_RWKV + DECAY_LORA + AAA_LORA + GATE_LORA
D_PROJ = 2 * D_MODEL + 3 * D_CONV + D_SHIFT
N_EXPERTS = 8
RMS_EPS = 1e-5
GN_EPS = 64e-5
L2_EPS = 1e-12

LANES = 128
SUBLANES = 8
CHUNK = 64
HEADS_PER_GROUP = LANES // HEAD_DIM
SEQS_PER_STEP = 2
CHUNKS_PER_STEP = 2
PIECES = 4
N_GROUPS = N_HEADS // HEADS_PER_GROUP
VMEM_LIMIT = 56 * 1024 * 1024

_ZA, _ZB, _CB, _CC, _CHH, _PS = (0, D_MODEL, 2 * D_MODEL, 2 * D_MODEL + D_CONV,
                                 2 * D_MODEL + 2 * D_CONV, 2 * D_MODEL + 3 * D_CONV)
_R, _K, _V, _WA, _G = 0, D_RWKV, 2 * D_RWKV, 3 * D_RWKV, 3 * D_RWKV + DECAY_LORA + AAA_LORA


def _params(semantics):
    return pltpu.CompilerParams(dimension_semantics=semantics, vmem_limit_bytes=VMEM_LIMIT)


def _rms(x, g):
    ms = jnp.mean(x * x, axis=-1, keepdims=True)
    return x * lax.rsqrt(ms + RMS_EPS) * g


def _split(x):
    hi = x.astype(BF16)
    lo = (x - hi.astype(F32)).astype(BF16)
    return hi, lo


_NN = (((1,), (0,)), ((), ()))
_NT = (((1,), (1,)), ((), ()))


def _dot(a, b, dims=_NN):
    return lax.dot_general(a, b, dims, preferred_element_type=F32)


def _mm1(a, b, dims=_NN):
    return _dot(a.astype(BF16), b.astype(BF16), dims)


def _mm3(a, b, dims=_NN):
    a1, a2 = _split(a)
    b1, b2 = _split(b)
    return _dot(a1, b1, dims) + (_dot(a1, b2, dims) + _dot(a2, b1, dims))


def _segment_sums(xs, seg_ones, exact=True):
    rows = xs[0].shape[0]
    if not exact:
        out = _dot(jnp.concatenate([x.astype(BF16) for x in xs], axis=0), seg_ones)
        return [out[i * rows:(i + 1) * rows] for i in range(len(xs))]
    parts = [half for x in xs for half in _split(x)]
    out = _dot(jnp.concatenate(parts, axis=0), seg_ones)
    return [out[2 * i * rows:(2 * i + 1) * rows] + out[(2 * i + 1) * rows:(2 * i + 2) * rows]
            for i in range(len(xs))]


def _norm_proj_kernel(x_ref, g_ref, w_ref, o_ref, *, n_col):
    xn = _rms(x_ref[...], g_ref[...]).astype(BF16)
    tn = w_ref.shape[1] // n_col
    for j in range(n_col):
        o_ref[:, j * tn:(j + 1) * tn] = _dot(xn, w_ref[:, j * tn:(j + 1) * tn])


def _norm_proj(x2d, g, w_bf16, tm, n_col):
    n, d = x2d.shape
    dp = w_bf16.shape[1]
    return pl.pallas_call(
        functools.partial(_norm_proj_kernel, n_col=n_col),
        out_shape=jax.ShapeDtypeStruct((n, dp), F32),
        grid=(n // tm,),
        in_specs=[pl.BlockSpec((tm, d), lambda i: (i, 0)),
                  pl.BlockSpec((1, d), lambda i: (0, 0)),
                  pl.BlockSpec((d, dp), lambda i: (0, 0), pipeline_mode=pl.Buffered(1))],
        out_specs=pl.BlockSpec((tm, dp), lambda i: (i, 0)),
        compiler_params=_params(("arbitrary",)),
        name="norm_proj",
    )(x2d, g, w_bf16)


def _block_diag(x, head_masks):
    return jnp.concatenate([jnp.where(m, x, 0.0) for m in head_masks], axis=0).astype(BF16)


def _wkv_prepare(r, k, v, kkn, a, ld, cum, head_masks):
    bd = functools.partial(_block_diag, head_masks=head_masks)
    cend = cum[CHUNK - 1:CHUNK, :]
    w_prev = jnp.exp(cum - ld)
    w_t = jnp.exp(cum)
    w_inv = 1.0 / w_t
    w_rest = jnp.exp(cend - cum)
    b = kkn * a
    return dict(
        lhs2=jnp.concatenate([-kkn * w_prev, r * w_t], axis=0).astype(BF16),
        rhs=jnp.concatenate([bd(b * w_inv), bd(k * w_inv)], axis=0),
        bd_v=bd(v), v=v, bk_h=jnp.concatenate([b * w_rest, k * w_rest], axis=0).astype(BF16),
        decay=jnp.exp(cend))


def _wkv_recurrence(ops, get_state, head_masks, causal2, bd_mask, out):
    groups = range(len(ops))
    bd = functools.partial(_block_diag, head_masks=head_masks)
    n_steps = CHUNK.bit_length() - 1
    s_prev = get_state()
    g_b, g_k, g_s = [], [], []
    for g in groups:
        gram = _mm1(ops[g]['lhs2'],
                    jnp.concatenate([ops[g]['rhs'], s_prev[g].astype(BF16)], axis=0), _NT)
        g_b.append(jnp.where(causal2, gram[:, :LANES], 0.0))
        g_k.append(jnp.where(causal2, gram[:, LANES:2 * LANES], 0.0))
        g_s.append(gram[:, 2 * LANES:])
    yield
    u = [g_s[g][:CHUNK] + _mm1(g_k[g][:CHUNK], ops[g]['bd_v']) for g in groups]
    pw = [g_b[g][:CHUNK] for g in groups]
    yield
    for i in range(n_steps):
        for g in groups:
            if i + 1 < n_steps:
                res = _mm1(pw[g], jnp.concatenate([bd(u[g]), bd(pw[g])], axis=1))
                u[g] = u[g] + res[:, :LANES]
                pw[g] = res[:, LANES:]
            else:
                u[g] = u[g] + _mm1(pw[g], bd(u[g]))
        yield
    y = []
    for g in groups:
        l_r = jnp.concatenate([g_b[g][CHUNK:], g_k[g][CHUNK:]], axis=1)
        y.append(g_s[g][CHUNK:] + _mm1(l_r, jnp.concatenate([bd(u[g]), ops[g]['bd_v']], axis=0)))
    out['y'] = y
    yield
    s_new = []
    for g in groups:
        uv_t = jnp.concatenate([u[g], ops[g]['v']], axis=0).T
        upd = _mm1(uv_t, ops[g]['bk_h'])
        s_new.append(s_prev[g] * ops[g]['decay'] + jnp.where(bd_mask, upd, 0.0))
    out['s_new'] = s_new
    yield


WKV_STAGES = 4 + CHUNK.bit_length() - 1


def _run_tasks(tasks):
    live = list(tasks)
    rnd = 0
    while live:
        for task in list(live):
            if task[0] <= rnd:
                try:
                    next(task[1])
                except StopIteration:
                    live.remove(task)
        rnd += 1


def _mixer_kernel(p_ref, conv0_ref, shift0_ref, wkv0_ref, convw_ref, mu_ref, vec_ref, wwa_ref,
                  g2_ref, merged_ref, nconv_ref, nshift_ref, nwkv_ref, cbuf, sbuf, state,
                  *, n_steps, n_seq, n_ch):
    t = pl.program_id(1)
    seqs = range(n_seq)
    groups = range(N_GROUPS)
    pairs = [(s, g) for s in seqs for g in groups]
    n_pairs = len(pairs)
    rows = n_ch * CHUNK

    @pl.when(t == 0)
    def _():
        zero = jnp.zeros((HEAD_DIM, HEAD_DIM), F32)
        for s in seqs:
            cbuf[s, 0:SUBLANES, :] = jnp.zeros((SUBLANES, D_CONV), F32)
            cbuf[s, SUBLANES - (CONV_W - 1):SUBLANES, :] = conv0_ref[s]
            sbuf[s, 0:SUBLANES, :] = jnp.zeros((SUBLANES, D_SHIFT), F32)
            sbuf[s, SUBLANES - 1:SUBLANES, :] = shift0_ref[s]
            for g in groups:
                h0 = g * HEADS_PER_GROUP
                state[s * N_GROUPS + g] = jnp.concatenate(
                    [jnp.concatenate([wkv0_ref[s, h0 + h] if hh == h else zero
                                      for hh in range(HEADS_PER_GROUP)], axis=1)
                     for h in range(HEADS_PER_GROUP)], axis=0)

    w0, a0, k_k, k_a, r_k, lnx_w, lnx_b = (vec_ref[i:i + 1, :] for i in range(7))
    lane = lax.broadcasted_iota(jnp.int32, (CHUNK, LANES), 1)
    lane_r = lax.broadcasted_iota(jnp.int32, (rows, LANES), 1)
    row = lax.broadcasted_iota(jnp.int32, (rows, rows), 0)
    col = lax.broadcasted_iota(jnp.int32, (rows, rows), 1)
    chunk_shift = CHUNK.bit_length() - 1
    tri = jnp.where((col <= row) & ((col >> chunk_shift) == (row >> chunk_shift)),
                    1.0, 0.0).astype(BF16)
    head_shift = HEAD_DIM.bit_length() - 1
    head_masks = [(lane >> head_shift) == h for h in range(HEADS_PER_GROUP)]
    row2 = lax.broadcasted_iota(jnp.int32, (2 * CHUNK, LANES), 0)
    lane2 = lax.broadcasted_iota(jnp.int32, (2 * CHUNK, LANES), 1)
    causal2 = (lane2 & (HEAD_DIM - 1)) < (row2 & (CHUNK - 1)) + (row2 >> chunk_shift)
    rowl = lax.broadcasted_iota(jnp.int32, (LANES, LANES), 0)
    lanel = lax.broadcasted_iota(jnp.int32, (LANES, LANES), 1)
    bd_mask = (rowl >> head_shift) == (lanel >> head_shift)
    seg_ones = jnp.where(bd_mask, 1.0, 0.0).astype(BF16)
    sls = [slice(g * LANES, (g + 1) * LANES) for g in groups]

    c0 = SUBLANES
    y_a, gate, new_conv, new_shift = [], [], [], []
    r, k, v, a, ld, cum = [], [], [], [], [], []
    for s in seqs:
        ch = p_ref[s, :, _CC:_CC + D_CONV] * p_ref[s, :, _CHH:_CHH + D_CONV]
        cbuf[s, c0:c0 + rows, :] = ch
        conv = (cbuf[s, c0 - 2:c0 - 2 + rows, :] * convw_ref[0:1, :]
                + cbuf[s, c0 - 1:c0 - 1 + rows, :] * convw_ref[1:2, :]
                + ch * convw_ref[2:3, :])
        y_a.append(jax.nn.sigmoid(p_ref[s, :, _ZA:_ZA + D_MODEL])
                   * (p_ref[s, :, _CB:_CB + D_CONV] * conv))
        new_conv.append(cbuf[s, c0 + rows - (CONV_W - 1):c0 + rows, :])
        cbuf[s, 0:SUBLANES, :] = cbuf[s, rows:rows + SUBLANES, :]

        ps = p_ref[s, :, _PS:_PS + D_SHIFT]
        sbuf[s, c0:c0 + rows, :] = ps
        prev = sbuf[s, c0 - 1:c0 - 1 + rows, :]
        xm = ps + (prev - ps) * mu_ref[...]
        new_shift.append(sbuf[s, c0 + rows - 1:c0 + rows, :])
        sbuf[s, 0:SUBLANES, :] = sbuf[s, rows:rows + SUBLANES, :]

        wa_in = xm[:, _WA:_WA + LANES]
        wa_in = jnp.where(lane_r < DECAY_LORA, jnp.tanh(wa_in), wa_in)
        wa = _mm1(wa_in, wwa_ref[...])
        gate.append(_mm1(jax.nn.sigmoid(xm[:, _G:_G + GATE_LORA]), g2_ref[...]))
        ld_all = -jnp.exp(-0.5) * jax.nn.sigmoid(w0 + wa[:, :D_RWKV])
        a_all = jax.nn.sigmoid(a0 + wa[:, D_RWKV:])
        ld1, ld2 = _split(ld_all)
        cum_all = _dot(tri, ld1) + _dot(tri, ld2)
        r.append(xm[:, _R:_R + D_RWKV])
        k.append(xm[:, _K:_K + D_RWKV])
        v.append(xm[:, _V:_V + D_RWKV])
        a.append(a_all)
        ld.append(ld_all)
        cum.append(cum_all)

    states = [[state[i] for i in range(n_pairs)]]
    pieces = [list(range(j, n_pairs, PIECES)) for j in range(PIECES)]
    ops = [[None] * n_pairs for _ in range(n_ch)]
    bonus = [[None] * n_pairs for _ in range(n_ch)]
    results = [dict() for _ in range(n_ch)]

    def prepare(c, idx):
        rs = slice(c * CHUNK, (c + 1) * CHUNK)
        cut = lambda xs, i: xs[pairs[i][0]][rs, sls[pairs[i][1]]]
        kk = [cut(k, i) * k_k[:, sls[pairs[i][1]]] for i in idx]
        kf = [cut(k, i) * (1.0 + (cut(a, i) - 1.0) * k_a[:, sls[pairs[i][1]]]) for i in idx]
        sums = _segment_sums([x * x for x in kk]
                             + [cut(r, i) * kf[n] * r_k[:, sls[pairs[i][1]]]
                                for n, i in enumerate(idx)], seg_ones, exact=False)
        yield
        for n, i in enumerate(idx):
            kkn = kk[n] / jnp.maximum(jnp.sqrt(sums[n]), L2_EPS)
            bonus[c][i] = sums[len(idx) + n] * cut(v, i)
            ops[c][i] = _wkv_prepare(cut(r, i), kf[n], cut(v, i), kkn, cut(a, i), cut(ld, i),
                                     cut(cum, i), head_masks)

    def recur(c):
        yield from _wkv_recurrence(ops[c], lambda: states[c], head_masks, causal2, bd_mask,
                                   results[c])
        states.append(results[c]['s_new'])

    def finish(c, idx):
        rs = slice(c * CHUNK, (c + 1) * CHUNK)
        y = [results[c]['y'][i] for i in idx]
        mean = [m * (1.0 / HEAD_DIM) for m in _segment_sums(y, seg_ones)]
        yield
        dev = [y[n] - mean[n] for n in range(len(idx))]
        var = [m * (1.0 / HEAD_DIM) for m in _segment_sums([d * d for d in dev], seg_ones)]
        yield
        for n, i in enumerate(idx):
            s, g = pairs[i]
            sl = sls[g]
            yn = dev[n] * lax.rsqrt(var[n] + GN_EPS) * lnx_w[:, sl] + lnx_b[:, sl]
            y_b = (yn + bonus[c][i]) * gate[s][rs, sl]
            z_b = p_ref[s, rs, _ZB + g * LANES:_ZB + (g + 1) * LANES]
            merged_ref[s, rs, sl] = (y_a[s][rs, sl] + jax.nn.sigmoid(z_b) * y_b).astype(merged_ref.dtype)

    base = lambda c: 2 + WKV_STAGES * c
    tasks = []
    for c in range(n_ch):
        for j, idx in enumerate(pieces):
            tasks.append((0 if c == 0 else base(c - 1) + 2 * j, prepare(c, idx)))
    for c in range(n_ch):
        tasks.append((base(c), recur(c)))
        for j, idx in enumerate(pieces):
            tasks.append((base(c + 1) + (2 * j if c + 1 < n_ch else 0), finish(c, idx)))
    _run_tasks(sorted(tasks, key=lambda task: task[0]))
    for i in range(n_pairs):
        state[i] = states[n_ch][i]

    @pl.when(t == n_steps - 1)
    def _():
        for s in seqs:
            nconv_ref[s] = new_conv[s]
            nshift_ref[s] = new_shift[s]
            for h in range(N_HEADS):
                o = (h % HEADS_PER_GROUP) * HEAD_DIM
                nwkv_ref[s, h] = state[s * N_GROUPS + h // HEADS_PER_GROUP,
                                       o:o + HEAD_DIM, o:o + HEAD_DIM]


def _mixer(p2d, conv0, shift0, wkv0, convw, mu, vecs, wwa, g2, batch, seq):
    n_seq = SEQS_PER_STEP if batch % SEQS_PER_STEP == 0 else 1
    n_ch = CHUNKS_PER_STEP if seq % (CHUNKS_PER_STEP * CHUNK) == 0 else 1
    rows = n_ch * CHUNK
    n_steps = seq // rows
    kern = functools.partial(_mixer_kernel, n_steps=n_steps, n_seq=n_seq, n_ch=n_ch)
    const = lambda b, t: (0, 0)
    per_b3 = lambda b, t: (b, 0, 0)
    per_b4 = lambda b, t: (b, 0, 0, 0)
    merged, conv, shift, wkv = pl.pallas_call(
        kern,
        out_shape=(jax.ShapeDtypeStruct((batch, seq, D_MODEL), BF16),
                   jax.ShapeDtypeStruct((batch, CONV_W - 1, D_CONV), F32),
                   jax.ShapeDtypeStruct((batch, 1, D_SHIFT), F32),
                   jax.ShapeDtypeStruct((batch, N_HEADS, HEAD_DIM, HEAD_DIM), F32)),
        grid=(batch // n_seq, n_steps),
        in_specs=[pl.BlockSpec((n_seq, rows, D_PROJ), lambda b, t: (b, t, 0)),
                  pl.BlockSpec((n_seq, CONV_W - 1, D_CONV), per_b3),
                  pl.BlockSpec((n_seq, 1, D_SHIFT), per_b3),
                  pl.BlockSpec((n_seq, N_HEADS, HEAD_DIM, HEAD_DIM), per_b4),
                  pl.BlockSpec((CONV_W, D_CONV), const),
                  pl.BlockSpec((1, D_SHIFT), const),
                  pl.BlockSpec((SUBLANES, D_RWKV), const),
                  pl.BlockSpec((LANES, 2 * D_RWKV), const),
                  pl.BlockSpec((GATE_LORA, D_RWKV), const)],
        out_specs=(pl.BlockSpec((n_seq, rows, D_MODEL), lambda b, t: (b, t, 0)),
                   pl.BlockSpec((n_seq, CONV_W - 1, D_CONV), per_b3),
                   pl.BlockSpec((n_seq, 1, D_SHIFT), per_b3),
                   pl.BlockSpec((n_seq, N_HEADS, HEAD_DIM, HEAD_DIM), per_b4)),
        scratch_shapes=[pltpu.VMEM((n_seq, rows + SUBLANES, D_CONV), F32),
                        pltpu.VMEM((n_seq, rows + SUBLANES, D_SHIFT), F32),
                        pltpu.VMEM((n_seq * N_GROUPS, LANES, LANES), F32)],
        compiler_params=_params(("arbitrary", "arbitrary")),
        name="mixer",
    )(p2d.reshape(batch, seq, D_PROJ), conv0, shift0, wkv0, convw, mu, vecs, wwa, g2)
    return merged.reshape(batch * seq, D_MODEL), conv, shift, wkv


def _silu_mul(gate, up):
    return gate * jax.nn.sigmoid(gate) * up


def _ffn_dense_kernel(x_ref, m_ref, wo_ref, g_ref, wg_ref, wu_ref, wd_ref, o_ref, *, n_f):
    x1 = x_ref[...] + _dot(m_ref[...], wo_ref[...])
    h = _rms(x1, g_ref[...]).astype(BF16)
    tf = wg_ref.shape[1] // n_f
    acc = x1
    for j in range(n_f):
        act = _silu_mul(_dot(h, wg_ref[:, j * tf:(j + 1) * tf]),
                        _dot(h, wu_ref[:, j * tf:(j + 1) * tf])).astype(BF16)
        acc = acc + _dot(act, wd_ref[j * tf:(j + 1) * tf, :])
    o_ref[...] = acc


def _ffn_dense(x2d, merged, wo, g, wg, wu, wd, tm, n_f):
    n, d = x2d.shape
    f = wg.shape[1]
    row = lambda i: (i, 0)
    resident = lambda shape: pl.BlockSpec(shape, lambda i: (0, 0), pipeline_mode=pl.Buffered(1))
    return pl.pallas_call(
        functools.partial(_ffn_dense_kernel, n_f=n_f),
        out_shape=jax.ShapeDtypeStruct((n, d), F32),
        grid=(n // tm,),
        in_specs=[pl.BlockSpec((tm, d), row), pl.BlockSpec((tm, d), row),
                  resident((d, d)), pl.BlockSpec((1, d), lambda i: (0, 0)),
                  resident((d, f)), resident((d, f)), resident((f, d))],
        out_specs=pl.BlockSpec((tm, d), row),
        compiler_params=_params(("arbitrary",)),
        name="ffn_dense",
    )(x2d, merged, wo, g, wg, wu, wd)


def _top2(logits):
    lane_i = lax.broadcasted_iota(jnp.int32, logits.shape, 1)
    lane = lane_i.astype(F32)
    neg = jnp.float32(-jnp.inf)
    lg = jnp.where(lane_i < N_EXPERTS, logits, neg)
    m1 = jnp.max(lg, axis=-1, keepdims=True)
    i1 = jnp.min(jnp.where(lg == m1, lane, float(LANES)), axis=-1, keepdims=True)
    lg2 = jnp.where(lane == i1, neg, lg)
    m2 = jnp.max(lg2, axis=-1, keepdims=True)
    i2 = jnp.min(jnp.where(lg2 == m2, lane, float(LANES)), axis=-1, keepdims=True)
    e2 = jnp.exp(m2 - m1)
    den = 1.0 + e2
    ids = jnp.where(lane_i == 0, i1, jnp.where(lane_i == 1, i2, 0.0)).astype(jnp.int32)
    wts = jnp.where(lane_i == 0, 1.0 / den, jnp.where(lane_i == 1, e2 / den, 0.0))
    return ids, wts


def _router_kernel(xa_ref, ma_ref, xb_ref, mb_ref, wo_ref, g_ref, rw_ref, x1_ref, ids_ref, wts_ref,
                   *, blocks_a):
    first = pl.program_id(0) < blocks_a
    x = jnp.where(first, xa_ref[...], xb_ref[...])
    m = jnp.where(first, ma_ref[...], mb_ref[...])
    x1 = x + _dot(m, wo_ref[...])
    x1_ref[...] = x1
    ids_ref[...], wts_ref[...] = _top2(_mm3(_rms(x1, g_ref[...]), rw_ref[...]))


def _router(xa, ma, xb, mb, wo, g, rw, tm):
    (na, d), nb = xa.shape, xb.shape[0]
    blocks_a = na // tm
    n = na + nb
    row = lambda i: (i, 0)
    row_a = lambda i: (jnp.minimum(i, blocks_a - 1), 0)
    row_b = lambda i: (jnp.maximum(i - blocks_a, 0), 0)
    const = lambda i: (0, 0)
    return pl.pallas_call(
        functools.partial(_router_kernel, blocks_a=blocks_a),
        out_shape=(jax.ShapeDtypeStruct((n, d), F32),
                   jax.ShapeDtypeStruct((n, LANES), jnp.int32),
                   jax.ShapeDtypeStruct((n, LANES), F32)),
        grid=(n // tm,),
        in_specs=[pl.BlockSpec((tm, d), row_a), pl.BlockSpec((tm, d), row_a),
                  pl.BlockSpec((tm, d), row_b), pl.BlockSpec((tm, d), row_b),
                  pl.BlockSpec((d, d), const), pl.BlockSpec((1, d), const),
                  pl.BlockSpec((d, LANES), const)],
        out_specs=(pl.BlockSpec((tm, d), row), pl.BlockSpec((tm, LANES), row),
                   pl.BlockSpec((tm, LANES), row)),
        compiler_params=_params(("arbitrary",)),
        name="router",
    )(xa, ma, xb, mb, wo, g, rw)


def _experts_kernel(te_ref, nv_ref, inv_prev_ref, inv_ref, inv_next_ref, x1_hbm, g_ref,
                    wg_ref, wu_ref, wd_ref, y_hbm, xbuf, obuf, gather_sem, scatter_sem,
                    *, tm, n_tok, f_split):
    t = pl.program_id(0)
    n_used = nv_ref[0]
    cur = t & 1
    nxt = 1 - cur

    def gather_row(idx_ref, buf, r):
        tok = jnp.maximum(idx_ref[0, 0, r], 0) >> 1
        pltpu.make_async_copy(x1_hbm.at[pl.ds(tok, 1)], xbuf.at[buf, pl.ds(r, 1)],
                              gather_sem.at[buf]).start()

    def gather_loop(idx_ref, buf):
        def body(r, c):
            gather_row(idx_ref, buf, r)
            return c
        lax.fori_loop(0, tm, body, 0, unroll=8)

    def wait_gather(buf):
        pltpu.make_async_copy(x1_hbm.at[pl.ds(0, tm)], xbuf.at[buf], gather_sem.at[buf]).wait()

    def scatter_row(idx_ref, buf, r, real):
        a = idx_ref[0, 0, r]
        row = jnp.where((a >= 0) & real, (a & 1) * n_tok + (a >> 1), 2 * n_tok + r)
        pltpu.make_async_copy(obuf.at[buf, pl.ds(r, 1)], y_hbm.at[pl.ds(row, 1)], scatter_sem).start()

    def wait_scatter(buf):
        pltpu.make_async_copy(obuf.at[buf], y_hbm.at[pl.ds(0, tm)], scatter_sem).wait()

    @pl.when(t == 0)
    def _():
        obuf[...] = jnp.zeros_like(obuf)
        gather_loop(inv_ref, 0)

    wait_gather(cur)

    @pl.when(t < n_used)
    def _():
        h = _rms(xbuf[cur], g_ref[...]).astype(BF16)
        fs = wg_ref.shape[2] // f_split
        rows_per = -(-tm // max(1, (2 * f_split) // 3))
        acc = None
        for i in range(f_split):
            act = _silu_mul(_dot(h, wg_ref[0, :, i * fs:(i + 1) * fs]),
                            _dot(h, wu_ref[0, :, i * fs:(i + 1) * fs])).astype(BF16)
            part = _dot(act, wd_ref[0, i * fs:(i + 1) * fs, :])
            acc = part if acc is None else acc + part
            for r in range(i * rows_per, min((i + 1) * rows_per, tm)):
                gather_row(inv_next_ref, nxt, r)
                scatter_row(inv_prev_ref, nxt, r, t > 0)
        wait_scatter(nxt)
        obuf[cur] = acc

        @pl.when(t == n_used - 1)
        def _():
            def body(r, c):
                scatter_row(inv_ref, cur, r, True)
                return c
            lax.fori_loop(0, tm, body, 0, unroll=8)
            wait_scatter(cur)

    @pl.when(t >= n_used)
    def _():
        gather_loop(inv_next_ref, nxt)

    @pl.when(t == pl.num_programs(0) - 1)
    def _():
        wait_gather(nxt)


def _experts(x1, g, wg, wu, wd, tile_expert, n_used, inv, tm):
    n, d = x1.shape
    n_tiles = tile_expert.shape[0]
    f = wg.shape[2]
    expert_w = lambda t, te, nv: (jnp.minimum(te[t], N_EXPERTS - 1), 0, 0)
    single = pl.Buffered(1)
    inv3 = inv.reshape(n_tiles, 1, tm)
    idx_spec = lambda shift: pl.BlockSpec(
        (1, 1, tm), lambda t, te, nv: (jnp.clip(t + shift, 0, n_tiles - 1), 0, 0),
        memory_space=pltpu.SMEM)
    grid_spec = pltpu.PrefetchScalarGridSpec(
        num_scalar_prefetch=2,
        grid=(n_tiles,),
        in_specs=[idx_spec(-1), idx_spec(0), idx_spec(1),
                  pl.BlockSpec(memory_space=pl.ANY),
                  pl.BlockSpec((1, d), lambda t, te, nv: (0, 0)),
                  pl.BlockSpec((1, d, f), expert_w, pipeline_mode=single),
                  pl.BlockSpec((1, d, f), expert_w, pipeline_mode=single),
                  pl.BlockSpec((1, f, d), expert_w, pipeline_mode=single)],
        out_specs=pl.BlockSpec(memory_space=pl.ANY),
        scratch_shapes=[pltpu.VMEM((2, tm, d), F32), pltpu.VMEM((2, tm, d), F32),
                        pltpu.SemaphoreType.DMA((2,)), pltpu.SemaphoreType.DMA(())])
    return pl.pallas_call(
        functools.partial(_experts_kernel, tm=tm, n_tok=n, f_split=f // (2 * LANES)),
        out_shape=jax.ShapeDtypeStruct((2 * n + tm, d), F32),
        grid_spec=grid_spec,
        compiler_params=_params(("arbitrary",)),
        name="experts",
    )(tile_expert, n_used, inv3, inv3, inv3, x1, g, wg, wu, wd)


def _combine_kernel(x1_ref, y0_ref, y1_ref, wts_ref, fin_ref, oa_ref, ob_ref, *, blocks_a):
    w = wts_ref[...]
    moe = w[:, 0:1] * y0_ref[...] + w[:, 1:2] * y1_ref[...]
    out = _rms(x1_ref[...] + moe, fin_ref[...])
    first = pl.program_id(0) < blocks_a

    @pl.when(first)
    def _():
        oa_ref[...] = out

    @pl.when(jnp.logical_not(first))
    def _():
        ob_ref[...] = out


def _combine(x1, y2, wts, fin, na, tm):
    n, d = x1.shape
    n_blocks = n // tm
    blocks_a = na // tm
    row = lambda i: (i, 0)
    return pl.pallas_call(
        functools.partial(_combine_kernel, blocks_a=blocks_a),
        out_shape=(jax.ShapeDtypeStruct((na, d), F32), jax.ShapeDtypeStruct((n - na, d), F32)),
        grid=(n_blocks,),
        in_specs=[pl.BlockSpec((tm, d), row),
                  pl.BlockSpec((tm, d), row),
                  pl.BlockSpec((tm, d), lambda i: (i + n_blocks, 0)),
                  pl.BlockSpec((tm, LANES), row),
                  pl.BlockSpec((1, d), lambda i: (0, 0))],
        out_specs=(pl.BlockSpec((tm, d), lambda i: (jnp.minimum(i, blocks_a - 1), 0)),
                   pl.BlockSpec((tm, d), lambda i: (jnp.maximum(i - blocks_a, 0), 0))),
        compiler_params=_params(("arbitrary",)),
        name="combine",
    )(x1, y2, y2, wts, fin)


def _routing_plan(ids, tm):
    n = ids.shape[0]
    e_flat = ids[:, :2].reshape(-1)
    onehot = (e_flat[:, None] == jnp.arange(N_EXPERTS, dtype=jnp.int32)[None, :]).astype(jnp.int32)
    csum = jnp.cumsum(onehot, axis=0)
    counts = csum[-1]
    padded = (counts + tm - 1) // tm * tm
    ends = jnp.cumsum(padded)
    pos = jnp.sum(onehot * (csum - 1 + (ends - padded)[None, :]), axis=1)
    total = 2 * n + N_EXPERTS * tm
    inv = jnp.full((total,), -1, jnp.int32).at[pos].set(jnp.arange(2 * n, dtype=jnp.int32))
    tile_start = jnp.arange(total // tm, dtype=jnp.int32) * tm
    tile_expert = jnp.sum((tile_start[:, None] >= ends[None, :]).astype(jnp.int32), axis=1)
    return tile_expert, (ends[-1:] // tm).astype(jnp.int32), inv


def _common_tile(na, nb, want):
    tm = want
    while na % tm or nb % tm:
        tm //= 2
    return tm


def _ffn_moe(xa, ma, xb, mb, wo, g, rw, wg, wu, wd, fin):
    na, nb = xa.shape[0], xb.shape[0]
    n = na + nb
    big_experts = 2 * n >= 4 * 512 * N_EXPERTS
    tm_expert = _row_tile(2 * n, 512 if big_experts else 256)
    x1, ids, wts = _router(xa, ma, xb, mb, wo, g, rw, _common_tile(na, nb, 1024))
    tile_expert, n_used, inv = _routing_plan(ids, tm_expert)
    y2 = _experts(x1, g, wg, wu, wd, tile_expert, n_used, inv, tm_expert)
    return _combine(x1, y2, wts, fin, na, _common_tile(na, nb, 512))


def _prep_layer_weights(l, w_in, conv_w, shift_mu, decay_w0, decay_w2, aaa_a0, aaa_a2, gate_g2,
                        key_k, key_a, bonus_r_k, lnx_w, lnx_b, w_out):
    zero = jnp.zeros((DECAY_LORA, D_RWKV), F32)
    wwa = jnp.concatenate([jnp.concatenate([decay_w2[l], zero], axis=1),
                           jnp.concatenate([zero, aaa_a2[l]], axis=1)], axis=0).astype(BF16)
    vecs = jnp.stack([decay_w0[l], aaa_a0[l], key_k[l], key_a[l], bonus_r_k[l], lnx_w[l], lnx_b[l],
                      jnp.zeros((D_RWKV,), F32)])
    return dict(w_in=w_in[l].astype(BF16), convw=conv_w[l], mu=shift_mu[l][None], vecs=vecs,
                wwa=wwa, g2=gate_g2[l].astype(BF16), w_out=w_out[l].astype(BF16))


def _row_tile(n, want):
    return want if n % want == 0 else n


def _tiles(n):
    return _row_tile(n, 512), _row_tile(n, 256)


def _mix(x2d, batch, seq, lw, conv0, shift0, wkv0):
    _, tm_proj = _tiles(batch * seq)
    p = _norm_proj(x2d, lw['norm'], lw['w_in'], tm_proj, 6)
    merged, c, s, w = _mixer(p, conv0, shift0[:, None], wkv0, lw['convw'], lw['mu'], lw['vecs'],
                             lw['wwa'], lw['g2'], batch, seq)
    return merged, c, s[:, 0], w


def _lower_trunk(x, conv_st, shift_st, wkv_st, layers, ffn_norm, dense):
    batch, seq, d = x.shape
    n = batch * seq
    tm, _ = _tiles(n)
    x2d = x.reshape(n, d)
    merged, c0, s0, w0 = _mix(x2d, batch, seq, layers[0], conv_st[0], shift_st[0], wkv_st[0])
    wg, wu, wd = dense
    x2d = _ffn_dense(x2d, merged, layers[0]['w_out'], ffn_norm[0][None], wg, wu, wd, tm, 2)
    merged, c1, s1, w1 = _mix(x2d, batch, seq, layers[1], conv_st[1], shift_st[1], wkv_st[1])
    return x2d, merged, jnp.stack([c0, c1]), jnp.stack([s0, s1]), jnp.stack([w0, w1])


def kernel(x_prompt, x_sample, state_conv, state_shift, state_wkv, mix_norm, w_in, conv_w, shift_mu,
           decay_w0, decay_w2, aaa_a0, aaa_a2, gate_g2, key_k, key_a, bonus_r_k, lnx_w, lnx_b, w_out,
           ffn_norm, ffn_w_gate, ffn_w_up, ffn_w_down, router_w, moe_w_gate, moe_w_up, moe_w_down,
           final_norm):
    depth = w_in.shape[0]
    assert depth == 2 and ffn_w_gate.shape[0] == 1 and moe_w_gate.shape[0] == 1
    layers = []
    for l in range(depth):
        lw = _prep_layer_weights(l, w_in, conv_w, shift_mu, decay_w0, decay_w2, aaa_a0, aaa_a2,
                                 gate_g2, key_k, key_a, bonus_r_k, lnx_w, lnx_b, w_out)
        lw['norm'] = mix_norm[l][None]
        layers.append(lw)
    dense = (ffn_w_gate[0].astype(BF16), ffn_w_up[0].astype(BF16), ffn_w_down[0].astype(BF16))
    rw = jnp.pad(router_w[0], ((0, 0), (0, LANES - N_EXPERTS)))
    moe = (rw, moe_w_gate[0].astype(BF16), moe_w_up[0].astype(BF16), moe_w_down[0].astype(BF16))

    b = x_prompt.shape[0]
    zero_conv = jnp.zeros((depth, b) + state_conv.shape[2:], state_conv.dtype)
    zero_shift = jnp.zeros((depth, b) + state_shift.shape[2:], state_shift.dtype)
    zero_wkv = jnp.zeros((depth, b) + state_wkv.shape[2:], state_wkv.dtype)
    run = functools.partial(_lower_trunk, layers=layers, ffn_norm=ffn_norm, dense=dense)
    x_s, m_s, conv_s, shift_s, wkv_s = run(x_sample, state_conv, state_shift, state_wkv)
    x_p, m_p, conv_p, shift_p, wkv_p = run(x_prompt, zero_conv, zero_shift, zero_wkv)
    y_p, y_s = _ffn_moe(x_p, m_p, x_s, m_s, layers[1]['w_out'], ffn_norm[1][None], *moe,
                        final_norm[None])
    return (y_p.reshape(x_prompt.shape), y_s.reshape(x_sample.shape),
            conv_p, shift_p, wkv_p, conv_s, shift_s, wkv_s)
```

```python
import functools

import jax
import jax.numpy as jnp
from jax import lax
from jax.experimental import pallas as pl
from jax.experimental.pallas import tpu as pltpu

F32 = jnp.float32
BF16 = jnp.bfloat16

D_MODEL = 1024
N_HEADS = 16
HEAD_DIM = 64
D_RWKV = N_HEADS * HEAD_DIM
D_CONV = 1024
CONV_W = 3
DECAY_LORA = 64
AAA_LORA = 64
GATE_LORA = 128
D_SHIFT = 3 * D_RWKV + DECAY_LORA + AAA_LORA + GATE_LORA
D_PROJ = 2 * D_MODEL + 3 * D_CONV + D_SHIFT
N_EXPERTS = 8
RMS_EPS = 1e-5
GN_EPS = 64e-5
L2_EPS = 1e-12

LANES = 128
SUBLANES = 8
CHUNK = 64
HEADS_PER_GROUP = LANES // HEAD_DIM
SEQS_PER_STEP = 2
CHUNKS_PER_STEP = 2
PIECES = 4
PROJ_CHUNKS = 11
FFN_CHUNKS = 11
N_GROUPS = N_HEADS // HEADS_PER_GROUP
VMEM_LIMIT = 56 * 1024 * 1024

_ZA, _ZB, _CB, _CC, _CHH, _PS = (0, D_MODEL, 2 * D_MODEL, 2 * D_MODEL + D_CONV,
                                 2 * D_MODEL + 2 * D_CONV, 2 * D_MODEL + 3 * D_CONV)
_R, _K, _V, _WA, _G = 0, D_RWKV, 2 * D_RWKV, 3 * D_RWKV, 3 * D_RWKV + DECAY_LORA + AAA_LORA


def _params(semantics):
    return pltpu.CompilerParams(dimension_semantics=semantics, vmem_limit_bytes=VMEM_LIMIT)


def _rms(x, g):
    ms = jnp.mean(x * x, axis=-1, keepdims=True)
    return x * lax.rsqrt(ms + RMS_EPS) * g


def _split(x):
    hi = x.astype(BF16)
    lo = (x - hi.astype(F32)).astype(BF16)
    return hi, lo


_NN = (((1,), (0,)), ((), ()))
_NT = (((1,), (1,)), ((), ()))


def _dot(a, b, dims=_NN):
    return lax.dot_general(a, b, dims, preferred_element_type=F32)


def _mm1(a, b, dims=_NN):
    return _dot(a.astype(BF16), b.astype(BF16), dims)


def _mm3(a, b, dims=_NN):
    a1, a2 = _split(a)
    b1, b2 = _split(b)
    return _dot(a1, b1, dims) + (_dot(a1, b2, dims) + _dot(a2, b1, dims))


def _segment_sums(xs, seg_ones, exact=True):
    rows = xs[0].shape[0]
    if not exact:
        out = _dot(jnp.concatenate([x.astype(BF16) for x in xs], axis=0), seg_ones)
        return [out[i * rows:(i + 1) * rows] for i in range(len(xs))]
    parts = [half for x in xs for half in _split(x)]
    out = _dot(jnp.concatenate(parts, axis=0), seg_ones)
    return [out[2 * i * rows:(2 * i + 1) * rows] + out[(2 * i + 1) * rows:(2 * i + 2) * rows]
            for i in range(len(xs))]


def _norm_proj_kernel(x_ref, g_ref, w_ref, o_ref, *, n_col):
    xn = _rms(x_ref[...], g_ref[...]).astype(BF16)
    tn = w_ref.shape[1] // n_col
    for j in range(n_col):
        o_ref[:, j * tn:(j + 1) * tn] = _dot(xn, w_ref[:, j * tn:(j + 1) * tn])


def _norm_proj(x2d, g, w_bf16, tm, n_col):
    n, d = x2d.shape
    dp = w_bf16.shape[1]
    return pl.pallas_call(
        functools.partial(_norm_proj_kernel, n_col=n_col),
        out_shape=jax.ShapeDtypeStruct((n, dp), F32),
        grid=(n // tm,),
        in_specs=[pl.BlockSpec((tm, d), lambda i: (i, 0)),
                  pl.BlockSpec((1, d), lambda i: (0, 0)),
                  pl.BlockSpec((d, dp), lambda i: (0, 0), pipeline_mode=pl.Buffered(1))],
        out_specs=pl.BlockSpec((tm, dp), lambda i: (i, 0)),
        compiler_params=_params(("arbitrary",)),
        name="norm_proj",
    )(x2d, g, w_bf16)


def _block_diag(x, head_masks):
    return jnp.concatenate([jnp.where(m, x, 0.0) for m in head_masks], axis=0).astype(BF16)


def _wkv_prepare(r, k, v, kkn, a, ld, cum, head_masks):
    bd = functools.partial(_block_diag, head_masks=head_masks)
    cend = cum[CHUNK - 1:CHUNK, :]
    w_prev = jnp.exp(cum - ld)
    w_t = jnp.exp(cum)
    w_inv = 1.0 / w_t
    w_rest = jnp.exp(cend - cum)
    b = kkn * a
    return dict(
        lhs2=jnp.concatenate([-kkn * w_prev, r * w_t], axis=0).astype(BF16),
        rhs=jnp.concatenate([bd(b * w_inv), bd(k * w_inv)], axis=0),
        bd_v=bd(v), v=v, bk_h=jnp.concatenate([b * w_rest, k * w_rest], axis=0).astype(BF16),
        decay=jnp.exp(cend))


def _wkv_recurrence(ops, get_state, head_masks, causal2, bd_mask, out):
    groups = range(len(ops))
    bd = functools.partial(_block_diag, head_masks=head_masks)
    n_steps = CHUNK.bit_length() - 1
    s_prev = get_state()
    g_b, g_k, g_s = [], [], []
    for g in groups:
        gram = _mm1(ops[g]['lhs2'],
                    jnp.concatenate([ops[g]['rhs'], s_prev[g].astype(BF16)], axis=0), _NT)
        g_b.append(jnp.where(causal2, gram[:, :LANES], 0.0))
        g_k.append(jnp.where(causal2, gram[:, LANES:2 * LANES], 0.0))
        g_s.append(gram[:, 2 * LANES:])
    yield
    u = [g_s[g][:CHUNK] + _mm1(g_k[g][:CHUNK], ops[g]['bd_v']) for g in groups]
    pw = [g_b[g][:CHUNK] for g in groups]
    yield
    for i in range(n_steps):
        for g in groups:
            if i + 1 < n_steps:
                res = _mm1(pw[g], jnp.concatenate([bd(u[g]), bd(pw[g])], axis=1))
                u[g] = u[g] + res[:, :LANES]
                pw[g] = res[:, LANES:]
            else:
                u[g] = u[g] + _mm1(pw[g], bd(u[g]))
        yield
    y = []
    for g in groups:
        l_r = jnp.concatenate([g_b[g][CHUNK:], g_k[g][CHUNK:]], axis=1)
        y.append(g_s[g][CHUNK:] + _mm1(l_r, jnp.concatenate([bd(u[g]), ops[g]['bd_v']], axis=0)))
    out['y'] = y
    yield
    s_new = []
    for g in groups:
        uv_t = jnp.concatenate([u[g], ops[g]['v']], axis=0).T
        upd = _mm1(uv_t, ops[g]['bk_h'])
        s_new.append(s_prev[g] * ops[g]['decay'] + jnp.where(bd_mask, upd, 0.0))
    out['s_new'] = s_new
    yield


WKV_STAGES = 4 + CHUNK.bit_length() - 1


def _run_tasks(tasks):
    live = list(tasks)
    rnd = 0
    while live:
        for task in list(live):
            if task[0] <= rnd:
                try:
                    next(task[1])
                except StopIteration:
                    live.remove(task)
        rnd += 1


def _mixer_kernel(p_ref, conv0_ref, shift0_ref, wkv0_ref, convw_ref, mu_ref, vec_ref, wwa_ref,
                  g2_ref, merged_ref, nconv_ref, nshift_ref, nwkv_ref, cbuf, sbuf, state,
                  *, n_steps, n_seq, n_ch):
    t = pl.program_id(1)
    seqs = range(n_seq)
    groups = range(N_GROUPS)
    pairs = [(s, g) for s in seqs for g in groups]
    n_pairs = len(pairs)
    rows = n_ch * CHUNK

    @pl.when(t == 0)
    def _():
        zero = jnp.zeros((HEAD_DIM, HEAD_DIM), F32)
        for s in seqs:
            cbuf[s, 0:SUBLANES, :] = jnp.zeros((SUBLANES, D_CONV), F32)
            cbuf[s, SUBLANES - (CONV_W - 1):SUBLANES, :] = conv0_ref[s]
            sbuf[s, 0:SUBLANES, :] = jnp.zeros((SUBLANES, D_SHIFT), F32)
            sbuf[s, SUBLANES - 1:SUBLANES, :] = shift0_ref[s]
            for g in groups:
                h0 = g * HEADS_PER_GROUP
                state[s * N_GROUPS + g] = jnp.concatenate(
                    [jnp.concatenate([wkv0_ref[s, h0 + h] if hh == h else zero
                                      for hh in range(HEADS_PER_GROUP)], axis=1)
                     for h in range(HEADS_PER_GROUP)], axis=0)

    w0, a0, k_k, k_a, r_k, lnx_w, lnx_b = (vec_ref[i:i + 1, :] for i in range(7))
    lane = lax.broadcasted_iota(jnp.int32, (CHUNK, LANES), 1)
    lane_r = lax.broadcasted_iota(jnp.int32, (rows, LANES), 1)
    row = lax.broadcasted_iota(jnp.int32, (rows, rows), 0)
    col = lax.broadcasted_iota(jnp.int32, (rows, rows), 1)
    chunk_shift = CHUNK.bit_length() - 1
    tri = jnp.where((col <= row) & ((col >> chunk_shift) == (row >> chunk_shift)),
                    1.0, 0.0).astype(BF16)
    head_shift = HEAD_DIM.bit_length() - 1
    head_masks = [(lane >> head_shift) == h for h in range(HEADS_PER_GROUP)]
    row2 = lax.broadcasted_iota(jnp.int32, (2 * CHUNK, LANES), 0)
    lane2 = lax.broadcasted_iota(jnp.int32, (2 * CHUNK, LANES), 1)
    causal2 = (lane2 & (HEAD_DIM - 1)) < (row2 & (CHUNK - 1)) + (row2 >> chunk_shift)
    rowl = lax.broadcasted_iota(jnp.int32, (LANES, LANES), 0)
    lanel = lax.broadcasted_iota(jnp.int32, (LANES, LANES), 1)
    bd_mask = (rowl >> head_shift) == (lanel >> head_shift)
    seg_ones = jnp.where(bd_mask, 1.0, 0.0).astype(BF16)
    sls = [slice(g * LANES, (g + 1) * LANES) for g in groups]

    c0 = SUBLANES
    y_a, gate, new_conv, new_shift = [], [], [], []
    r, k, v, a, ld, cum = [], [], [], [], [], []
    for s in seqs:
        ch = p_ref[s, :, _CC:_CC + D_CONV] * p_ref[s, :, _CHH:_CHH + D_CONV]
        cbuf[s, c0:c0 + rows, :] = ch
        conv = (cbuf[s, c0 - 2:c0 - 2 + rows, :] * convw_ref[0:1, :]
                + cbuf[s, c0 - 1:c0 - 1 + rows, :] * convw_ref[1:2, :]
                + ch * convw_ref[2:3, :])
        y_a.append(jax.nn.sigmoid(p_ref[s, :, _ZA:_ZA + D_MODEL])
                   * (p_ref[s, :, _CB:_CB + D_CONV] * conv))
        new_conv.append(cbuf[s, c0 + rows - (CONV_W - 1):c0 + rows, :])
        cbuf[s, 0:SUBLANES, :] = cbuf[s, rows:rows + SUBLANES, :]

        ps = p_ref[s, :, _PS:_PS + D_SHIFT]
        sbuf[s, c0:c0 + rows, :] = ps
        prev = sbuf[s, c0 - 1:c0 - 1 + rows, :]
        xm = ps + (prev - ps) * mu_ref[...]
        new_shift.append(sbuf[s, c0 + rows - 1:c0 + rows, :])
        sbuf[s, 0:SUBLANES, :] = sbuf[s, rows:rows + SUBLANES, :]

        wa_in = xm[:, _WA:_WA + LANES]
        wa_in = jnp.where(lane_r < DECAY_LORA, jnp.tanh(wa_in), wa_in)
        wa = _mm1(wa_in, wwa_ref[...])
        gate.append(_mm1(jax.nn.sigmoid(xm[:, _G:_G + GATE_LORA]), g2_ref[...]))
        ld_all = -jnp.exp(-0.5) * jax.nn.sigmoid(w0 + wa[:, :D_RWKV])
        a_all = jax.nn.sigmoid(a0 + wa[:, D_RWKV:])
        ld1, ld2 = _split(ld_all)
        cum_all = _dot(tri, ld1) + _dot(tri, ld2)
        r.append(xm[:, _R:_R + D_RWKV])
        k.append(xm[:, _K:_K + D_RWKV])
        v.append(xm[:, _V:_V + D_RWKV])
        a.append(a_all)
        ld.append(ld_all)
        cum.append(cum_all)

    states = [[state[i] for i in range(n_pairs)]]
    pieces = [list(range(j, n_pairs, PIECES)) for j in range(PIECES)]
    ops = [[None] * n_pairs for _ in range(n_ch)]
    bonus = [[None] * n_pairs for _ in range(n_ch)]
    results = [dict() for _ in range(n_ch)]

    def prepare(c, idx):
        rs = slice(c * CHUNK, (c + 1) * CHUNK)
        cut = lambda xs, i: xs[pairs[i][0]][rs, sls[pairs[i][1]]]
        kk = [cut(k, i) * k_k[:, sls[pairs[i][1]]] for i in idx]
        kf = [cut(k, i) * (1.0 + (cut(a, i) - 1.0) * k_a[:, sls[pairs[i][1]]]) for i in idx]
        sums = _segment_sums([x * x for x in kk]
                             + [cut(r, i) * kf[n] * r_k[:, sls[pairs[i][1]]]
                                for n, i in enumerate(idx)], seg_ones, exact=False)
        yield
        for n, i in enumerate(idx):
            kkn = kk[n] / jnp.maximum(jnp.sqrt(sums[n]), L2_EPS)
            bonus[c][i] = sums[len(idx) + n] * cut(v, i)
            ops[c][i] = _wkv_prepare(cut(r, i), kf[n], cut(v, i), kkn, cut(a, i), cut(ld, i),
                                     cut(cum, i), head_masks)

    def recur(c):
        yield from _wkv_recurrence(ops[c], lambda: states[c], head_masks, causal2, bd_mask,
                                   results[c])
        states.append(results[c]['s_new'])

    def finish(c, idx):
        rs = slice(c * CHUNK, (c + 1) * CHUNK)
        y = [results[c]['y'][i] for i in idx]
        mean = [m * (1.0 / HEAD_DIM) for m in _segment_sums(y, seg_ones)]
        yield
        dev = [y[n] - mean[n] for n in range(len(idx))]
        var = [m * (1.0 / HEAD_DIM) for m in _segment_sums([d * d for d in dev], seg_ones)]
        yield
        for n, i in enumerate(idx):
            s, g = pairs[i]
            sl = sls[g]
            yn = dev[n] * lax.rsqrt(var[n] + GN_EPS) * lnx_w[:, sl] + lnx_b[:, sl]
            y_b = (yn + bonus[c][i]) * gate[s][rs, sl]
            z_b = p_ref[s, rs, _ZB + g * LANES:_ZB + (g + 1) * LANES]
            merged_ref[s, rs, sl] = (y_a[s][rs, sl] + jax.nn.sigmoid(z_b) * y_b).astype(merged_ref.dtype)

    base = lambda c: 2 + WKV_STAGES * c
    tasks = []
    for c in range(n_ch):
        for j, idx in enumerate(pieces):
            tasks.append((0 if c == 0 else base(c - 1) + 2 * j, prepare(c, idx)))
    for c in range(n_ch):
        tasks.append((base(c), recur(c)))
        for j, idx in enumerate(pieces):
            tasks.append((base(c + 1) + (2 * j if c + 1 < n_ch else 0), finish(c, idx)))
    _run_tasks(sorted(tasks, key=lambda task: task[0]))
    for i in range(n_pairs):
        state[i] = states[n_ch][i]

    @pl.when(t == n_steps - 1)
    def _():
        for s in seqs:
            nconv_ref[s] = new_conv[s]
            nshift_ref[s] = new_shift[s]
            for h in range(N_HEADS):
                o = (h % HEADS_PER_GROUP) * HEAD_DIM
                nwkv_ref[s, h] = state[s * N_GROUPS + h // HEADS_PER_GROUP,
                                       o:o + HEAD_DIM, o:o + HEAD_DIM]


def _mixer(p2d, conv0, shift0, wkv0, convw, mu, vecs, wwa, g2, batch, seq):
    n_seq = SEQS_PER_STEP if batch % SEQS_PER_STEP == 0 else 1
    n_ch = CHUNKS_PER_STEP if seq % (CHUNKS_PER_STEP * CHUNK) == 0 else 1
    rows = n_ch * CHUNK
    n_steps = seq // rows
    kern = functools.partial(_mixer_kernel, n_steps=n_steps, n_seq=n_seq, n_ch=n_ch)
    const = lambda b, t: (0, 0)
    per_b3 = lambda b, t: (b, 0, 0)
    per_b4 = lambda b, t: (b, 0, 0, 0)
    merged, conv, shift, wkv = pl.pallas_call(
        kern,
        out_shape=(jax.ShapeDtypeStruct((batch, seq, D_MODEL), BF16),
                   jax.ShapeDtypeStruct((batch, CONV_W - 1, D_CONV), F32),
                   jax.ShapeDtypeStruct((batch, 1, D_SHIFT), F32),
                   jax.ShapeDtypeStruct((batch, N_HEADS, HEAD_DIM, HEAD_DIM), F32)),
        grid=(batch // n_seq, n_steps),
        in_specs=[pl.BlockSpec((n_seq, rows, D_PROJ), lambda b, t: (b, t, 0)),
                  pl.BlockSpec((n_seq, CONV_W - 1, D_CONV), per_b3),
                  pl.BlockSpec((n_seq, 1, D_SHIFT), per_b3),
                  pl.BlockSpec((n_seq, N_HEADS, HEAD_DIM, HEAD_DIM), per_b4),
                  pl.BlockSpec((CONV_W, D_CONV), const),
                  pl.BlockSpec((1, D_SHIFT), const),
                  pl.BlockSpec((SUBLANES, D_RWKV), const),
                  pl.BlockSpec((LANES, 2 * D_RWKV), const),
                  pl.BlockSpec((GATE_LORA, D_RWKV), const)],
        out_specs=(pl.BlockSpec((n_seq, rows, D_MODEL), lambda b, t: (b, t, 0)),
                   pl.BlockSpec((n_seq, CONV_W - 1, D_CONV), per_b3),
                   pl.BlockSpec((n_seq, 1, D_SHIFT), per_b3),
                   pl.BlockSpec((n_seq, N_HEADS, HEAD_DIM, HEAD_DIM), per_b4)),
        scratch_shapes=[pltpu.VMEM((n_seq, rows + SUBLANES, D_CONV), F32),
                        pltpu.VMEM((n_seq, rows + SUBLANES, D_SHIFT), F32),
                        pltpu.VMEM((n_seq * N_GROUPS, LANES, LANES), F32)],
        compiler_params=_params(("arbitrary", "arbitrary")),
        name="mixer",
    )(p2d.reshape(batch, seq, D_PROJ), conv0, shift0, wkv0, convw, mu, vecs, wwa, g2)
    return merged.reshape(batch * seq, D_MODEL), conv, shift, wkv


def _silu_mul(gate, up):
    return gate * jax.nn.sigmoid(gate) * up


def _ffn_dense_kernel(x_ref, m_ref, wo_ref, g_ref, wg_ref, wu_ref, wd_ref, o_ref, *, n_f):
    x1 = x_ref[...] + _dot(m_ref[...], wo_ref[...])
    h = _rms(x1, g_ref[...]).astype(BF16)
    tf = wg_ref.shape[1] // n_f
    acc = x1
    for j in range(n_f):
        act = _silu_mul(_dot(h, wg_ref[:, j * tf:(j + 1) * tf]),
                        _dot(h, wu_ref[:, j * tf:(j + 1) * tf])).astype(BF16)
        acc = acc + _dot(act, wd_ref[j * tf:(j + 1) * tf, :])
    o_ref[...] = acc


def _ffn_dense(x2d, merged, wo, g, wg, wu, wd, tm, n_f):
    n, d = x2d.shape
    f = wg.shape[1]
    row = lambda i: (i, 0)
    resident = lambda shape: pl.BlockSpec(shape, lambda i: (0, 0), pipeline_mode=pl.Buffered(1))
    return pl.pallas_call(
        functools.partial(_ffn_dense_kernel, n_f=n_f),
        out_shape=jax.ShapeDtypeStruct((n, d), F32),
        grid=(n // tm,),
        in_specs=[pl.BlockSpec((tm, d), row), pl.BlockSpec((tm, d), row),
                  resident((d, d)), pl.BlockSpec((1, d), lambda i: (0, 0)),
                  resident((d, f)), resident((d, f)), resident((f, d))],
        out_specs=pl.BlockSpec((tm, d), row),
        compiler_params=_params(("arbitrary",)),
        name="ffn_dense",
    )(x2d, merged, wo, g, wg, wu, wd)


def _top2(logits):
    lane_i = lax.broadcasted_iota(jnp.int32, logits.shape, 1)
    lane = lane_i.astype(F32)
    neg = jnp.float32(-jnp.inf)
    lg = jnp.where(lane_i < N_EXPERTS, logits, neg)
    m1 = jnp.max(lg, axis=-1, keepdims=True)
    i1 = jnp.min(jnp.where(lg == m1, lane, float(LANES)), axis=-1, keepdims=True)
    lg2 = jnp.where(lane == i1, neg, lg)
    m2 = jnp.max(lg2, axis=-1, keepdims=True)
    i2 = jnp.min(jnp.where(lg2 == m2, lane, float(LANES)), axis=-1, keepdims=True)
    e2 = jnp.exp(m2 - m1)
    den = 1.0 + e2
    ids = jnp.where(lane_i == 0, i1, jnp.where(lane_i == 1, i2, 0.0)).astype(jnp.int32)
    wts = jnp.where(lane_i == 0, 1.0 / den, jnp.where(lane_i == 1, e2 / den, 0.0))
    return ids, wts


def _router_kernel(xa_ref, ma_ref, xb_ref, mb_ref, wo_ref, g_ref, rw_ref, x1_ref, ids_ref, wts_ref,
                   *, blocks_a):
    first = pl.program_id(0) < blocks_a
    x = jnp.where(first, xa_ref[...], xb_ref[...])
    m = jnp.where(first, ma_ref[...], mb_ref[...])
    x1 = x + _dot(m, wo_ref[...])
    x1_ref[...] = x1
    ids_ref[...], wts_ref[...] = _top2(_mm3(_rms(x1, g_ref[...]), rw_ref[...]))


def _router(xa, ma, xb, mb, wo, g, rw, tm):
    (na, d), nb = xa.shape, xb.shape[0]
    blocks_a = na // tm
    n = na + nb
    row = lambda i: (i, 0)
    row_a = lambda i: (jnp.minimum(i, blocks_a - 1), 0)
    row_b = lambda i: (jnp.maximum(i - blocks_a, 0), 0)
    const = lambda i: (0, 0)
    return pl.pallas_call(
        functools.partial(_router_kernel, blocks_a=blocks_a),
        out_shape=(jax.ShapeDtypeStruct((n, d), F32),
                   jax.ShapeDtypeStruct((n, LANES), jnp.int32),
                   jax.ShapeDtypeStruct((n, LANES), F32)),
        grid=(n // tm,),
        in_specs=[pl.BlockSpec((tm, d), row_a), pl.BlockSpec((tm, d), row_a),
                  pl.BlockSpec((tm, d), row_b), pl.BlockSpec((tm, d), row_b),
                  pl.BlockSpec((d, d), const), pl.BlockSpec((1, d), const),
                  pl.BlockSpec((d, LANES), const)],
        out_specs=(pl.BlockSpec((tm, d), row), pl.BlockSpec((tm, LANES), row),
                   pl.BlockSpec((tm, LANES), row)),
        compiler_params=_params(("arbitrary",)),
        name="router",
    )(xa, ma, xb, mb, wo, g, rw)


def _experts_kernel(te_ref, nv_ref, inv_prev_ref, inv_ref, inv_next_ref, x1_hbm, g_ref,
                    wg_ref, wu_ref, wd_ref, y_hbm, xbuf, obuf, gather_sem, scatter_sem,
                    *, tm, n_tok, f_split):
    t = pl.program_id(0)
    n_used = nv_ref[0]
    cur = t & 1
    nxt = 1 - cur

    def gather_row(idx_ref, buf, r):
        tok = jnp.maximum(idx_ref[0, 0, r], 0) >> 1
        pltpu.make_async_copy(x1_hbm.at[pl.ds(tok, 1)], xbuf.at[buf, pl.ds(r, 1)],
                              gather_sem.at[buf]).start()

    def gather_loop(idx_ref, buf):
        def body(r, c):
            gather_row(idx_ref, buf, r)
            return c
        lax.fori_loop(0, tm, body, 0, unroll=8)

    def wait_gather(buf):
        pltpu.make_async_copy(x1_hbm.at[pl.ds(0, tm)], xbuf.at[buf], gather_sem.at[buf]).wait()

    def scatter_row(idx_ref, buf, r, real):
        a = idx_ref[0, 0, r]
        row = jnp.where((a >= 0) & real, (a & 1) * n_tok + (a >> 1), 2 * n_tok + r)
        pltpu.make_async_copy(obuf.at[buf, pl.ds(r, 1)], y_hbm.at[pl.ds(row, 1)], scatter_sem).start()

    def wait_scatter(buf):
        pltpu.make_async_copy(obuf.at[buf], y_hbm.at[pl.ds(0, tm)], scatter_sem).wait()

    @pl.when(t == 0)
    def _():
        obuf[...] = jnp.zeros_like(obuf)
        gather_loop(inv_ref, 0)

    wait_gather(cur)

    @pl.when(t < n_used)
    def _():
        h = _rms(xbuf[cur], g_ref[...]).astype(BF16)
        fs = wg_ref.shape[2] // f_split
        rows_per = -(-tm // max(1, (2 * f_split) // 3))
        acc = None
        for i in range(f_split):
            act = _silu_mul(_dot(h, wg_ref[0, :, i * fs:(i + 1) * fs]),
                            _dot(h, wu_ref[0, :, i * fs:(i + 1) * fs])).astype(BF16)
            part = _dot(act, wd_ref[0, i * fs:(i + 1) * fs, :])
            acc = part if acc is None else acc + part
            for r in range(i * rows_per, min((i + 1) * rows_per, tm)):
                gather_row(inv_next_ref, nxt, r)
                scatter_row(inv_prev_ref, nxt, r, t > 0)
        wait_scatter(nxt)
        obuf[cur] = acc

        @pl.when(t == n_used - 1)
        def _():
            def body(r, c):
                scatter_row(inv_ref, cur, r, True)
                return c
            lax.fori_loop(0, tm, body, 0, unroll=8)
            wait_scatter(cur)

    @pl.when(t >= n_used)
    def _():
        gather_loop(inv_next_ref, nxt)

    @pl.when(t == pl.num_programs(0) - 1)
    def _():
        wait_gather(nxt)


def _experts(x1, g, wg, wu, wd, tile_expert, n_used, inv, tm):
    n, d = x1.shape
    n_tiles = tile_expert.shape[0]
    f = wg.shape[2]
    expert_w = lambda t, te, nv: (jnp.minimum(te[t], N_EXPERTS - 1), 0, 0)
    single = pl.Buffered(1)
    inv3 = inv.reshape(n_tiles, 1, tm)
    idx_spec = lambda shift: pl.BlockSpec(
        (1, 1, tm), lambda t, te, nv: (jnp.clip(t + shift, 0, n_tiles - 1), 0, 0),
        memory_space=pltpu.SMEM)
    grid_spec = pltpu.PrefetchScalarGridSpec(
        num_scalar_prefetch=2,
        grid=(n_tiles,),
        in_specs=[idx_spec(-1), idx_spec(0), idx_spec(1),
                  pl.BlockSpec(memory_space=pl.ANY),
                  pl.BlockSpec((1, d), lambda t, te, nv: (0, 0)),
                  pl.BlockSpec((1, d, f), expert_w, pipeline_mode=single),
                  pl.BlockSpec((1, d, f), expert_w, pipeline_mode=single),
                  pl.BlockSpec((1, f, d), expert_w, pipeline_mode=single)],
        out_specs=pl.BlockSpec(memory_space=pl.ANY),
        scratch_shapes=[pltpu.VMEM((2, tm, d), F32), pltpu.VMEM((2, tm, d), F32),
                        pltpu.SemaphoreType.DMA((2,)), pltpu.SemaphoreType.DMA(())])
    return pl.pallas_call(
        functools.partial(_experts_kernel, tm=tm, n_tok=n, f_split=f // (2 * LANES)),
        out_shape=jax.ShapeDtypeStruct((2 * n + tm, d), F32),
        grid_spec=grid_spec,
        compiler_params=_params(("arbitrary",)),
        name="experts",
    )(tile_expert, n_used, inv3, inv3, inv3, x1, g, wg, wu, wd)


def _combine_kernel(x1_ref, y0_ref, y1_ref, wts_ref, fin_ref, oa_ref, ob_ref, *, blocks_a):
    w = wts_ref[...]
    moe = w[:, 0:1] * y0_ref[...] + w[:, 1:2] * y1_ref[...]
    out = _rms(x1_ref[...] + moe, fin_ref[...])
    first = pl.program_id(0) < blocks_a

    @pl.when(first)
    def _():
        oa_ref[...] = out

    @pl.when(jnp.logical_not(first))
    def _():
        ob_ref[...] = out


def _combine(x1, y2, wts, fin, na, tm):
    n, d = x1.shape
    n_blocks = n // tm
    blocks_a = na // tm
    row = lambda i: (i, 0)
    return pl.pallas_call(
        functools.partial(_combine_kernel, blocks_a=blocks_a),
        out_shape=(jax.ShapeDtypeStruct((na, d), F32), jax.ShapeDtypeStruct((n - na, d), F32)),
        grid=(n_blocks,),
        in_specs=[pl.BlockSpec((tm, d), row),
                  pl.BlockSpec((tm, d), row),
                  pl.BlockSpec((tm, d), lambda i: (i + n_blocks, 0)),
                  pl.BlockSpec((tm, LANES), row),
                  pl.BlockSpec((1, d), lambda i: (0, 0))],
        out_specs=(pl.BlockSpec((tm, d), lambda i: (jnp.minimum(i, blocks_a - 1), 0)),
                   pl.BlockSpec((tm, d), lambda i: (jnp.maximum(i - blocks_a, 0), 0))),
        compiler_params=_params(("arbitrary",)),
        name="combine",
    )(x1, y2, y2, wts, fin)


def _routing_plan(ids, tm):
    n = ids.shape[0]
    e_flat = ids[:, :2].reshape(-1)
    onehot = (e_flat[:, None] == jnp.arange(N_EXPERTS, dtype=jnp.int32)[None, :]).astype(jnp.int32)
    csum = jnp.cumsum(onehot, axis=0)
    counts = csum[-1]
    padded = (counts + tm - 1) // tm * tm
    ends = jnp.cumsum(padded)
    pos = jnp.sum(onehot * (csum - 1 + (ends - padded)[None, :]), axis=1)
    total = 2 * n + N_EXPERTS * tm
    inv = jnp.full((total,), -1, jnp.int32).at[pos].set(jnp.arange(2 * n, dtype=jnp.int32))
    tile_start = jnp.arange(total // tm, dtype=jnp.int32) * tm
    tile_expert = jnp.sum((tile_start[:, None] >= ends[None, :]).astype(jnp.int32), axis=1)
    return tile_expert, (ends[-1:] // tm).astype(jnp.int32), inv


def _common_tile(na, nb, want):
    tm = want
    while na % tm or nb % tm:
        tm //= 2
    return tm


def _ffn_moe(xa, ma, xb, mb, wo, g, rw, wg, wu, wd, fin):
    na, nb = xa.shape[0], xb.shape[0]
    n = na + nb
    big_experts = 2 * n >= 4 * 512 * N_EXPERTS
    tm_expert = _row_tile(2 * n, 512 if big_experts else 256)
    x1, ids, wts = _router(xa, ma, xb, mb, wo, g, rw, _common_tile(na, nb, 1024))
    tile_expert, n_used, inv = _routing_plan(ids, tm_expert)
    y2 = _experts(x1, g, wg, wu, wd, tile_expert, n_used, inv, tm_expert)
    return _combine(x1, y2, wts, fin, na, _common_tile(na, nb, 512))


def _prep_layer_weights(l, w_in, conv_w, shift_mu, decay_w0, decay_w2, aaa_a0, aaa_a2, gate_g2,
                        key_k, key_a, bonus_r_k, lnx_w, lnx_b, w_out):
    zero = jnp.zeros((DECAY_LORA, D_RWKV), F32)
    wwa = jnp.concatenate([jnp.concatenate([decay_w2[l], zero], axis=1),
                           jnp.concatenate([zero, aaa_a2[l]], axis=1)], axis=0).astype(BF16)
    vecs = jnp.stack([decay_w0[l], aaa_a0[l], key_k[l], key_a[l], bonus_r_k[l], lnx_w[l], lnx_b[l],
                      jnp.zeros((D_RWKV,), F32)])
    return dict(w_in=w_in[l].astype(BF16), convw=conv_w[l], mu=shift_mu[l][None], vecs=vecs,
                wwa=wwa, g2=gate_g2[l].astype(BF16), w_out=w_out[l].astype(BF16))


def _row_tile(n, want):
    return want if n % want == 0 else n


def _tiles(n):
    return _row_tile(n, 512), _row_tile(n, 256)


def _mix(x2d, batch, seq, lw, conv0, shift0, wkv0):
    _, tm_proj = _tiles(batch * seq)
    p = _norm_proj(x2d, lw['norm'], lw['w_in'], tm_proj, PROJ_CHUNKS)
    merged, c, s, w = _mixer(p, conv0, shift0[:, None], wkv0, lw['convw'], lw['mu'], lw['vecs'],
                             lw['wwa'], lw['g2'], batch, seq)
    return merged, c, s[:, 0], w


def _lower_trunk(x, conv_st, shift_st, wkv_st, layers, ffn_norm, dense):
    batch, seq, d = x.shape
    n = batch * seq
    tm, _ = _tiles(n)
    x2d = x.reshape(n, d)
    merged, c0, s0, w0 = _mix(x2d, batch, seq, layers[0], conv_st[0], shift_st[0], wkv_st[0])
    wg, wu, wd = dense
    x2d = _ffn_dense(x2d, merged, layers[0]['w_out'], ffn_norm[0][None], wg, wu, wd, tm,
                     FFN_CHUNKS)
    merged, c1, s1, w1 = _mix(x2d, batch, seq, layers[1], conv_st[1], shift_st[1], wkv_st[1])
    return x2d, merged, jnp.stack([c0, c1]), jnp.stack([s0, s1]), jnp.stack([w0, w1])


def kernel(x_prompt, x_sample, state_conv, state_shift, state_wkv, mix_norm, w_in, conv_w, shift_mu,
           decay_w0, decay_w2, aaa_a0, aaa_a2, gate_g2, key_k, key_a, bonus_r_k, lnx_w, lnx_b, w_out,
           ffn_norm, ffn_w_gate, ffn_w_up, ffn_w_down, router_w, moe_w_gate, moe_w_up, moe_w_down,
           final_norm):
    depth = w_in.shape[0]
    assert depth == 2 and ffn_w_gate.shape[0] == 1 and moe_w_gate.shape[0] == 1
    layers = []
    for l in range(depth):
        lw = _prep_layer_weights(l, w_in, conv_w, shift_mu, decay_w0, decay_w2, aaa_a0, aaa_a2,
                                 gate_g2, key_k, key_a, bonus_r_k, lnx_w, lnx_b, w_out)
        lw['norm'] = mix_norm[l][None]
        layers.append(lw)
    dense = (ffn_w_gate[0].astype(BF16), ffn_w_up[0].astype(BF16), ffn_w_down[0].astype(BF16))
    rw = jnp.pad(router_w[0], ((0, 0), (0, LANES - N_EXPERTS)))
    moe = (rw, moe_w_gate[0].astype(BF16), moe_w_up[0].astype(BF16), moe_w_down[0].astype(BF16))

    b = x_prompt.shape[0]
    zero_conv = jnp.zeros((depth, b) + state_conv.shape[2:], state_conv.dtype)
    zero_shift = jnp.zeros((depth, b) + state_shift.shape[2:], state_shift.dtype)
    zero_wkv = jnp.zeros((depth, b) + state_wkv.shape[2:], state_wkv.dtype)
    run = functools.partial(_lower_trunk, layers=layers, ffn_norm=ffn_norm, dense=dense)
    x_s, m_s, conv_s, shift_s, wkv_s = run(x_sample, state_conv, state_shift, state_wkv)
    x_p, m_p, conv_p, shift_p, wkv_p = run(x_prompt, zero_conv, zero_shift, zero_wkv)
    y_p, y_s = _ffn_moe(x_p, m_p, x_s, m_s, layers[1]['w_out'], ffn_norm[1][None], *moe,
                        final_norm[None])
    return (y_p.reshape(x_prompt.shape), y_s.reshape(x_sample.shape),
            conv_p, shift_p, wkv_p, conv_s, shift_s, wkv_s)
```

```python
import functools

import jax
import jax.numpy as jnp
from jax import lax
from jax.experimental import pallas as pl
from jax.experimental.pallas import tpu as pltpu

F32 = jnp.float32
BF16 = jnp.bfloat16

D_MODEL = 1024
N_HEADS = 16
HEAD_DIM = 64
D_RWKV = N_HEADS * HEAD_DIM
D_CONV = 1024
CONV_W = 3
DECAY_LORA = 64
AAA_LORA = 64
GATE_LORA = 128
D_SHIFT = 3 * D_RWKV + DECAY_LORA + AAA_LORA + GATE_LORA
D_PROJ = 2 * D_MODEL + 3 * D_CONV + D_SHIFT
N_EXPERTS = 8
RMS_EPS = 1e-5
GN_EPS = 64e-5
L2_EPS = 1e-12

LANES = 128
SUBLANES = 8
CHUNK = 64
HEADS_PER_GROUP = LANES // HEAD_DIM
SEQS_PER_STEP = 2
CHUNKS_PER_STEP = 2
PIECES = 4
PROJ_CHUNKS = 11
FFN_CHUNKS = 11
ROUTER_SUBBLOCKS = 4
N_GROUPS = N_HEADS // HEADS_PER_GROUP
VMEM_LIMIT = 56 * 1024 * 1024

_ZA, _ZB, _CB, _CC, _CHH, _PS = (0, D_MODEL, 2 * D_MODEL, 2 * D_MODEL + D_CONV,
                                 2 * D_MODEL + 2 * D_CONV, 2 * D_MODEL + 3 * D_CONV)
_R, _K, _V, _WA, _G = 0, D_RWKV, 2 * D_RWKV, 3 * D_RWKV, 3 * D_RWKV + DECAY_LORA + AAA_LORA


def _params(semantics):
    return pltpu.CompilerParams(dimension_semantics=semantics, vmem_limit_bytes=VMEM_LIMIT)


def _rms(x, g):
    ms = jnp.mean(x * x, axis=-1, keepdims=True)
    return x * lax.rsqrt(ms + RMS_EPS) * g


def _split(x):
    hi = x.astype(BF16)
    lo = (x - hi.astype(F32)).astype(BF16)
    return hi, lo


_NN = (((1,), (0,)), ((), ()))
_NT = (((1,), (1,)), ((), ()))


def _dot(a, b, dims=_NN):
    return lax.dot_general(a, b, dims, preferred_element_type=F32)


def _mm1(a, b, dims=_NN):
    return _dot(a.astype(BF16), b.astype(BF16), dims)


def _mm3(a, b, dims=_NN):
    a1, a2 = _split(a)
    b1, b2 = _split(b)
    return _dot(a1, b1, dims) + (_dot(a1, b2, dims) + _dot(a2, b1, dims))


def _segment_sums(xs, seg_ones, exact=True):
    rows = xs[0].shape[0]
    if not exact:
        out = _dot(jnp.concatenate([x.astype(BF16) for x in xs], axis=0), seg_ones)
        return [out[i * rows:(i + 1) * rows] for i in range(len(xs))]
    parts = [half for x in xs for half in _split(x)]
    out = _dot(jnp.concatenate(parts, axis=0), seg_ones)
    return [out[2 * i * rows:(2 * i + 1) * rows] + out[(2 * i + 1) * rows:(2 * i + 2) * rows]
            for i in range(len(xs))]


def _norm_proj_kernel(x_ref, g_ref, w_ref, o_ref, *, n_col):
    xn = _rms(x_ref[...], g_ref[...]).astype(BF16)
    tn = w_ref.shape[1] // n_col
    for j in range(n_col):
        o_ref[:, j * tn:(j + 1) * tn] = _dot(xn, w_ref[:, j * tn:(j + 1) * tn])


def _norm_proj(x2d, g, w_bf16, tm, n_col):
    n, d = x2d.shape
    dp = w_bf16.shape[1]
    return pl.pallas_call(
        functools.partial(_norm_proj_kernel, n_col=n_col),
        out_shape=jax.ShapeDtypeStruct((n, dp), F32),
        grid=(n // tm,),
        in_specs=[pl.BlockSpec((tm, d), lambda i: (i, 0)),
                  pl.BlockSpec((1, d), lambda i: (0, 0)),
                  pl.BlockSpec((d, dp), lambda i: (0, 0), pipeline_mode=pl.Buffered(1))],
        out_specs=pl.BlockSpec((tm, dp), lambda i: (i, 0)),
        compiler_params=_params(("arbitrary",)),
        name="norm_proj",
    )(x2d, g, w_bf16)


def _block_diag(x, head_masks):
    return jnp.concatenate([jnp.where(m, x, 0.0) for m in head_masks], axis=0).astype(BF16)


def _wkv_prepare(r, k, v, kkn, a, ld, cum, head_masks):
    bd = functools.partial(_block_diag, head_masks=head_masks)
    cend = cum[CHUNK - 1:CHUNK, :]
    w_prev = jnp.exp(cum - ld)
    w_t = jnp.exp(cum)
    w_inv = 1.0 / w_t
    w_rest = jnp.exp(cend - cum)
    b = kkn * a
    return dict(
        lhs2=jnp.concatenate([-kkn * w_prev, r * w_t], axis=0).astype(BF16),
        rhs=jnp.concatenate([bd(b * w_inv), bd(k * w_inv)], axis=0),
        bd_v=bd(v), v=v, bk_h=jnp.concatenate([b * w_rest, k * w_rest], axis=0).astype(BF16),
        decay=jnp.exp(cend))


def _wkv_recurrence(ops, get_state, head_masks, causal2, bd_mask, out):
    groups = range(len(ops))
    bd = functools.partial(_block_diag, head_masks=head_masks)
    n_steps = CHUNK.bit_length() - 1
    s_prev = get_state()
    g_b, g_k, g_s = [], [], []
    for g in groups:
        gram = _mm1(ops[g]['lhs2'],
                    jnp.concatenate([ops[g]['rhs'], s_prev[g].astype(BF16)], axis=0), _NT)
        g_b.append(jnp.where(causal2, gram[:, :LANES], 0.0))
        g_k.append(jnp.where(causal2, gram[:, LANES:2 * LANES], 0.0))
        g_s.append(gram[:, 2 * LANES:])
    yield
    u = [g_s[g][:CHUNK] + _mm1(g_k[g][:CHUNK], ops[g]['bd_v']) for g in groups]
    pw = [g_b[g][:CHUNK] for g in groups]
    yield
    for i in range(n_steps):
        for g in groups:
            if i + 1 < n_steps:
                res = _mm1(pw[g], jnp.concatenate([bd(u[g]), bd(pw[g])], axis=1))
                u[g] = u[g] + res[:, :LANES]
                pw[g] = res[:, LANES:]
            else:
                u[g] = u[g] + _mm1(pw[g], bd(u[g]))
        yield
    y = []
    for g in groups:
        l_r = jnp.concatenate([g_b[g][CHUNK:], g_k[g][CHUNK:]], axis=1)
        y.append(g_s[g][CHUNK:] + _mm1(l_r, jnp.concatenate([bd(u[g]), ops[g]['bd_v']], axis=0)))
    out['y'] = y
    yield
    s_new = []
    for g in groups:
        uv_t = jnp.concatenate([u[g], ops[g]['v']], axis=0).T
        upd = _mm1(uv_t, ops[g]['bk_h'])
        s_new.append(s_prev[g] * ops[g]['decay'] + jnp.where(bd_mask, upd, 0.0))
    out['s_new'] = s_new
    yield


WKV_STAGES = 4 + CHUNK.bit_length() - 1


def _run_tasks(tasks):
    live = list(tasks)
    rnd = 0
    while live:
        for task in list(live):
            if task[0] <= rnd:
                try:
                    next(task[1])
                except StopIteration:
                    live.remove(task)
        rnd += 1


def _mixer_kernel(p_ref, conv0_ref, shift0_ref, wkv0_ref, convw_ref, mu_ref, vec_ref, wwa_ref,
                  g2_ref, merged_ref, nconv_ref, nshift_ref, nwkv_ref, cbuf, sbuf, state,
                  *, n_steps, n_seq, n_ch):
    t = pl.program_id(1)
    seqs = range(n_seq)
    groups = range(N_GROUPS)
    pairs = [(s, g) for s in seqs for g in groups]
    n_pairs = len(pairs)
    rows = n_ch * CHUNK

    @pl.when(t == 0)
    def _():
        zero = jnp.zeros((HEAD_DIM, HEAD_DIM), F32)
        for s in seqs:
            cbuf[s, 0:SUBLANES, :] = jnp.zeros((SUBLANES, D_CONV), F32)
            cbuf[s, SUBLANES - (CONV_W - 1):SUBLANES, :] = conv0_ref[s]
            sbuf[s, 0:SUBLANES, :] = jnp.zeros((SUBLANES, D_SHIFT), F32)
            sbuf[s, SUBLANES - 1:SUBLANES, :] = shift0_ref[s]
            for g in groups:
                h0 = g * HEADS_PER_GROUP
                state[s * N_GROUPS + g] = jnp.concatenate(
                    [jnp.concatenate([wkv0_ref[s, h0 + h] if hh == h else zero
                                      for hh in range(HEADS_PER_GROUP)], axis=1)
                     for h in range(HEADS_PER_GROUP)], axis=0)

    w0, a0, k_k, k_a, r_k, lnx_w, lnx_b = (vec_ref[i:i + 1, :] for i in range(7))
    lane = lax.broadcasted_iota(jnp.int32, (CHUNK, LANES), 1)
    lane_r = lax.broadcasted_iota(jnp.int32, (rows, LANES), 1)
    row = lax.broadcasted_iota(jnp.int32, (rows, rows), 0)
    col = lax.broadcasted_iota(jnp.int32, (rows, rows), 1)
    chunk_shift = CHUNK.bit_length() - 1
    tri = jnp.where((col <= row) & ((col >> chunk_shift) == (row >> chunk_shift)),
                    1.0, 0.0).astype(BF16)
    head_shift = HEAD_DIM.bit_length() - 1
    head_masks = [(lane >> head_shift) == h for h in range(HEADS_PER_GROUP)]
    row2 = lax.broadcasted_iota(jnp.int32, (2 * CHUNK, LANES), 0)
    lane2 = lax.broadcasted_iota(jnp.int32, (2 * CHUNK, LANES), 1)
    causal2 = (lane2 & (HEAD_DIM - 1)) < (row2 & (CHUNK - 1)) + (row2 >> chunk_shift)
    rowl = lax.broadcasted_iota(jnp.int32, (LANES, LANES), 0)
    lanel = lax.broadcasted_iota(jnp.int32, (LANES, LANES), 1)
    bd_mask = (rowl >> head_shift) == (lanel >> head_shift)
    seg_ones = jnp.where(bd_mask, 1.0, 0.0).astype(BF16)
    sls = [slice(g * LANES, (g + 1) * LANES) for g in groups]

    c0 = SUBLANES
    y_a, gate, new_conv, new_shift = [], [], [], []
    r, k, v, a, ld, cum = [], [], [], [], [], []
    for s in seqs:
        ch = p_ref[s, :, _CC:_CC + D_CONV] * p_ref[s, :, _CHH:_CHH + D_CONV]
        cbuf[s, c0:c0 + rows, :] = ch
        conv = (cbuf[s, c0 - 2:c0 - 2 + rows, :] * convw_ref[0:1, :]
                + cbuf[s, c0 - 1:c0 - 1 + rows, :] * convw_ref[1:2, :]
                + ch * convw_ref[2:3, :])
        y_a.append(jax.nn.sigmoid(p_ref[s, :, _ZA:_ZA + D_MODEL])
                   * (p_ref[s, :, _CB:_CB + D_CONV] * conv))
        new_conv.append(cbuf[s, c0 + rows - (CONV_W - 1):c0 + rows, :])
        cbuf[s, 0:SUBLANES, :] = cbuf[s, rows:rows + SUBLANES, :]

        ps = p_ref[s, :, _PS:_PS + D_SHIFT]
        sbuf[s, c0:c0 + rows, :] = ps
        prev = sbuf[s, c0 - 1:c0 - 1 + rows, :]
        xm = ps + (prev - ps) * mu_ref[...]
        new_shift.append(sbuf[s, c0 + rows - 1:c0 + rows, :])
        sbuf[s, 0:SUBLANES, :] = sbuf[s, rows:rows + SUBLANES, :]

        wa_in = xm[:, _WA:_WA + LANES]
        wa_in = jnp.where(lane_r < DECAY_LORA, jnp.tanh(wa_in), wa_in)
        wa = _mm1(wa_in, wwa_ref[...])
        gate.append(_mm1(jax.nn.sigmoid(xm[:, _G:_G + GATE_LORA]), g2_ref[...]))
        ld_all = -jnp.exp(-0.5) * jax.nn.sigmoid(w0 + wa[:, :D_RWKV])
        a_all = jax.nn.sigmoid(a0 + wa[:, D_RWKV:])
        ld1, ld2 = _split(ld_all)
        cum_all = _dot(tri, ld1) + _dot(tri, ld2)
        r.append(xm[:, _R:_R + D_RWKV])
        k.append(xm[:, _K:_K + D_RWKV])
        v.append(xm[:, _V:_V + D_RWKV])
        a.append(a_all)
        ld.append(ld_all)
        cum.append(cum_all)

    states = [[state[i] for i in range(n_pairs)]]
    pieces = [list(range(j, n_pairs, PIECES)) for j in range(PIECES)]
    ops = [[None] * n_pairs for _ in range(n_ch)]
    bonus = [[None] * n_pairs for _ in range(n_ch)]
    results = [dict() for _ in range(n_ch)]

    def prepare(c, idx):
        rs = slice(c * CHUNK, (c + 1) * CHUNK)
        cut = lambda xs, i: xs[pairs[i][0]][rs, sls[pairs[i][1]]]
        kk = [cut(k, i) * k_k[:, sls[pairs[i][1]]] for i in idx]
        kf = [cut(k, i) * (1.0 + (cut(a, i) - 1.0) * k_a[:, sls[pairs[i][1]]]) for i in idx]
        sums = _segment_sums([x * x for x in kk]
                             + [cut(r, i) * kf[n] * r_k[:, sls[pairs[i][1]]]
                                for n, i in enumerate(idx)], seg_ones, exact=False)
        yield
        for n, i in enumerate(idx):
            kkn = kk[n] / jnp.maximum(jnp.sqrt(sums[n]), L2_EPS)
            bonus[c][i] = sums[len(idx) + n] * cut(v, i)
            ops[c][i] = _wkv_prepare(cut(r, i), kf[n], cut(v, i), kkn, cut(a, i), cut(ld, i),
                                     cut(cum, i), head_masks)

    def recur(c):
        yield from _wkv_recurrence(ops[c], lambda: states[c], head_masks, causal2, bd_mask,
                                   results[c])
        states.append(results[c]['s_new'])

    def finish(c, idx):
        rs = slice(c * CHUNK, (c + 1) * CHUNK)
        y = [results[c]['y'][i] for i in idx]
        mean = [m * (1.0 / HEAD_DIM) for m in _segment_sums(y, seg_ones)]
        yield
        dev = [y[n] - mean[n] for n in range(len(idx))]
        var = [m * (1.0 / HEAD_DIM) for m in _segment_sums([d * d for d in dev], seg_ones)]
        yield
        for n, i in enumerate(idx):
            s, g = pairs[i]
            sl = sls[g]
            yn = dev[n] * lax.rsqrt(var[n] + GN_EPS) * lnx_w[:, sl] + lnx_b[:, sl]
            y_b = (yn + bonus[c][i]) * gate[s][rs, sl]
            z_b = p_ref[s, rs, _ZB + g * LANES:_ZB + (g + 1) * LANES]
            merged_ref[s, rs, sl] = (y_a[s][rs, sl] + jax.nn.sigmoid(z_b) * y_b).astype(merged_ref.dtype)

    base = lambda c: 2 + WKV_STAGES * c
    tasks = []
    for c in range(n_ch):
        for j, idx in enumerate(pieces):
            tasks.append((0 if c == 0 else base(c - 1) + 2 * j, prepare(c, idx)))
    for c in range(n_ch):
        tasks.append((base(c), recur(c)))
        for j, idx in enumerate(pieces):
            tasks.append((base(c + 1) + (2 * j if c + 1 < n_ch else 0), finish(c, idx)))
    _run_tasks(sorted(tasks, key=lambda task: task[0]))
    for i in range(n_pairs):
        state[i] = states[n_ch][i]

    @pl.when(t == n_steps - 1)
    def _():
        for s in seqs:
            nconv_ref[s] = new_conv[s]
            nshift_ref[s] = new_shift[s]
            for h in range(N_HEADS):
                o = (h % HEADS_PER_GROUP) * HEAD_DIM
                nwkv_ref[s, h] = state[s * N_GROUPS + h // HEADS_PER_GROUP,
                                       o:o + HEAD_DIM, o:o + HEAD_DIM]


def _mixer(p2d, conv0, shift0, wkv0, convw, mu, vecs, wwa, g2, batch, seq):
    n_seq = SEQS_PER_STEP if batch % SEQS_PER_STEP == 0 else 1
    n_ch = CHUNKS_PER_STEP if seq % (CHUNKS_PER_STEP * CHUNK) == 0 else 1
    rows = n_ch * CHUNK
    n_steps = seq // rows
    kern = functools.partial(_mixer_kernel, n_steps=n_steps, n_seq=n_seq, n_ch=n_ch)
    const = lambda b, t: (0, 0)
    per_b3 = lambda b, t: (b, 0, 0)
    per_b4 = lambda b, t: (b, 0, 0, 0)
    merged, conv, shift, wkv = pl.pallas_call(
        kern,
        out_shape=(jax.ShapeDtypeStruct((batch, seq, D_MODEL), BF16),
                   jax.ShapeDtypeStruct((batch, CONV_W - 1, D_CONV), F32),
                   jax.ShapeDtypeStruct((batch, 1, D_SHIFT), F32),
                   jax.ShapeDtypeStruct((batch, N_HEADS, HEAD_DIM, HEAD_DIM), F32)),
        grid=(batch // n_seq, n_steps),
        in_specs=[pl.BlockSpec((n_seq, rows, D_PROJ), lambda b, t: (b, t, 0)),
                  pl.BlockSpec((n_seq, CONV_W - 1, D_CONV), per_b3),
                  pl.BlockSpec((n_seq, 1, D_SHIFT), per_b3),
                  pl.BlockSpec((n_seq, N_HEADS, HEAD_DIM, HEAD_DIM), per_b4),
                  pl.BlockSpec((CONV_W, D_CONV), const),
                  pl.BlockSpec((1, D_SHIFT), const),
                  pl.BlockSpec((SUBLANES, D_RWKV), const),
                  pl.BlockSpec((LANES, 2 * D_RWKV), const),
                  pl.BlockSpec((GATE_LORA, D_RWKV), const)],
        out_specs=(pl.BlockSpec((n_seq, rows, D_MODEL), lambda b, t: (b, t, 0)),
                   pl.BlockSpec((n_seq, CONV_W - 1, D_CONV), per_b3),
                   pl.BlockSpec((n_seq, 1, D_SHIFT), per_b3),
                   pl.BlockSpec((n_seq, N_HEADS, HEAD_DIM, HEAD_DIM), per_b4)),
        scratch_shapes=[pltpu.VMEM((n_seq, rows + SUBLANES, D_CONV), F32),
                        pltpu.VMEM((n_seq, rows + SUBLANES, D_SHIFT), F32),
                        pltpu.VMEM((n_seq * N_GROUPS, LANES, LANES), F32)],
        compiler_params=_params(("arbitrary", "arbitrary")),
        name="mixer",
    )(p2d.reshape(batch, seq, D_PROJ), conv0, shift0, wkv0, convw, mu, vecs, wwa, g2)
    return merged.reshape(batch * seq, D_MODEL), conv, shift, wkv


def _silu_mul(gate, up):
    return gate * jax.nn.sigmoid(gate) * up


def _ffn_dense_kernel(x_ref, m_ref, wo_ref, g_ref, wg_ref, wu_ref, wd_ref, o_ref, *, n_f):
    x1 = x_ref[...] + _dot(m_ref[...], wo_ref[...])
    h = _rms(x1, g_ref[...]).astype(BF16)
    tf = wg_ref.shape[1] // n_f
    acc = x1
    for j in range(n_f):
        act = _silu_mul(_dot(h, wg_ref[:, j * tf:(j + 1) * tf]),
                        _dot(h, wu_ref[:, j * tf:(j + 1) * tf])).astype(BF16)
        acc = acc + _dot(act, wd_ref[j * tf:(j + 1) * tf, :])
    o_ref[...] = acc


def _ffn_dense(x2d, merged, wo, g, wg, wu, wd, tm, n_f):
    n, d = x2d.shape
    f = wg.shape[1]
    row = lambda i: (i, 0)
    resident = lambda shape: pl.BlockSpec(shape, lambda i: (0, 0), pipeline_mode=pl.Buffered(1))
    return pl.pallas_call(
        functools.partial(_ffn_dense_kernel, n_f=n_f),
        out_shape=jax.ShapeDtypeStruct((n, d), F32),
        grid=(n // tm,),
        in_specs=[pl.BlockSpec((tm, d), row), pl.BlockSpec((tm, d), row),
                  resident((d, d)), pl.BlockSpec((1, d), lambda i: (0, 0)),
                  resident((d, f)), resident((d, f)), resident((f, d))],
        out_specs=pl.BlockSpec((tm, d), row),
        compiler_params=_params(("arbitrary",)),
        name="ffn_dense",
    )(x2d, merged, wo, g, wg, wu, wd)


def _top2(logits):
    lane_i = lax.broadcasted_iota(jnp.int32, logits.shape, 1)
    lane = lane_i.astype(F32)
    neg = jnp.float32(-jnp.inf)
    lg = jnp.where(lane_i < N_EXPERTS, logits, neg)
    m1 = jnp.max(lg, axis=-1, keepdims=True)
    i1 = jnp.min(jnp.where(lg == m1, lane, float(LANES)), axis=-1, keepdims=True)
    lg2 = jnp.where(lane == i1, neg, lg)
    m2 = jnp.max(lg2, axis=-1, keepdims=True)
    i2 = jnp.min(jnp.where(lg2 == m2, lane, float(LANES)), axis=-1, keepdims=True)
    e2 = jnp.exp(m2 - m1)
    den = 1.0 + e2
    ids = jnp.where(lane_i == 0, i1, jnp.where(lane_i == 1, i2, 0.0)).astype(jnp.int32)
    wts = jnp.where(lane_i == 0, 1.0 / den, jnp.where(lane_i == 1, e2 / den, 0.0))
    return ids, wts


def _router_kernel(xa_ref, ma_ref, xb_ref, mb_ref, wo_ref, g_ref, rw_ref, x1_ref, ids_ref, wts_ref,
                   *, blocks_a, n_sub):
    first = pl.program_id(0) < blocks_a
    rows = x1_ref.shape[0] // n_sub
    subs = [slice(i * rows, (i + 1) * rows) for i in range(n_sub)]
    x1 = [jnp.where(first, xa_ref[sl, :], xb_ref[sl, :])
          + _dot(jnp.where(first, ma_ref[sl, :], mb_ref[sl, :]), wo_ref[...]) for sl in subs]
    logits = [_mm3(_rms(v, g_ref[...]), rw_ref[...]) for v in x1]
    for sl, v, lg in zip(subs, x1, logits):
        x1_ref[sl, :] = v
        ids_ref[sl, :], wts_ref[sl, :] = _top2(lg)


def _router(xa, ma, xb, mb, wo, g, rw, tm):
    (na, d), nb = xa.shape, xb.shape[0]
    blocks_a = na // tm
    n = na + nb
    row = lambda i: (i, 0)
    row_a = lambda i: (jnp.minimum(i, blocks_a - 1), 0)
    row_b = lambda i: (jnp.maximum(i - blocks_a, 0), 0)
    const = lambda i: (0, 0)
    return pl.pallas_call(
        functools.partial(_router_kernel, blocks_a=blocks_a, n_sub=ROUTER_SUBBLOCKS),
        out_shape=(jax.ShapeDtypeStruct((n, d), F32),
                   jax.ShapeDtypeStruct((n, LANES), jnp.int32),
                   jax.ShapeDtypeStruct((n, LANES), F32)),
        grid=(n // tm,),
        in_specs=[pl.BlockSpec((tm, d), row_a), pl.BlockSpec((tm, d), row_a),
                  pl.BlockSpec((tm, d), row_b), pl.BlockSpec((tm, d), row_b),
                  pl.BlockSpec((d, d), const), pl.BlockSpec((1, d), const),
                  pl.BlockSpec((d, LANES), const)],
        out_specs=(pl.BlockSpec((tm, d), row), pl.BlockSpec((tm, LANES), row),
                   pl.BlockSpec((tm, LANES), row)),
        compiler_params=_params(("arbitrary",)),
        name="router",
    )(xa, ma, xb, mb, wo, g, rw)


def _experts_kernel(te_ref, nv_ref, inv_prev_ref, inv_ref, inv_next_ref, x1_hbm, g_ref,
                    wg_ref, wu_ref, wd_ref, y_hbm, xbuf, obuf, gather_sem, scatter_sem,
                    *, tm, n_tok, f_split):
    t = pl.program_id(0)
    n_used = nv_ref[0]
    cur = t & 1
    nxt = 1 - cur

    def gather_row(idx_ref, buf, r):
        tok = jnp.maximum(idx_ref[0, 0, r], 0) >> 1
        pltpu.make_async_copy(x1_hbm.at[pl.ds(tok, 1)], xbuf.at[buf, pl.ds(r, 1)],
                              gather_sem.at[buf]).start()

    def gather_loop(idx_ref, buf):
        def body(r, c):
            gather_row(idx_ref, buf, r)
            return c
        lax.fori_loop(0, tm, body, 0, unroll=8)

    def wait_gather(buf):
        pltpu.make_async_copy(x1_hbm.at[pl.ds(0, tm)], xbuf.at[buf], gather_sem.at[buf]).wait()

    def scatter_row(idx_ref, buf, r, real):
        a = idx_ref[0, 0, r]
        row = jnp.where((a >= 0) & real, (a & 1) * n_tok + (a >> 1), 2 * n_tok + r)
        pltpu.make_async_copy(obuf.at[buf, pl.ds(r, 1)], y_hbm.at[pl.ds(row, 1)], scatter_sem).start()

    def wait_scatter(buf):
        pltpu.make_async_copy(obuf.at[buf], y_hbm.at[pl.ds(0, tm)], scatter_sem).wait()

    @pl.when(t == 0)
    def _():
        obuf[...] = jnp.zeros_like(obuf)
        gather_loop(inv_ref, 0)

    wait_gather(cur)

    @pl.when(t < n_used)
    def _():
        h = _rms(xbuf[cur], g_ref[...]).astype(BF16)
        fs = wg_ref.shape[2] // f_split
        rows_per = -(-tm // max(1, (2 * f_split) // 3))
        acc = None
        for i in range(f_split):
            act = _silu_mul(_dot(h, wg_ref[0, :, i * fs:(i + 1) * fs]),
                            _dot(h, wu_ref[0, :, i * fs:(i + 1) * fs])).astype(BF16)
            part = _dot(act, wd_ref[0, i * fs:(i + 1) * fs, :])
            acc = part if acc is None else acc + part
            for r in range(i * rows_per, min((i + 1) * rows_per, tm)):
                gather_row(inv_next_ref, nxt, r)
                scatter_row(inv_prev_ref, nxt, r, t > 0)
        wait_scatter(nxt)
        obuf[cur] = acc

        @pl.when(t == n_used - 1)
        def _():
            def body(r, c):
                scatter_row(inv_ref, cur, r, True)
                return c
            lax.fori_loop(0, tm, body, 0, unroll=8)
            wait_scatter(cur)

    @pl.when(t >= n_used)
    def _():
        gather_loop(inv_next_ref, nxt)

    @pl.when(t == pl.num_programs(0) - 1)
    def _():
        wait_gather(nxt)


def _experts(x1, g, wg, wu, wd, tile_expert, n_used, inv, tm):
    n, d = x1.shape
    n_tiles = tile_expert.shape[0]
    f = wg.shape[2]
    expert_w = lambda t, te, nv: (jnp.minimum(te[t], N_EXPERTS - 1), 0, 0)
    single = pl.Buffered(1)
    inv3 = inv.reshape(n_tiles, 1, tm)
    idx_spec = lambda shift: pl.BlockSpec(
        (1, 1, tm), lambda t, te, nv: (jnp.clip(t + shift, 0, n_tiles - 1), 0, 0),
        memory_space=pltpu.SMEM)
    grid_spec = pltpu.PrefetchScalarGridSpec(
        num_scalar_prefetch=2,
        grid=(n_tiles,),
        in_specs=[idx_spec(-1), idx_spec(0), idx_spec(1),
                  pl.BlockSpec(memory_space=pl.ANY),
                  pl.BlockSpec((1, d), lambda t, te, nv: (0, 0)),
                  pl.BlockSpec((1, d, f), expert_w, pipeline_mode=single),
                  pl.BlockSpec((1, d, f), expert_w, pipeline_mode=single),
                  pl.BlockSpec((1, f, d), expert_w, pipeline_mode=single)],
        out_specs=pl.BlockSpec(memory_space=pl.ANY),
        scratch_shapes=[pltpu.VMEM((2, tm, d), F32), pltpu.VMEM((2, tm, d), F32),
                        pltpu.SemaphoreType.DMA((2,)), pltpu.SemaphoreType.DMA(())])
    return pl.pallas_call(
        functools.partial(_experts_kernel, tm=tm, n_tok=n, f_split=f // (2 * LANES)),
        out_shape=jax.ShapeDtypeStruct((2 * n + tm, d), F32),
        grid_spec=grid_spec,
        compiler_params=_params(("arbitrary",)),
        name="experts",
    )(tile_expert, n_used, inv3, inv3, inv3, x1, g, wg, wu, wd)


def _combine_kernel(x1_ref, y0_ref, y1_ref, wts_ref, fin_ref, oa_ref, ob_ref, *, blocks_a):
    w = wts_ref[...]
    moe = w[:, 0:1] * y0_ref[...] + w[:, 1:2] * y1_ref[...]
    out = _rms(x1_ref[...] + moe, fin_ref[...])
    first = pl.program_id(0) < blocks_a

    @pl.when(first)
    def _():
        oa_ref[...] = out

    @pl.when(jnp.logical_not(first))
    def _():
        ob_ref[...] = out


def _combine(x1, y2, wts, fin, na, tm):
    n, d = x1.shape
    n_blocks = n // tm
    blocks_a = na // tm
    row = lambda i: (i, 0)
    return pl.pallas_call(
        functools.partial(_combine_kernel, blocks_a=blocks_a),
        out_shape=(jax.ShapeDtypeStruct((na, d), F32), jax.ShapeDtypeStruct((n - na, d), F32)),
        grid=(n_blocks,),
        in_specs=[pl.BlockSpec((tm, d), row),
                  pl.BlockSpec((tm, d), row),
                  pl.BlockSpec((tm, d), lambda i: (i + n_blocks, 0)),
                  pl.BlockSpec((tm, LANES), row),
                  pl.BlockSpec((1, d), lambda i: (0, 0))],
        out_specs=(pl.BlockSpec((tm, d), lambda i: (jnp.minimum(i, blocks_a - 1), 0)),
                   pl.BlockSpec((tm, d), lambda i: (jnp.maximum(i - blocks_a, 0), 0))),
        compiler_params=_params(("arbitrary",)),
        name="combine",
    )(x1, y2, y2, wts, fin)


def _routing_plan(ids, tm):
    n = ids.shape[0]
    e_flat = ids[:, :2].reshape(-1)
    onehot = (e_flat[:, None] == jnp.arange(N_EXPERTS, dtype=jnp.int32)[None, :]).astype(jnp.int32)
    csum = jnp.cumsum(onehot, axis=0)
    counts = csum[-1]
    padded = (counts + tm - 1) // tm * tm
    ends = jnp.cumsum(padded)
    pos = jnp.sum(onehot * (csum - 1 + (ends - padded)[None, :]), axis=1)
    total = 2 * n + N_EXPERTS * tm
    inv = jnp.full((total,), -1, jnp.int32).at[pos].set(jnp.arange(2 * n, dtype=jnp.int32))
    tile_start = jnp.arange(total // tm, dtype=jnp.int32) * tm
    tile_expert = jnp.sum((tile_start[:, None] >= ends[None, :]).astype(jnp.int32), axis=1)
    return tile_expert, (ends[-1:] // tm).astype(jnp.int32), inv


def _common_tile(na, nb, want):
    tm = want
    while na % tm or nb % tm:
        tm //= 2
    return tm


def _ffn_moe(xa, ma, xb, mb, wo, g, rw, wg, wu, wd, fin):
    na, nb = xa.shape[0], xb.shape[0]
    n = na + nb
    big_experts = 2 * n >= 4 * 512 * N_EXPERTS
    tm_expert = _row_tile(2 * n, 512 if big_experts else 256)
    x1, ids, wts = _router(xa, ma, xb, mb, wo, g, rw, _common_tile(na, nb, 1024))
    tile_expert, n_used, inv = _routing_plan(ids, tm_expert)
    y2 = _experts(x1, g, wg, wu, wd, tile_expert, n_used, inv, tm_expert)
    return _combine(x1, y2, wts, fin, na, _common_tile(na, nb, 512))


def _prep_layer_weights(l, w_in, conv_w, shift_mu, decay_w0, decay_w2, aaa_a0, aaa_a2, gate_g2,
                        key_k, key_a, bonus_r_k, lnx_w, lnx_b, w_out):
    zero = jnp.zeros((DECAY_LORA, D_RWKV), F32)
    wwa = jnp.concatenate([jnp.concatenate([decay_w2[l], zero], axis=1),
                           jnp.concatenate([zero, aaa_a2[l]], axis=1)], axis=0).astype(BF16)
    vecs = jnp.stack([decay_w0[l], aaa_a0[l], key_k[l], key_a[l], bonus_r_k[l], lnx_w[l], lnx_b[l],
                      jnp.zeros((D_RWKV,), F32)])
    return dict(w_in=w_in[l].astype(BF16), convw=conv_w[l], mu=shift_mu[l][None], vecs=vecs,
                wwa=wwa, g2=gate_g2[l].astype(BF16), w_out=w_out[l].astype(BF16))


def _row_tile(n, want):
    return want if n % want == 0 else n


def _tiles(n):
    return _row_tile(n, 512), _row_tile(n, 256)


def _mix(x2d, batch, seq, lw, conv0, shift0, wkv0):
    _, tm_proj = _tiles(batch * seq)
    p = _norm_proj(x2d, lw['norm'], lw['w_in'], tm_proj, PROJ_CHUNKS)
    merged, c, s, w = _mixer(p, conv0, shift0[:, None], wkv0, lw['convw'], lw['mu'], lw['vecs'],
                             lw['wwa'], lw['g2'], batch, seq)
    return merged, c, s[:, 0], w


def _lower_trunk(x, conv_st, shift_st, wkv_st, layers, ffn_norm, dense):
    batch, seq, d = x.shape
    n = batch * seq
    tm, _ = _tiles(n)
    x2d = x.reshape(n, d)
    merged, c0, s0, w0 = _mix(x2d, batch, seq, layers[0], conv_st[0], shift_st[0], wkv_st[0])
    wg, wu, wd = dense
    x2d = _ffn_dense(x2d, merged, layers[0]['w_out'], ffn_norm[0][None], wg, wu, wd, tm,
                     FFN_CHUNKS)
    merged, c1, s1, w1 = _mix(x2d, batch, seq, layers[1], conv_st[1], shift_st[1], wkv_st[1])
    return x2d, merged, jnp.stack([c0, c1]), jnp.stack([s0, s1]), jnp.stack([w0, w1])


def kernel(x_prompt, x_sample, state_conv, state_shift, state_wkv, mix_norm, w_in, conv_w, shift_mu,
           decay_w0, decay_w2, aaa_a0, aaa_a2, gate_g2, key_k, key_a, bonus_r_k, lnx_w, lnx_b, w_out,
           ffn_norm, ffn_w_gate, ffn_w_up, ffn_w_down, router_w, moe_w_gate, moe_w_up, moe_w_down,
           final_norm):
    depth = w_in.shape[0]
    assert depth == 2 and ffn_w_gate.shape[0] == 1 and moe_w_gate.shape[0] == 1
    layers = []
    for l in range(depth):
        lw = _prep_layer_weights(l, w_in, conv_w, shift_mu, decay_w0, decay_w2, aaa_a0, aaa_a2,
                                 gate_g2, key_k, key_a, bonus_r_k, lnx_w, lnx_b, w_out)
        lw['norm'] = mix_norm[l][None]
        layers.append(lw)
    dense = (ffn_w_gate[0].astype(BF16), ffn_w_up[0].astype(BF16), ffn_w_down[0].astype(BF16))
    rw = jnp.pad(router_w[0], ((0, 0), (0, LANES - N_EXPERTS)))
    moe = (rw, moe_w_gate[0].astype(BF16), moe_w_up[0].astype(BF16), moe_w_down[0].astype(BF16))

    b = x_prompt.shape[0]
    zero_conv = jnp.zeros((depth, b) + state_conv.shape[2:], state_conv.dtype)
    zero_shift = jnp.zeros((depth, b) + state_shift.shape[2:], state_shift.dtype)
    zero_wkv = jnp.zeros((depth, b) + state_wkv.shape[2:], state_wkv.dtype)
    run = functools.partial(_lower_trunk, layers=layers, ffn_norm=ffn_norm, dense=dense)
    x_s, m_s, conv_s, shift_s, wkv_s = run(x_sample, state_conv, state_shift, state_wkv)
    x_p, m_p, conv_p, shift_p, wkv_p = run(x_prompt, zero_conv, zero_shift, zero_wkv)
    y_p, y_s = _ffn_moe(x_p, m_p, x_s, m_s, layers[1]['w_out'], ffn_norm[1][None], *moe,
                        final_norm[None])
    return (y_p.reshape(x_prompt.shape), y_s.reshape(x_sample.shape),
            conv_p, shift_p, wkv_p, conv_s, shift_s, wkv_s)
```

```python
import functools

import jax
import jax.numpy as jnp
from jax import lax
from jax.experimental import pallas as pl
from jax.experimental.pallas import tpu as pltpu

F32 = jnp.float32
BF16 = jnp.bfloat16

D_MODEL = 1024
N_HEADS = 16
HEAD_DIM = 64
D_RWKV = N_HEADS * HEAD_DIM
D_CONV = 1024
CONV_W = 3
DECAY_LORA = 64
AAA_LORA = 64
GATE_LORA = 128
D_SHIFT = 3 * D_RWKV + DECAY_LORA + AAA_LORA + GATE_LORA
D_PROJ = 2 * D_MODEL + 3 * D_CONV + D_SHIFT
N_EXPERTS = 8
RMS_EPS = 1e-5
GN_EPS = 64e-5
L2_EPS = 1e-12

LANES = 128
SUBLANES = 8
CHUNK = 64
HEADS_PER_GROUP = LANES // HEAD_DIM
SEQS_PER_STEP = 2
CHUNKS_PER_STEP = 2
PIECES = 4
PROJ_CHUNKS = 11
FFN_CHUNKS = 11
ROUTER_SUBBLOCKS = 4
N_GROUPS = N_HEADS // HEADS_PER_GROUP
VMEM_LIMIT = 56 * 1024 * 1024

_ZA, _ZB, _CB, _CC, _CHH, _PS = (0, D_MODEL, 2 * D_MODEL, 2 * D_MODEL + D_CONV,
                                 2 * D_MODEL + 2 * D_CONV, 2 * D_MODEL + 3 * D_CONV)
_R, _K, _V, _WA, _G = 0, D_RWKV, 2 * D_RWKV, 3 * D_RWKV, 3 * D_RWKV + DECAY_LORA + AAA_LORA


def _params(semantics):
    return pltpu.CompilerParams(dimension_semantics=semantics, vmem_limit_bytes=VMEM_LIMIT)


def _rms(x, g):
    ms = jnp.mean(x * x, axis=-1, keepdims=True)
    return x * lax.rsqrt(ms + RMS_EPS) * g


def _split(x):
    hi = x.astype(BF16)
    lo = (x - hi.astype(F32)).astype(BF16)
    return hi, lo


_NN = (((1,), (0,)), ((), ()))
_NT = (((1,), (1,)), ((), ()))


def _dot(a, b, dims=_NN):
    return lax.dot_general(a, b, dims, preferred_element_type=F32)


def _mm1(a, b, dims=_NN):
    return _dot(a.astype(BF16), b.astype(BF16), dims)


def _mm3(a, b, dims=_NN):
    a1, a2 = _split(a)
    b1, b2 = _split(b)
    return _dot(a1, b1, dims) + (_dot(a1, b2, dims) + _dot(a2, b1, dims))


def _segment_sums(xs, seg_ones, exact=True):
    rows = xs[0].shape[0]
    if not exact:
        out = _dot(jnp.concatenate([x.astype(BF16) for x in xs], axis=0), seg_ones)
        return [out[i * rows:(i + 1) * rows] for i in range(len(xs))]
    parts = [half for x in xs for half in _split(x)]
    out = _dot(jnp.concatenate(parts, axis=0), seg_ones)
    return [out[2 * i * rows:(2 * i + 1) * rows] + out[(2 * i + 1) * rows:(2 * i + 2) * rows]
            for i in range(len(xs))]


def _norm_proj_kernel(x_ref, g_ref, w_ref, o_ref, *, n_col):
    xn = _rms(x_ref[...], g_ref[...]).astype(BF16)
    tn = w_ref.shape[1] // n_col
    for j in range(n_col):
        o_ref[:, j * tn:(j + 1) * tn] = _dot(xn, w_ref[:, j * tn:(j + 1) * tn])


def _norm_proj(x2d, g, w_bf16, tm, n_col):
    n, d = x2d.shape
    dp = w_bf16.shape[1]
    return pl.pallas_call(
        functools.partial(_norm_proj_kernel, n_col=n_col),
        out_shape=jax.ShapeDtypeStruct((n, dp), F32),
        grid=(n // tm,),
        in_specs=[pl.BlockSpec((tm, d), lambda i: (i, 0)),
                  pl.BlockSpec((1, d), lambda i: (0, 0)),
                  pl.BlockSpec((d, dp), lambda i: (0, 0), pipeline_mode=pl.Buffered(1))],
        out_specs=pl.BlockSpec((tm, dp), lambda i: (i, 0)),
        compiler_params=_params(("arbitrary",)),
        name="norm_proj",
    )(x2d, g, w_bf16)


def _block_diag(x, head_masks):
    return jnp.concatenate([jnp.where(m, x, 0.0) for m in head_masks], axis=0).astype(BF16)


def _wkv_prepare(r, k, v, kkn, a, ld, cum, head_masks):
    bd = functools.partial(_block_diag, head_masks=head_masks)
    cend = cum[CHUNK - 1:CHUNK, :]
    w_prev = jnp.exp(cum - ld)
    w_t = jnp.exp(cum)
    w_inv = 1.0 / w_t
    w_rest = jnp.exp(cend - cum)
    b = kkn * a
    return dict(
        lhs2=jnp.concatenate([-kkn * w_prev, r * w_t], axis=0).astype(BF16),
        rhs=jnp.concatenate([bd(b * w_inv), bd(k * w_inv)], axis=0),
        bd_v=bd(v), v=v, bk_h=jnp.concatenate([b * w_rest, k * w_rest], axis=0).astype(BF16),
        decay=jnp.exp(cend))


def _wkv_recurrence(ops, get_state, head_masks, causal2, bd_mask, out):
    groups = range(len(ops))
    bd = functools.partial(_block_diag, head_masks=head_masks)
    n_steps = CHUNK.bit_length() - 1
    s_prev = get_state()
    g_b, g_k, g_s = [], [], []
    for g in groups:
        gram = _mm1(ops[g]['lhs2'],
                    jnp.concatenate([ops[g]['rhs'], s_prev[g].astype(BF16)], axis=0), _NT)
        g_b.append(jnp.where(causal2, gram[:, :LANES], 0.0))
        g_k.append(jnp.where(causal2, gram[:, LANES:2 * LANES], 0.0))
        g_s.append(gram[:, 2 * LANES:])
    yield
    u = [g_s[g][:CHUNK] + _mm1(g_k[g][:CHUNK], ops[g]['bd_v']) for g in groups]
    pw = [g_b[g][:CHUNK] for g in groups]
    yield
    for i in range(n_steps):
        for g in groups:
            if i + 1 < n_steps:
                res = _mm1(pw[g], jnp.concatenate([bd(u[g]), bd(pw[g])], axis=1))
                u[g] = u[g] + res[:, :LANES]
                pw[g] = res[:, LANES:]
            else:
                u[g] = u[g] + _mm1(pw[g], bd(u[g]))
        yield
    y = []
    for g in groups:
        l_r = jnp.concatenate([g_b[g][CHUNK:], g_k[g][CHUNK:]], axis=1)
        y.append(g_s[g][CHUNK:] + _mm1(l_r, jnp.concatenate([bd(u[g]), ops[g]['bd_v']], axis=0)))
    out['y'] = y
    yield
    s_new = []
    for g in groups:
        uv_t = jnp.concatenate([u[g], ops[g]['v']], axis=0).T
        upd = _mm1(uv_t, ops[g]['bk_h'])
        s_new.append(s_prev[g] * ops[g]['decay'] + jnp.where(bd_mask, upd, 0.0))
    out['s_new'] = s_new
    yield


WKV_STAGES = 4 + CHUNK.bit_length() - 1


def _run_tasks(tasks):
    live = list(tasks)
    rnd = 0
    while live:
        for task in list(live):
            if task[0] <= rnd:
                try:
                    next(task[1])
                except StopIteration:
                    live.remove(task)
        rnd += 1


def _mixer_kernel(p_ref, conv0_ref, shift0_ref, wkv0_ref, convw_ref, mu_ref, vec_ref, wwa_ref,
                  g2_ref, merged_ref, nconv_ref, nshift_ref, nwkv_ref, cbuf, sbuf, state,
                  *, n_steps, n_seq, n_ch):
    t = pl.program_id(1)
    seqs = range(n_seq)
    groups = range(N_GROUPS)
    pairs = [(s, g) for s in seqs for g in groups]
    n_pairs = len(pairs)
    rows = n_ch * CHUNK

    @pl.when(t == 0)
    def _():
        zero = jnp.zeros((HEAD_DIM, HEAD_DIM), F32)
        for s in seqs:
            cbuf[s, 0:SUBLANES, :] = jnp.zeros((SUBLANES, D_CONV), F32)
            cbuf[s, SUBLANES - (CONV_W - 1):SUBLANES, :] = conv0_ref[s]
            sbuf[s, 0:SUBLANES, :] = jnp.zeros((SUBLANES, D_SHIFT), F32)
            sbuf[s, SUBLANES - 1:SUBLANES, :] = shift0_ref[s]
            for g in groups:
                h0 = g * HEADS_PER_GROUP
                state[s * N_GROUPS + g] = jnp.concatenate(
                    [jnp.concatenate([wkv0_ref[s, h0 + h] if hh == h else zero
                                      for hh in range(HEADS_PER_GROUP)], axis=1)
                     for h in range(HEADS_PER_GROUP)], axis=0)

    w0, a0, k_k, k_a, r_k, lnx_w, lnx_b = (vec_ref[i:i + 1, :] for i in range(7))
    lane = lax.broadcasted_iota(jnp.int32, (CHUNK, LANES), 1)
    lane_r = lax.broadcasted_iota(jnp.int32, (rows, LANES), 1)
    row = lax.broadcasted_iota(jnp.int32, (rows, rows), 0)
    col = lax.broadcasted_iota(jnp.int32, (rows, rows), 1)
    chunk_shift = CHUNK.bit_length() - 1
    tri = jnp.where((col <= row) & ((col >> chunk_shift) == (row >> chunk_shift)),
                    1.0, 0.0).astype(BF16)
    head_shift = HEAD_DIM.bit_length() - 1
    head_masks = [(lane >> head_shift) == h for h in range(HEADS_PER_GROUP)]
    row2 = lax.broadcasted_iota(jnp.int32, (2 * CHUNK, LANES), 0)
    lane2 = lax.broadcasted_iota(jnp.int32, (2 * CHUNK, LANES), 1)
    causal2 = (lane2 & (HEAD_DIM - 1)) < (row2 & (CHUNK - 1)) + (row2 >> chunk_shift)
    rowl = lax.broadcasted_iota(jnp.int32, (LANES, LANES), 0)
    lanel = lax.broadcasted_iota(jnp.int32, (LANES, LANES), 1)
    bd_mask = (rowl >> head_shift) == (lanel >> head_shift)
    seg_ones = jnp.where(bd_mask, 1.0, 0.0).astype(BF16)
    sls = [slice(g * LANES, (g + 1) * LANES) for g in groups]

    c0 = SUBLANES
    y_a, gate, new_conv, new_shift = [], [], [], []
    r, k, v, a, ld, cum = [], [], [], [], [], []
    for s in seqs:
        ch = p_ref[s, :, _CC:_CC + D_CONV] * p_ref[s, :, _CHH:_CHH + D_CONV]
        cbuf[s, c0:c0 + rows, :] = ch
        conv = (cbuf[s, c0 - 2:c0 - 2 + rows, :] * convw_ref[0:1, :]
                + cbuf[s, c0 - 1:c0 - 1 + rows, :] * convw_ref[1:2, :]
                + ch * convw_ref[2:3, :])
        y_a.append(jax.nn.sigmoid(p_ref[s, :, _ZA:_ZA + D_MODEL])
                   * (p_ref[s, :, _CB:_CB + D_CONV] * conv))
        new_conv.append(cbuf[s, c0 + rows - (CONV_W - 1):c0 + rows, :])
        cbuf[s, 0:SUBLANES, :] = cbuf[s, rows:rows + SUBLANES, :]

        ps = p_ref[s, :, _PS:_PS + D_SHIFT]
        sbuf[s, c0:c0 + rows, :] = ps
        prev = sbuf[s, c0 - 1:c0 - 1 + rows, :]
        xm = ps + (prev - ps) * mu_ref[...]
        new_shift.append(sbuf[s, c0 + rows - 1:c0 + rows, :])
        sbuf[s, 0:SUBLANES, :] = sbuf[s, rows:rows + SUBLANES, :]

        wa_in = xm[:, _WA:_WA + LANES]
        wa_in = jnp.where(lane_r < DECAY_LORA, jnp.tanh(wa_in), wa_in)
        wa = _mm1(wa_in, wwa_ref[...])
        gate.append(_mm1(jax.nn.sigmoid(xm[:, _G:_G + GATE_LORA]), g2_ref[...]))
        ld_all = -jnp.exp(-0.5) * jax.nn.sigmoid(w0 + wa[:, :D_RWKV])
        a_all = jax.nn.sigmoid(a0 + wa[:, D_RWKV:])
        ld1, ld2 = _split(ld_all)
        cum_all = _dot(tri, ld1) + _dot(tri, ld2)
        r.append(xm[:, _R:_R + D_RWKV])
        k.append(xm[:, _K:_K + D_RWKV])
        v.append(xm[:, _V:_V + D_RWKV])
        a.append(a_all)
        ld.append(ld_all)
        cum.append(cum_all)

    states = [[state[i] for i in range(n_pairs)]]
    pieces = [list(range(j, n_pairs, PIECES)) for j in range(PIECES)]
    ops = [[None] * n_pairs for _ in range(n_ch)]
    bonus = [[None] * n_pairs for _ in range(n_ch)]
    results = [dict() for _ in range(n_ch)]

    def prepare(c, idx):
        rs = slice(c * CHUNK, (c + 1) * CHUNK)
        cut = lambda xs, i: xs[pairs[i][0]][rs, sls[pairs[i][1]]]
        kk = [cut(k, i) * k_k[:, sls[pairs[i][1]]] for i in idx]
        kf = [cut(k, i) * (1.0 + (cut(a, i) - 1.0) * k_a[:, sls[pairs[i][1]]]) for i in idx]
        sums = _segment_sums([x * x for x in kk]
                             + [cut(r, i) * kf[n] * r_k[:, sls[pairs[i][1]]]
                                for n, i in enumerate(idx)], seg_ones, exact=False)
        yield
        for n, i in enumerate(idx):
            kkn = kk[n] / jnp.maximum(jnp.sqrt(sums[n]), L2_EPS)
            bonus[c][i] = sums[len(idx) + n] * cut(v, i)
            ops[c][i] = _wkv_prepare(cut(r, i), kf[n], cut(v, i), kkn, cut(a, i), cut(ld, i),
                                     cut(cum, i), head_masks)

    def recur(c):
        yield from _wkv_recurrence(ops[c], lambda: states[c], head_masks, causal2, bd_mask,
                                   results[c])
        states.append(results[c]['s_new'])

    def finish(c, idx):
        rs = slice(c * CHUNK, (c + 1) * CHUNK)
        y = [results[c]['y'][i] for i in idx]
        mean = [m * (1.0 / HEAD_DIM) for m in _segment_sums(y, seg_ones)]
        yield
        dev = [y[n] - mean[n] for n in range(len(idx))]
        var = [m * (1.0 / HEAD_DIM) for m in _segment_sums([d * d for d in dev], seg_ones)]
        yield
        for n, i in enumerate(idx):
            s, g = pairs[i]
            sl = sls[g]
            yn = dev[n] * lax.rsqrt(var[n] + GN_EPS) * lnx_w[:, sl] + lnx_b[:, sl]
            y_b = (yn + bonus[c][i]) * gate[s][rs, sl]
            z_b = p_ref[s, rs, _ZB + g * LANES:_ZB + (g + 1) * LANES]
            merged_ref[s, rs, sl] = (y_a[s][rs, sl] + jax.nn.sigmoid(z_b) * y_b).astype(merged_ref.dtype)

    base = lambda c: 2 + WKV_STAGES * c
    tasks = []
    for c in range(n_ch):
        for j, idx in enumerate(pieces):
            tasks.append((0 if c == 0 else base(c - 1) + 2 * j, prepare(c, idx)))
    for c in range(n_ch):
        tasks.append((base(c), recur(c)))
        for j, idx in enumerate(pieces):
            tasks.append((base(c + 1) + (2 * j if c + 1 < n_ch else 0), finish(c, idx)))
    _run_tasks(sorted(tasks, key=lambda task: task[0]))
    for i in range(n_pairs):
        state[i] = states[n_ch][i]

    @pl.when(t == n_steps - 1)
    def _():
        for s in seqs:
            nconv_ref[s] = new_conv[s]
            nshift_ref[s] = new_shift[s]
            for h in range(N_HEADS):
                o = (h % HEADS_PER_GROUP) * HEAD_DIM
                nwkv_ref[s, h] = state[s * N_GROUPS + h // HEADS_PER_GROUP,
                                       o:o + HEAD_DIM, o:o + HEAD_DIM]


def _mixer(p2d, conv0, shift0, wkv0, convw, mu, vecs, wwa, g2, batch, seq):
    n_seq = SEQS_PER_STEP if batch % SEQS_PER_STEP == 0 else 1
    n_ch = CHUNKS_PER_STEP if seq % (CHUNKS_PER_STEP * CHUNK) == 0 else 1
    rows = n_ch * CHUNK
    n_steps = seq // rows
    kern = functools.partial(_mixer_kernel, n_steps=n_steps, n_seq=n_seq, n_ch=n_ch)
    const = lambda b, t: (0, 0)
    per_b3 = lambda b, t: (b, 0, 0)
    per_b4 = lambda b, t: (b, 0, 0, 0)
    merged, conv, shift, wkv = pl.pallas_call(
        kern,
        out_shape=(jax.ShapeDtypeStruct((batch, seq, D_MODEL), BF16),
                   jax.ShapeDtypeStruct((batch, CONV_W - 1, D_CONV), F32),
                   jax.ShapeDtypeStruct((batch, 1, D_SHIFT), F32),
                   jax.ShapeDtypeStruct((batch, N_HEADS, HEAD_DIM, HEAD_DIM), F32)),
        grid=(batch // n_seq, n_steps),
        in_specs=[pl.BlockSpec((n_seq, rows, D_PROJ), lambda b, t: (b, t, 0)),
                  pl.BlockSpec((n_seq, CONV_W - 1, D_CONV), per_b3),
                  pl.BlockSpec((n_seq, 1, D_SHIFT), per_b3),
                  pl.BlockSpec((n_seq, N_HEADS, HEAD_DIM, HEAD_DIM), per_b4),
                  pl.BlockSpec((CONV_W, D_CONV), const),
                  pl.BlockSpec((1, D_SHIFT), const),
                  pl.BlockSpec((SUBLANES, D_RWKV), const),
                  pl.BlockSpec((LANES, 2 * D_RWKV), const),
                  pl.BlockSpec((GATE_LORA, D_RWKV), const)],
        out_specs=(pl.BlockSpec((n_seq, rows, D_MODEL), lambda b, t: (b, t, 0)),
                   pl.BlockSpec((n_seq, CONV_W - 1, D_CONV), per_b3),
                   pl.BlockSpec((n_seq, 1, D_SHIFT), per_b3),
                   pl.BlockSpec((n_seq, N_HEADS, HEAD_DIM, HEAD_DIM), per_b4)),
        scratch_shapes=[pltpu.VMEM((n_seq, rows + SUBLANES, D_CONV), F32),
                        pltpu.VMEM((n_seq, rows + SUBLANES, D_SHIFT), F32),
                        pltpu.VMEM((n_seq * N_GROUPS, LANES, LANES), F32)],
        compiler_params=_params(("arbitrary", "arbitrary")),
        name="mixer",
    )(p2d.reshape(batch, seq, D_PROJ), conv0, shift0, wkv0, convw, mu, vecs, wwa, g2)
    return merged.reshape(batch * seq, D_MODEL), conv, shift, wkv


def _silu_mul(gate, up):
    return gate * jax.nn.sigmoid(gate) * up


def _ffn_dense_kernel(x_ref, m_ref, wo_ref, g_ref, wg_ref, wu_ref, wd_ref, o_ref, *, n_f):
    x1 = x_ref[...] + _dot(m_ref[...], wo_ref[...])
    h = _rms(x1, g_ref[...]).astype(BF16)
    tf = wg_ref.shape[1] // n_f
    acc = x1
    for j in range(n_f):
        act = _silu_mul(_dot(h, wg_ref[:, j * tf:(j + 1) * tf]),
                        _dot(h, wu_ref[:, j * tf:(j + 1) * tf])).astype(BF16)
        acc = acc + _dot(act, wd_ref[j * tf:(j + 1) * tf, :])
    o_ref[...] = acc


def _ffn_dense(x2d, merged, wo, g, wg, wu, wd, tm, n_f):
    n, d = x2d.shape
    f = wg.shape[1]
    row = lambda i: (i, 0)
    resident = lambda shape: pl.BlockSpec(shape, lambda i: (0, 0), pipeline_mode=pl.Buffered(1))
    return pl.pallas_call(
        functools.partial(_ffn_dense_kernel, n_f=n_f),
        out_shape=jax.ShapeDtypeStruct((n, d), F32),
        grid=(n // tm,),
        in_specs=[pl.BlockSpec((tm, d), row), pl.BlockSpec((tm, d), row),
                  resident((d, d)), pl.BlockSpec((1, d), lambda i: (0, 0)),
                  resident((d, f)), resident((d, f)), resident((f, d))],
        out_specs=pl.BlockSpec((tm, d), row),
        compiler_params=_params(("arbitrary",)),
        name="ffn_dense",
    )(x2d, merged, wo, g, wg, wu, wd)


def _top2(logits):
    lane_i = lax.broadcasted_iota(jnp.int32, logits.shape, 1)
    lane = lane_i.astype(F32)
    neg = jnp.float32(-jnp.inf)
    lg = jnp.where(lane_i < N_EXPERTS, logits, neg)
    m1 = jnp.max(lg, axis=-1, keepdims=True)
    i1 = jnp.min(jnp.where(lg == m1, lane, float(LANES)), axis=-1, keepdims=True)
    lg2 = jnp.where(lane == i1, neg, lg)
    m2 = jnp.max(lg2, axis=-1, keepdims=True)
    i2 = jnp.min(jnp.where(lg2 == m2, lane, float(LANES)), axis=-1, keepdims=True)
    e2 = jnp.exp(m2 - m1)
    den = 1.0 + e2
    ids = jnp.where(lane_i == 0, i1, jnp.where(lane_i == 1, i2, 0.0)).astype(jnp.int32)
    wts = jnp.where(lane_i == 0, 1.0 / den, jnp.where(lane_i == 1, e2 / den, 0.0))
    return ids, wts


def _router_kernel(xa_ref, ma_ref, xb_ref, mb_ref, wo_ref, g_ref, rw_ref, x1_ref, ids_ref, wts_ref,
                   *, blocks_a, n_sub):
    first = pl.program_id(0) < blocks_a
    rows = x1_ref.shape[0] // n_sub
    subs = [slice(i * rows, (i + 1) * rows) for i in range(n_sub)]
    x1 = [jnp.where(first, xa_ref[sl, :], xb_ref[sl, :])
          + _dot(jnp.where(first, ma_ref[sl, :], mb_ref[sl, :]), wo_ref[...]) for sl in subs]
    logits = [_mm3(_rms(v, g_ref[...]), rw_ref[...]) for v in x1]
    for sl, v, lg in zip(subs, x1, logits):
        x1_ref[sl, :] = v
        ids_ref[sl, :], wts_ref[sl, :] = _top2(lg)


def _router(xa, ma, xb, mb, wo, g, rw, tm):
    (na, d), nb = xa.shape, xb.shape[0]
    blocks_a = na // tm
    n = na + nb
    row = lambda i: (i, 0)
    row_a = lambda i: (jnp.minimum(i, blocks_a - 1), 0)
    row_b = lambda i: (jnp.maximum(i - blocks_a, 0), 0)
    const = lambda i: (0, 0)
    return pl.pallas_call(
        functools.partial(_router_kernel, blocks_a=blocks_a, n_sub=ROUTER_SUBBLOCKS),
        out_shape=(jax.ShapeDtypeStruct((n, d), F32),
                   jax.ShapeDtypeStruct((n, LANES), jnp.int32),
                   jax.ShapeDtypeStruct((n, LANES), F32)),
        grid=(n // tm,),
        in_specs=[pl.BlockSpec((tm, d), row_a), pl.BlockSpec((tm, d), row_a),
                  pl.BlockSpec((tm, d), row_b), pl.BlockSpec((tm, d), row_b),
                  pl.BlockSpec((d, d), const), pl.BlockSpec((1, d), const),
                  pl.BlockSpec((d, LANES), const)],
        out_specs=(pl.BlockSpec((tm, d), row), pl.BlockSpec((tm, LANES), row),
                   pl.BlockSpec((tm, LANES), row)),
        compiler_params=_params(("arbitrary",)),
        name="router",
    )(xa, ma, xb, mb, wo, g, rw)


def _experts_kernel(te_ref, nv_ref, inv_prev_ref, inv_ref, inv_next_ref, x1_hbm, g_ref,
                    wg_ref, wu_ref, wd_ref, y_hbm, xbuf, obuf, gather_sem, scatter_sem,
                    *, tm, n_tok, f_split):
    t = pl.program_id(0)
    n_used = nv_ref[0]
    cur = t & 1
    nxt = 1 - cur

    def gather_row(idx_ref, buf, r):
        tok = jnp.maximum(idx_ref[0, 0, r], 0) >> 1
        pltpu.make_async_copy(x1_hbm.at[pl.ds(tok, 1)], xbuf.at[buf, pl.ds(r, 1)],
                              gather_sem.at[buf]).start()

    def gather_loop(idx_ref, buf):
        def body(r, c):
            gather_row(idx_ref, buf, r)
            return c
        lax.fori_loop(0, tm, body, 0, unroll=8)

    def wait_gather(buf):
        pltpu.make_async_copy(x1_hbm.at[pl.ds(0, tm)], xbuf.at[buf], gather_sem.at[buf]).wait()

    def scatter_row(idx_ref, buf, r, real):
        a = idx_ref[0, 0, r]
        row = jnp.where((a >= 0) & real, (a & 1) * n_tok + (a >> 1), 2 * n_tok + r)
        pltpu.make_async_copy(obuf.at[buf, pl.ds(r, 1)], y_hbm.at[pl.ds(row, 1)], scatter_sem).start()

    def wait_scatter(buf):
        pltpu.make_async_copy(obuf.at[buf], y_hbm.at[pl.ds(0, tm)], scatter_sem).wait()

    @pl.when(t == 0)
    def _():
        obuf[...] = jnp.zeros_like(obuf)
        gather_loop(inv_ref, 0)

    wait_gather(cur)

    @pl.when(t < n_used)
    def _():
        h = _rms(xbuf[cur], g_ref[...]).astype(BF16)
        fs = wg_ref.shape[2] // f_split
        rows_per = -(-tm // max(1, (2 * f_split) // 3))
        acc = None
        for i in range(f_split):
            act = _silu_mul(_dot(h, wg_ref[0, :, i * fs:(i + 1) * fs]),
                            _dot(h, wu_ref[0, :, i * fs:(i + 1) * fs])).astype(BF16)
            part = _dot(act, wd_ref[0, i * fs:(i + 1) * fs, :])
            acc = part if acc is None else acc + part
            for r in range(i * rows_per, min((i + 1) * rows_per, tm)):
                gather_row(inv_next_ref, nxt, r)
                scatter_row(inv_prev_ref, nxt, r, t > 0)
        wait_scatter(nxt)
        obuf[cur] = acc

        @pl.when(t == n_used - 1)
        def _():
            def body(r, c):
                scatter_row(inv_ref, cur, r, True)
                return c
            lax.fori_loop(0, tm, body, 0, unroll=8)
            wait_scatter(cur)

    @pl.when(t >= n_used)
    def _():
        gather_loop(inv_next_ref, nxt)

    @pl.when(t == pl.num_programs(0) - 1)
    def _():
        wait_gather(nxt)


def _experts(x1, g, wg, wu, wd, tile_expert, n_used, inv, tm):
    n, d = x1.shape
    n_tiles = tile_expert.shape[0]
    f = wg.shape[2]
    expert_w = lambda t, te, nv: (jnp.minimum(te[t], N_EXPERTS - 1), 0, 0)
    single = pl.Buffered(1)
    inv3 = inv.reshape(n_tiles, 1, tm)
    idx_spec = lambda shift: pl.BlockSpec(
        (1, 1, tm), lambda t, te, nv: (jnp.clip(t + shift, 0, n_tiles - 1), 0, 0),
        memory_space=pltpu.SMEM)
    grid_spec = pltpu.PrefetchScalarGridSpec(
        num_scalar_prefetch=2,
        grid=(n_tiles,),
        in_specs=[idx_spec(-1), idx_spec(0), idx_spec(1),
                  pl.BlockSpec(memory_space=pl.ANY),
                  pl.BlockSpec((1, d), lambda t, te, nv: (0, 0)),
                  pl.BlockSpec((1, d, f), expert_w, pipeline_mode=single),
                  pl.BlockSpec((1, d, f), expert_w, pipeline_mode=single),
                  pl.BlockSpec((1, f, d), expert_w, pipeline_mode=single)],
        out_specs=pl.BlockSpec(memory_space=pl.ANY),
        scratch_shapes=[pltpu.VMEM((2, tm, d), F32), pltpu.VMEM((2, tm, d), F32),
                        pltpu.SemaphoreType.DMA((2,)), pltpu.SemaphoreType.DMA(())])
    return pl.pallas_call(
        functools.partial(_experts_kernel, tm=tm, n_tok=n, f_split=f // (2 * LANES)),
        out_shape=jax.ShapeDtypeStruct((2 * n + tm, d), F32),
        grid_spec=grid_spec,
        compiler_params=_params(("arbitrary",)),
        name="experts",
    )(tile_expert, n_used, inv3, inv3, inv3, x1, g, wg, wu, wd)


def _combine_kernel(x1_ref, y0_ref, y1_ref, wts_ref, fin_ref, oa_ref, ob_ref, *, blocks_a):
    w = wts_ref[...]
    moe = w[:, 0:1] * y0_ref[...] + w[:, 1:2] * y1_ref[...]
    out = _rms(x1_ref[...] + moe, fin_ref[...])
    first = pl.program_id(0) < blocks_a

    @pl.when(first)
    def _():
        oa_ref[...] = out

    @pl.when(jnp.logical_not(first))
    def _():
        ob_ref[...] = out


def _combine(x1, y2, wts, fin, na, tm):
    n, d = x1.shape
    n_blocks = n // tm
    blocks_a = na // tm
    row = lambda i: (i, 0)
    return pl.pallas_call(
        functools.partial(_combine_kernel, blocks_a=blocks_a),
        out_shape=(jax.ShapeDtypeStruct((na, d), F32), jax.ShapeDtypeStruct((n - na, d), F32)),
        grid=(n_blocks,),
        in_specs=[pl.BlockSpec((tm, d), row),
                  pl.BlockSpec((tm, d), row),
                  pl.BlockSpec((tm, d), lambda i: (i + n_blocks, 0)),
                  pl.BlockSpec((tm, LANES), row),
                  pl.BlockSpec((1, d), lambda i: (0, 0))],
        out_specs=(pl.BlockSpec((tm, d), lambda i: (jnp.minimum(i, blocks_a - 1), 0)),
                   pl.BlockSpec((tm, d), lambda i: (jnp.maximum(i - blocks_a, 0), 0))),
        compiler_params=_params(("arbitrary",)),
        name="combine",
    )(x1, y2, y2, wts, fin)


def _routing_plan(ids, tm):
    n = ids.shape[0]
    e_flat = ids[:, :2].reshape(-1)
    onehot = (e_flat[:, None] == jnp.arange(N_EXPERTS, dtype=jnp.int32)[None, :]).astype(jnp.int32)
    csum = jnp.cumsum(onehot, axis=0)
    counts = csum[-1]
    padded = (counts + tm - 1) // tm * tm
    ends = jnp.cumsum(padded)
    pos = jnp.sum(onehot * (csum - 1 + (ends - padded)[None, :]), axis=1)
    total = 2 * n + N_EXPERTS * tm
    inv = jnp.full((total,), -1, jnp.int32).at[pos].set(
        jnp.arange(2 * n, dtype=jnp.int32), unique_indices=True, mode='promise_in_bounds')
    tile_start = jnp.arange(total // tm, dtype=jnp.int32) * tm
    tile_expert = jnp.sum((tile_start[:, None] >= ends[None, :]).astype(jnp.int32), axis=1)
    return tile_expert, (ends[-1:] // tm).astype(jnp.int32), inv


def _common_tile(na, nb, want):
    tm = want
    while na % tm or nb % tm:
        tm //= 2
    return tm


def _ffn_moe(xa, ma, xb, mb, wo, g, rw, wg, wu, wd, fin):
    na, nb = xa.shape[0], xb.shape[0]
    n = na + nb
    big_experts = 2 * n >= 4 * 512 * N_EXPERTS
    tm_expert = _row_tile(2 * n, 512 if big_experts else 256)
    x1, ids, wts = _router(xa, ma, xb, mb, wo, g, rw, _common_tile(na, nb, 1024))
    tile_expert, n_used, inv = _routing_plan(ids, tm_expert)
    y2 = _experts(x1, g, wg, wu, wd, tile_expert, n_used, inv, tm_expert)
    return _combine(x1, y2, wts, fin, na, _common_tile(na, nb, 512))


def _prep_layer_weights(l, w_in, conv_w, shift_mu, decay_w0, decay_w2, aaa_a0, aaa_a2, gate_g2,
                        key_k, key_a, bonus_r_k, lnx_w, lnx_b, w_out):
    zero = jnp.zeros((DECAY_LORA, D_RWKV), F32)
    wwa = jnp.concatenate([jnp.concatenate([decay_w2[l], zero], axis=1),
                           jnp.concatenate([zero, aaa_a2[l]], axis=1)], axis=0).astype(BF16)
    vecs = jnp.stack([decay_w0[l], aaa_a0[l], key_k[l], key_a[l], bonus_r_k[l], lnx_w[l], lnx_b[l],
                      jnp.zeros((D_RWKV,), F32)])
    return dict(w_in=w_in[l].astype(BF16), convw=conv_w[l], mu=shift_mu[l][None], vecs=vecs,
                wwa=wwa, g2=gate_g2[l].astype(BF16), w_out=w_out[l].astype(BF16))


def _row_tile(n, want):
    return want if n % want == 0 else n


def _tiles(n):
    return _row_tile(n, 512), _row_tile(n, 256)


def _mix(x2d, batch, seq, lw, conv0, shift0, wkv0):
    _, tm_proj = _tiles(batch * seq)
    p = _norm_proj(x2d, lw['norm'], lw['w_in'], tm_proj, PROJ_CHUNKS)
    merged, c, s, w = _mixer(p, conv0, shift0[:, None], wkv0, lw['convw'], lw['mu'], lw['vecs'],
                             lw['wwa'], lw['g2'], batch, seq)
    return merged, c, s[:, 0], w


def _lower_trunk(x, conv_st, shift_st, wkv_st, layers, ffn_norm, dense):
    batch, seq, d = x.shape
    n = batch * seq
    tm, _ = _tiles(n)
    x2d = x.reshape(n, d)
    merged, c0, s0, w0 = _mix(x2d, batch, seq, layers[0], conv_st[0], shift_st[0], wkv_st[0])
    wg, wu, wd = dense
    x2d = _ffn_dense(x2d, merged, layers[0]['w_out'], ffn_norm[0][None], wg, wu, wd, tm,
                     FFN_CHUNKS)
    merged, c1, s1, w1 = _mix(x2d, batch, seq, layers[1], conv_st[1], shift_st[1], wkv_st[1])
    return x2d, merged, jnp.stack([c0, c1]), jnp.stack([s0, s1]), jnp.stack([w0, w1])


def kernel(x_prompt, x_sample, state_conv, state_shift, state_wkv, mix_norm, w_in, conv_w, shift_mu,
           decay_w0, decay_w2, aaa_a0, aaa_a2, gate_g2, key_k, key_a, bonus_r_k, lnx_w, lnx_b, w_out,
           ffn_norm, ffn_w_gate, ffn_w_up, ffn_w_down, router_w, moe_w_gate, moe_w_up, moe_w_down,
           final_norm):
    depth = w_in.shape[0]
    assert depth == 2 and ffn_w_gate.shape[0] == 1 and moe_w_gate.shape[0] == 1
    layers = []
    for l in range(depth):
        lw = _prep_layer_weights(l, w_in, conv_w, shift_mu, decay_w0, decay_w2, aaa_a0, aaa_a2,
                                 gate_g2, key_k, key_a, bonus_r_k, lnx_w, lnx_b, w_out)
        lw['norm'] = mix_norm[l][None]
        layers.append(lw)
    dense = (ffn_w_gate[0].astype(BF16), ffn_w_up[0].astype(BF16), ffn_w_down[0].astype(BF16))
    rw = jnp.pad(router_w[0], ((0, 0), (0, LANES - N_EXPERTS)))
    moe = (rw, moe_w_gate[0].astype(BF16), moe_w_up[0].astype(BF16), moe_w_down[0].astype(BF16))

    b = x_prompt.shape[0]
    zero_conv = jnp.zeros((depth, b) + state_conv.shape[2:], state_conv.dtype)
    zero_shift = jnp.zeros((depth, b) + state_shift.shape[2:], state_shift.dtype)
    zero_wkv = jnp.zeros((depth, b) + state_wkv.shape[2:], state_wkv.dtype)
    run = functools.partial(_lower_trunk, layers=layers, ffn_norm=ffn_norm, dense=dense)
    x_s, m_s, conv_s, shift_s, wkv_s = run(x_sample, state_conv, state_shift, state_wkv)
    x_p, m_p, conv_p, shift_p, wkv_p = run(x_prompt, zero_conv, zero_shift, zero_wkv)
    y_p, y_s = _ffn_moe(x_p, m_p, x_s, m_s, layers[1]['w_out'], ffn_norm[1][None], *moe,
                        final_norm[None])
    return (y_p.reshape(x_prompt.shape), y_s.reshape(x_sample.shape),
            conv_p, shift_p, wkv_p, conv_s, shift_s, wkv_s)
```

```python
import functools

import jax
import jax.numpy as jnp
from jax import lax
from jax.experimental import pallas as pl
from jax.experimental.pallas import tpu as pltpu

F32 = jnp.float32
BF16 = jnp.bfloat16

D_MODEL = 1024
N_HEADS = 16
HEAD_DIM = 64
D_RWKV = N_HEADS * HEAD_DIM
D_CONV = 1024
CONV_W = 3
DECAY_LORA = 64
AAA_LORA = 64
GATE_LORA = 128
D_SHIFT = 3 * D_RWKV + DECAY_LORA + AAA_LORA + GATE_LORA
D_PROJ = 2 * D_MODEL + 3 * D_CONV + D_SHIFT
N_EXPERTS = 8
RMS_EPS = 1e-5
GN_EPS = 64e-5
L2_EPS = 1e-12

LANES = 128
SUBLANES = 8
CHUNK = 64
HEADS_PER_GROUP = LANES // HEAD_DIM
SEQS_PER_STEP = 2
CHUNKS_PER_STEP = 2
PIECES = 4
PROJ_CHUNKS = 11
FFN_CHUNKS = 11
ROUTER_SUBBLOCKS = 4
N_GROUPS = N_HEADS // HEADS_PER_GROUP
VMEM_LIMIT = 56 * 1024 * 1024

_ZA, _ZB, _CB, _CC, _CHH, _PS = (0, D_MODEL, 2 * D_MODEL, 2 * D_MODEL + D_CONV,
                                 2 * D_MODEL + 2 * D_CONV, 2 * D_MODEL + 3 * D_CONV)
_R, _K, _V, _WA, _G = 0, D_RWKV, 2 * D_RWKV, 3 * D_RWKV, 3 * D_RWKV + DECAY_LORA + AAA_LORA


def _params(semantics):
    return pltpu.CompilerParams(dimension_semantics=semantics, vmem_limit_bytes=VMEM_LIMIT)


def _rms(x, g):
    ms = jnp.mean(x * x, axis=-1, keepdims=True)
    return x * lax.rsqrt(ms + RMS_EPS) * g


def _split(x):
    hi = x.astype(BF16)
    lo = (x - hi.astype(F32)).astype(BF16)
    return hi, lo


_NN = (((1,), (0,)), ((), ()))
_NT = (((1,), (1,)), ((), ()))


def _dot(a, b, dims=_NN):
    return lax.dot_general(a, b, dims, preferred_element_type=F32)


def _mm1(a, b, dims=_NN):
    return _dot(a.astype(BF16), b.astype(BF16), dims)


def _mm3(a, b, dims=_NN):
    a1, a2 = _split(a)
    b1, b2 = _split(b)
    return _dot(a1, b1, dims) + (_dot(a1, b2, dims) + _dot(a2, b1, dims))


def _segment_sums(xs, seg_ones, exact=True):
    rows = xs[0].shape[0]
    if not exact:
        out = _dot(jnp.concatenate([x.astype(BF16) for x in xs], axis=0), seg_ones)
        return [out[i * rows:(i + 1) * rows] for i in range(len(xs))]
    parts = [half for x in xs for half in _split(x)]
    out = _dot(jnp.concatenate(parts, axis=0), seg_ones)
    return [out[2 * i * rows:(2 * i + 1) * rows] + out[(2 * i + 1) * rows:(2 * i + 2) * rows]
            for i in range(len(xs))]


def _norm_proj_kernel(x_ref, g_ref, w_ref, o_ref, *, n_col):
    xn = _rms(x_ref[...], g_ref[...]).astype(BF16)
    tn = w_ref.shape[1] // n_col
    for j in range(n_col):
        o_ref[:, j * tn:(j + 1) * tn] = _dot(xn, w_ref[:, j * tn:(j + 1) * tn])


def _norm_proj(x2d, g, w_bf16, tm, n_col):
    n, d = x2d.shape
    dp = w_bf16.shape[1]
    return pl.pallas_call(
        functools.partial(_norm_proj_kernel, n_col=n_col),
        out_shape=jax.ShapeDtypeStruct((n, dp), F32),
        grid=(n // tm,),
        in_specs=[pl.BlockSpec((tm, d), lambda i: (i, 0)),
                  pl.BlockSpec((1, d), lambda i: (0, 0)),
                  pl.BlockSpec((d, dp), lambda i: (0, 0), pipeline_mode=pl.Buffered(1))],
        out_specs=pl.BlockSpec((tm, dp), lambda i: (i, 0)),
        compiler_params=_params(("arbitrary",)),
        name="norm_proj",
    )(x2d, g, w_bf16)


def _block_diag(x, head_masks):
    return jnp.concatenate([jnp.where(m, x, 0.0) for m in head_masks], axis=0).astype(BF16)


def _wkv_prepare(r, k, v, kkn, a, ld, cum, head_masks):
    bd = functools.partial(_block_diag, head_masks=head_masks)
    cend = cum[CHUNK - 1:CHUNK, :]
    w_prev = jnp.exp(cum - ld)
    w_t = jnp.exp(cum)
    w_inv = 1.0 / w_t
    w_rest = jnp.exp(cend - cum)
    b = kkn * a
    return dict(
        lhs2=jnp.concatenate([-kkn * w_prev, r * w_t], axis=0).astype(BF16),
        rhs=jnp.concatenate([bd(b * w_inv), bd(k * w_inv)], axis=0),
        bd_v=bd(v), v=v, bk_h=jnp.concatenate([b * w_rest, k * w_rest], axis=0).astype(BF16),
        decay=jnp.exp(cend))


def _wkv_recurrence(ops, get_state, head_masks, causal2, bd_mask, out):
    groups = range(len(ops))
    bd = functools.partial(_block_diag, head_masks=head_masks)
    n_steps = CHUNK.bit_length() - 1
    s_prev = get_state()
    g_b, g_k, g_s = [], [], []
    for g in groups:
        gram = _mm1(ops[g]['lhs2'],
                    jnp.concatenate([ops[g]['rhs'], s_prev[g].astype(BF16)], axis=0), _NT)
        g_b.append(jnp.where(causal2, gram[:, :LANES], 0.0))
        g_k.append(jnp.where(causal2, gram[:, LANES:2 * LANES], 0.0))
        g_s.append(gram[:, 2 * LANES:])
    yield
    u = [g_s[g][:CHUNK] + _mm1(g_k[g][:CHUNK], ops[g]['bd_v']) for g in groups]
    pw = [g_b[g][:CHUNK] for g in groups]
    yield
    for i in range(n_steps):
        for g in groups:
            if i + 1 < n_steps:
                res = _mm1(pw[g], jnp.concatenate([bd(u[g]), bd(pw[g])], axis=1))
                u[g] = u[g] + res[:, :LANES]
                pw[g] = res[:, LANES:]
            else:
                u[g] = u[g] + _mm1(pw[g], bd(u[g]))
        yield
    y = []
    for g in groups:
        l_r = jnp.concatenate([g_b[g][CHUNK:], g_k[g][CHUNK:]], axis=1)
        y.append(g_s[g][CHUNK:] + _mm1(l_r, jnp.concatenate([bd(u[g]), ops[g]['bd_v']], axis=0)))
    out['y'] = y
    yield
    s_new = []
    for g in groups:
        uv_t = jnp.concatenate([u[g], ops[g]['v']], axis=0).T
        upd = _mm1(uv_t, ops[g]['bk_h'])
        s_new.append(s_prev[g] * ops[g]['decay'] + jnp.where(bd_mask, upd, 0.0))
    out['s_new'] = s_new
    yield


WKV_STAGES = 4 + CHUNK.bit_length() - 1


def _run_tasks(tasks):
    live = list(tasks)
    rnd = 0
    while live:
        for task in list(live):
            if task[0] <= rnd:
                try:
                    next(task[1])
                except StopIteration:
                    live.remove(task)
        rnd += 1


def _mixer_kernel(p_ref, conv0_ref, shift0_ref, wkv0_ref, convw_ref, mu_ref, vec_ref, wwa_ref,
                  g2_ref, merged_ref, nconv_ref, nshift_ref, nwkv_ref, cbuf, sbuf, state,
                  *, n_steps, n_seq, n_ch):
    t = pl.program_id(1)
    seqs = range(n_seq)
    groups = range(N_GROUPS)
    pairs = [(s, g) for s in seqs for g in groups]
    n_pairs = len(pairs)
    rows = n_ch * CHUNK

    @pl.when(t == 0)
    def _():
        zero = jnp.zeros((HEAD_DIM, HEAD_DIM), F32)
        for s in seqs:
            cbuf[s, 0:SUBLANES, :] = jnp.zeros((SUBLANES, D_CONV), F32)
            cbuf[s, SUBLANES - (CONV_W - 1):SUBLANES, :] = conv0_ref[s]
            sbuf[s, 0:SUBLANES, :] = jnp.zeros((SUBLANES, D_SHIFT), F32)
            sbuf[s, SUBLANES - 1:SUBLANES, :] = shift0_ref[s]
            for g in groups:
                h0 = g * HEADS_PER_GROUP
                state[s * N_GROUPS + g] = jnp.concatenate(
                    [jnp.concatenate([wkv0_ref[s, h0 + h] if hh == h else zero
                                      for hh in range(HEADS_PER_GROUP)], axis=1)
                     for h in range(HEADS_PER_GROUP)], axis=0)

    w0, a0, k_k, k_a, r_k, lnx_w, lnx_b = (vec_ref[i:i + 1, :] for i in range(7))
    lane = lax.broadcasted_iota(jnp.int32, (CHUNK, LANES), 1)
    lane_r = lax.broadcasted_iota(jnp.int32, (rows, LANES), 1)
    row = lax.broadcasted_iota(jnp.int32, (rows, rows), 0)
    col = lax.broadcasted_iota(jnp.int32, (rows, rows), 1)
    chunk_shift = CHUNK.bit_length() - 1
    tri = jnp.where((col <= row) & ((col >> chunk_shift) == (row >> chunk_shift)),
                    1.0, 0.0).astype(BF16)
    head_shift = HEAD_DIM.bit_length() - 1
    head_masks = [(lane >> head_shift) == h for h in range(HEADS_PER_GROUP)]
    row2 = lax.broadcasted_iota(jnp.int32, (2 * CHUNK, LANES), 0)
    lane2 = lax.broadcasted_iota(jnp.int32, (2 * CHUNK, LANES), 1)
    causal2 = (lane2 & (HEAD_DIM - 1)) < (row2 & (CHUNK - 1)) + (row2 >> chunk_shift)
    rowl = lax.broadcasted_iota(jnp.int32, (LANES, LANES), 0)
    lanel = lax.broadcasted_iota(jnp.int32, (LANES, LANES), 1)
    bd_mask = (rowl >> head_shift) == (lanel >> head_shift)
    seg_ones = jnp.where(bd_mask, 1.0, 0.0).astype(BF16)
    sls = [slice(g * LANES, (g + 1) * LANES) for g in groups]

    c0 = SUBLANES
    y_a, gate, new_conv, new_shift = [], [], [], []
    r, k, v, a, ld, cum = [], [], [], [], [], []
    for s in seqs:
        ch = p_ref[s, :, _CC:_CC + D_CONV] * p_ref[s, :, _CHH:_CHH + D_CONV]
        cbuf[s, c0:c0 + rows, :] = ch
        conv = (cbuf[s, c0 - 2:c0 - 2 + rows, :] * convw_ref[0:1, :]
                + cbuf[s, c0 - 1:c0 - 1 + rows, :] * convw_ref[1:2, :]
                + ch * convw_ref[2:3, :])
        y_a.append(jax.nn.sigmoid(p_ref[s, :, _ZA:_ZA + D_MODEL])
                   * (p_ref[s, :, _CB:_CB + D_CONV] * conv))
        new_conv.append(cbuf[s, c0 + rows - (CONV_W - 1):c0 + rows, :])
        cbuf[s, 0:SUBLANES, :] = cbuf[s, rows:rows + SUBLANES, :]

        ps = p_ref[s, :, _PS:_PS + D_SHIFT]
        sbuf[s, c0:c0 + rows, :] = ps
        prev = sbuf[s, c0 - 1:c0 - 1 + rows, :]
        xm = ps + (prev - ps) * mu_ref[...]
        new_shift.append(sbuf[s, c0 + rows - 1:c0 + rows, :])
        sbuf[s, 0:SUBLANES, :] = sbuf[s, rows:rows + SUBLANES, :]

        wa_in = xm[:, _WA:_WA + LANES]
        wa_in = jnp.where(lane_r < DECAY_LORA, jnp.tanh(wa_in), wa_in)
        wa = _mm1(wa_in, wwa_ref[...])
        gate.append(_mm1(jax.nn.sigmoid(xm[:, _G:_G + GATE_LORA]), g2_ref[...]))
        ld_all = -jnp.exp(-0.5) * jax.nn.sigmoid(w0 + wa[:, :D_RWKV])
        a_all = jax.nn.sigmoid(a0 + wa[:, D_RWKV:])
        ld1, ld2 = _split(ld_all)
        cum_all = _dot(tri, ld1) + _dot(tri, ld2)
        r.append(xm[:, _R:_R + D_RWKV])
        k.append(xm[:, _K:_K + D_RWKV])
        v.append(xm[:, _V:_V + D_RWKV])
        a.append(a_all)
        ld.append(ld_all)
        cum.append(cum_all)

    states = [[state[i] for i in range(n_pairs)]]
    pieces = [list(range(j, n_pairs, PIECES)) for j in range(PIECES)]
    ops = [[None] * n_pairs for _ in range(n_ch)]
    bonus = [[None] * n_pairs for _ in range(n_ch)]
    results = [dict() for _ in range(n_ch)]

    def prepare(c, idx):
        rs = slice(c * CHUNK, (c + 1) * CHUNK)
        cut = lambda xs, i: xs[pairs[i][0]][rs, sls[pairs[i][1]]]
        kk = [cut(k, i) * k_k[:, sls[pairs[i][1]]] for i in idx]
        kf = [cut(k, i) * (1.0 + (cut(a, i) - 1.0) * k_a[:, sls[pairs[i][1]]]) for i in idx]
        sums = _segment_sums([x * x for x in kk]
                             + [cut(r, i) * kf[n] * r_k[:, sls[pairs[i][1]]]
                                for n, i in enumerate(idx)], seg_ones, exact=False)
        yield
        for n, i in enumerate(idx):
            kkn = kk[n] / jnp.maximum(jnp.sqrt(sums[n]), L2_EPS)
            bonus[c][i] = sums[len(idx) + n] * cut(v, i)
            ops[c][i] = _wkv_prepare(cut(r, i), kf[n], cut(v, i), kkn, cut(a, i), cut(ld, i),
                                     cut(cum, i), head_masks)

    def recur(c):
        yield from _wkv_recurrence(ops[c], lambda: states[c], head_masks, causal2, bd_mask,
                                   results[c])
        states.append(results[c]['s_new'])

    def finish(c, idx):
        rs = slice(c * CHUNK, (c + 1) * CHUNK)
        y = [results[c]['y'][i] for i in idx]
        mean = [m * (1.0 / HEAD_DIM) for m in _segment_sums(y, seg_ones)]
        yield
        dev = [y[n] - mean[n] for n in range(len(idx))]
        var = [m * (1.0 / HEAD_DIM) for m in _segment_sums([d * d for d in dev], seg_ones)]
        yield
        for n, i in enumerate(idx):
            s, g = pairs[i]
            sl = sls[g]
            yn = dev[n] * lax.rsqrt(var[n] + GN_EPS) * lnx_w[:, sl] + lnx_b[:, sl]
            y_b = (yn + bonus[c][i]) * gate[s][rs, sl]
            z_b = p_ref[s, rs, _ZB + g * LANES:_ZB + (g + 1) * LANES]
            merged_ref[s, rs, sl] = (y_a[s][rs, sl] + jax.nn.sigmoid(z_b) * y_b).astype(merged_ref.dtype)

    base = lambda c: 2 + WKV_STAGES * c
    tasks = []
    for c in range(n_ch):
        for j, idx in enumerate(pieces):
            tasks.append((0 if c == 0 else base(c - 1) + 2 * j, prepare(c, idx)))
    for c in range(n_ch):
        tasks.append((base(c), recur(c)))
        for j, idx in enumerate(pieces):
            tasks.append((base(c + 1) + (2 * j if c + 1 < n_ch else 0), finish(c, idx)))
    _run_tasks(sorted(tasks, key=lambda task: task[0]))
    for i in range(n_pairs):
        state[i] = states[n_ch][i]

    @pl.when(t == n_steps - 1)
    def _():
        for s in seqs:
            nconv_ref[s] = new_conv[s]
            nshift_ref[s] = new_shift[s]
            for h in range(N_HEADS):
                o = (h % HEADS_PER_GROUP) * HEAD_DIM
                nwkv_ref[s, h] = state[s * N_GROUPS + h // HEADS_PER_GROUP,
                                       o:o + HEAD_DIM, o:o + HEAD_DIM]


def _mixer(p2d, conv0, shift0, wkv0, convw, mu, vecs, wwa, g2, batch, seq):
    n_seq = SEQS_PER_STEP if batch % SEQS_PER_STEP == 0 else 1
    n_ch = CHUNKS_PER_STEP if seq % (CHUNKS_PER_STEP * CHUNK) == 0 else 1
    rows = n_ch * CHUNK
    n_steps = seq // rows
    kern = functools.partial(_mixer_kernel, n_steps=n_steps, n_seq=n_seq, n_ch=n_ch)
    const = lambda b, t: (0, 0)
    per_b3 = lambda b, t: (b, 0, 0)
    per_b4 = lambda b, t: (b, 0, 0, 0)
    merged, conv, shift, wkv = pl.pallas_call(
        kern,
        out_shape=(jax.ShapeDtypeStruct((batch, seq, D_MODEL), BF16),
                   jax.ShapeDtypeStruct((batch, CONV_W - 1, D_CONV), F32),
                   jax.ShapeDtypeStruct((batch, 1, D_SHIFT), F32),
                   jax.ShapeDtypeStruct((batch, N_HEADS, HEAD_DIM, HEAD_DIM), F32)),
        grid=(batch // n_seq, n_steps),
        in_specs=[pl.BlockSpec((n_seq, rows, D_PROJ), lambda b, t: (b, t, 0)),
                  pl.BlockSpec((n_seq, CONV_W - 1, D_CONV), per_b3),
                  pl.BlockSpec((n_seq, 1, D_SHIFT), per_b3),
                  pl.BlockSpec((n_seq, N_HEADS, HEAD_DIM, HEAD_DIM), per_b4),
                  pl.BlockSpec((CONV_W, D_CONV), const),
                  pl.BlockSpec((1, D_SHIFT), const),
                  pl.BlockSpec((SUBLANES, D_RWKV), const),
                  pl.BlockSpec((LANES, 2 * D_RWKV), const),
                  pl.BlockSpec((GATE_LORA, D_RWKV), const)],
        out_specs=(pl.BlockSpec((n_seq, rows, D_MODEL), lambda b, t: (b, t, 0)),
                   pl.BlockSpec((n_seq, CONV_W - 1, D_CONV), per_b3),
                   pl.BlockSpec((n_seq, 1, D_SHIFT), per_b3),
                   pl.BlockSpec((n_seq, N_HEADS, HEAD_DIM, HEAD_DIM), per_b4)),
        scratch_shapes=[pltpu.VMEM((n_seq, rows + SUBLANES, D_CONV), F32),
                        pltpu.VMEM((n_seq, rows + SUBLANES, D_SHIFT), F32),
                        pltpu.VMEM((n_seq * N_GROUPS, LANES, LANES), F32)],
        compiler_params=_params(("arbitrary", "arbitrary")),
        name="mixer",
    )(p2d.reshape(batch, seq, D_PROJ), conv0, shift0, wkv0, convw, mu, vecs, wwa, g2)
    return merged.reshape(batch * seq, D_MODEL), conv, shift, wkv


def _silu_mul(gate, up):
    return gate * jax.nn.sigmoid(gate) * up


def _ffn_dense_kernel(x_ref, m_ref, wo_ref, g_ref, wg_ref, wu_ref, wd_ref, o_ref, *, n_f):
    x1 = x_ref[...] + _dot(m_ref[...], wo_ref[...])
    h = _rms(x1, g_ref[...]).astype(BF16)
    tf = wg_ref.shape[1] // n_f
    acc = x1
    for j in range(n_f):
        act = _silu_mul(_dot(h, wg_ref[:, j * tf:(j + 1) * tf]),
                        _dot(h, wu_ref[:, j * tf:(j + 1) * tf])).astype(BF16)
        acc = acc + _dot(act, wd_ref[j * tf:(j + 1) * tf, :])
    o_ref[...] = acc


def _ffn_dense(x2d, merged, wo, g, wg, wu, wd, tm, n_f):
    n, d = x2d.shape
    f = wg.shape[1]
    row = lambda i: (i, 0)
    resident = lambda shape: pl.BlockSpec(shape, lambda i: (0, 0), pipeline_mode=pl.Buffered(1))
    return pl.pallas_call(
        functools.partial(_ffn_dense_kernel, n_f=n_f),
        out_shape=jax.ShapeDtypeStruct((n, d), F32),
        grid=(n // tm,),
        in_specs=[pl.BlockSpec((tm, d), row), pl.BlockSpec((tm, d), row),
                  resident((d, d)), pl.BlockSpec((1, d), lambda i: (0, 0)),
                  resident((d, f)), resident((d, f)), resident((f, d))],
        out_specs=pl.BlockSpec((tm, d), row),
        compiler_params=_params(("arbitrary",)),
        name="ffn_dense",
    )(x2d, merged, wo, g, wg, wu, wd)


def _top2(logits):
    lane_i = lax.broadcasted_iota(jnp.int32, logits.shape, 1)
    lane = lane_i.astype(F32)
    neg = jnp.float32(-jnp.inf)
    lg = jnp.where(lane_i < N_EXPERTS, logits, neg)
    m1 = jnp.max(lg, axis=-1, keepdims=True)
    i1 = jnp.min(jnp.where(lg == m1, lane, float(LANES)), axis=-1, keepdims=True)
    lg2 = jnp.where(lane == i1, neg, lg)
    m2 = jnp.max(lg2, axis=-1, keepdims=True)
    i2 = jnp.min(jnp.where(lg2 == m2, lane, float(LANES)), axis=-1, keepdims=True)
    e2 = jnp.exp(m2 - m1)
    den = 1.0 + e2
    ids = jnp.where(lane_i == 0, i1, jnp.where(lane_i == 1, i2, 0.0)).astype(jnp.int32)
    wts = jnp.where(lane_i == 0, 1.0 / den, jnp.where(lane_i == 1, e2 / den, 0.0))
    return ids, wts


def _router_kernel(xa_ref, ma_ref, xb_ref, mb_ref, wo_ref, g_ref, rw_ref, x1_ref, ids_ref, wts_ref,
                   *, blocks_a, n_sub):
    first = pl.program_id(0) < blocks_a
    rows = x1_ref.shape[0] // n_sub
    subs = [slice(i * rows, (i + 1) * rows) for i in range(n_sub)]
    x1 = [jnp.where(first, xa_ref[sl, :], xb_ref[sl, :])
          + _dot(jnp.where(first, ma_ref[sl, :], mb_ref[sl, :]), wo_ref[...]) for sl in subs]
    logits = [_mm3(_rms(v, g_ref[...]), rw_ref[...]) for v in x1]
    for sl, v, lg in zip(subs, x1, logits):
        x1_ref[sl, :] = v
        ids_ref[sl, :], wts_ref[sl, :] = _top2(lg)


def _router(xa, ma, xb, mb, wo, g, rw, tm):
    (na, d), nb = xa.shape, xb.shape[0]
    blocks_a = na // tm
    n = na + nb
    row = lambda i: (i, 0)
    row_a = lambda i: (jnp.minimum(i, blocks_a - 1), 0)
    row_b = lambda i: (jnp.maximum(i - blocks_a, 0), 0)
    const = lambda i: (0, 0)
    return pl.pallas_call(
        functools.partial(_router_kernel, blocks_a=blocks_a, n_sub=ROUTER_SUBBLOCKS),
        out_shape=(jax.ShapeDtypeStruct((n, d), F32),
                   jax.ShapeDtypeStruct((n, LANES), jnp.int32),
                   jax.ShapeDtypeStruct((n, LANES), F32)),
        grid=(n // tm,),
        in_specs=[pl.BlockSpec((tm, d), row_a), pl.BlockSpec((tm, d), row_a),
                  pl.BlockSpec((tm, d), row_b), pl.BlockSpec((tm, d), row_b),
                  pl.BlockSpec((d, d), const), pl.BlockSpec((1, d), const),
                  pl.BlockSpec((d, LANES), const)],
        out_specs=(pl.BlockSpec((tm, d), row), pl.BlockSpec((tm, LANES), row),
                   pl.BlockSpec((tm, LANES), row)),
        compiler_params=_params(("arbitrary",)),
        name="router",
    )(xa, ma, xb, mb, wo, g, rw)


def _experts_kernel(te_ref, nv_ref, inv_prev_ref, inv_ref, inv_next_ref, x1_hbm, g_ref,
                    wg_ref, wu_ref, wd_ref, y_hbm, xbuf, obuf, gather_sem, scatter_sem,
                    *, tm, n_tok, f_split):
    t = pl.program_id(0)
    n_used = nv_ref[0]
    cur = t & 1
    nxt = 1 - cur

    def gather_row(idx_ref, buf, r, priority=0):
        tok = jnp.maximum(idx_ref[0, 0, r], 0) >> 1
        pltpu.make_async_copy(x1_hbm.at[pl.ds(tok, 1)], xbuf.at[buf, pl.ds(r, 1)],
                              gather_sem.at[buf]).start(priority=priority)

    def gather_loop(idx_ref, buf):
        def body(r, c):
            gather_row(idx_ref, buf, r)
            return c
        lax.fori_loop(0, tm, body, 0, unroll=8)

    def wait_gather(buf):
        pltpu.make_async_copy(x1_hbm.at[pl.ds(0, tm)], xbuf.at[buf], gather_sem.at[buf]).wait()

    def scatter_row(idx_ref, buf, r, real, priority=0):
        a = idx_ref[0, 0, r]
        row = jnp.where((a >= 0) & real, (a & 1) * n_tok + (a >> 1), 2 * n_tok + r)
        pltpu.make_async_copy(obuf.at[buf, pl.ds(r, 1)], y_hbm.at[pl.ds(row, 1)],
                              scatter_sem).start(priority=priority)

    def wait_scatter(buf):
        pltpu.make_async_copy(obuf.at[buf], y_hbm.at[pl.ds(0, tm)], scatter_sem).wait()

    @pl.when(t == 0)
    def _():
        obuf[...] = jnp.zeros_like(obuf)
        gather_loop(inv_ref, 0)

    wait_gather(cur)

    @pl.when(t < n_used)
    def _():
        h = _rms(xbuf[cur], g_ref[...]).astype(BF16)
        fs = wg_ref.shape[2] // f_split
        rows_per = -(-tm // max(1, (2 * f_split) // 3))
        acc = None
        for i in range(f_split):
            act = _silu_mul(_dot(h, wg_ref[0, :, i * fs:(i + 1) * fs]),
                            _dot(h, wu_ref[0, :, i * fs:(i + 1) * fs])).astype(BF16)
            part = _dot(act, wd_ref[0, i * fs:(i + 1) * fs, :])
            acc = part if acc is None else acc + part
            for r in range(i * rows_per, min((i + 1) * rows_per, tm)):
                gather_row(inv_next_ref, nxt, r, priority=r % 2)
                scatter_row(inv_prev_ref, nxt, r, t > 0, priority=r % 2)
        wait_scatter(nxt)
        obuf[cur] = acc

        @pl.when(t == n_used - 1)
        def _():
            def body(r, c):
                scatter_row(inv_ref, cur, r, True)
                return c
            lax.fori_loop(0, tm, body, 0, unroll=8)
            wait_scatter(cur)

    @pl.when(t >= n_used)
    def _():
        gather_loop(inv_next_ref, nxt)

    @pl.when(t == pl.num_programs(0) - 1)
    def _():
        wait_gather(nxt)


def _experts(x1, g, wg, wu, wd, tile_expert, n_used, inv, tm):
    n, d = x1.shape
    n_tiles = tile_expert.shape[0]
    f = wg.shape[2]
    expert_w = lambda t, te, nv: (jnp.minimum(te[t], N_EXPERTS - 1), 0, 0)
    single = pl.Buffered(1)
    inv3 = inv.reshape(n_tiles, 1, tm)
    idx_spec = lambda shift: pl.BlockSpec(
        (1, 1, tm), lambda t, te, nv: (jnp.clip(t + shift, 0, n_tiles - 1), 0, 0),
        memory_space=pltpu.SMEM)
    grid_spec = pltpu.PrefetchScalarGridSpec(
        num_scalar_prefetch=2,
        grid=(n_tiles,),
        in_specs=[idx_spec(-1), idx_spec(0), idx_spec(1),
                  pl.BlockSpec(memory_space=pl.ANY),
                  pl.BlockSpec((1, d), lambda t, te, nv: (0, 0)),
                  pl.BlockSpec((1, d, f), expert_w, pipeline_mode=single),
                  pl.BlockSpec((1, d, f), expert_w, pipeline_mode=single),
                  pl.BlockSpec((1, f, d), expert_w, pipeline_mode=single)],
        out_specs=pl.BlockSpec(memory_space=pl.ANY),
        scratch_shapes=[pltpu.VMEM((2, tm, d), F32), pltpu.VMEM((2, tm, d), F32),
                        pltpu.SemaphoreType.DMA((2,)), pltpu.SemaphoreType.DMA(())])
    return pl.pallas_call(
        functools.partial(_experts_kernel, tm=tm, n_tok=n, f_split=f // (2 * LANES)),
        out_shape=jax.ShapeDtypeStruct((2 * n + tm, d), F32),
        grid_spec=grid_spec,
        compiler_params=_params(("arbitrary",)),
        name="experts",
    )(tile_expert, n_used, inv3, inv3, inv3, x1, g, wg, wu, wd)


def _combine_kernel(x1_ref, y0_ref, y1_ref, wts_ref, fin_ref, oa_ref, ob_ref, *, blocks_a):
    w = wts_ref[...]
    moe = w[:, 0:1] * y0_ref[...] + w[:, 1:2] * y1_ref[...]
    out = _rms(x1_ref[...] + moe, fin_ref[...])
    first = pl.program_id(0) < blocks_a

    @pl.when(first)
    def _():
        oa_ref[...] = out

    @pl.when(jnp.logical_not(first))
    def _():
        ob_ref[...] = out


def _combine(x1, y2, wts, fin, na, tm):
    n, d = x1.shape
    n_blocks = n // tm
    blocks_a = na // tm
    row = lambda i: (i, 0)
    return pl.pallas_call(
        functools.partial(_combine_kernel, blocks_a=blocks_a),
        out_shape=(jax.ShapeDtypeStruct((na, d), F32), jax.ShapeDtypeStruct((n - na, d), F32)),
        grid=(n_blocks,),
        in_specs=[pl.BlockSpec((tm, d), row),
                  pl.BlockSpec((tm, d), row),
                  pl.BlockSpec((tm, d), lambda i: (i + n_blocks, 0)),
                  pl.BlockSpec((tm, LANES), row),
                  pl.BlockSpec((1, d), lambda i: (0, 0))],
        out_specs=(pl.BlockSpec((tm, d), lambda i: (jnp.minimum(i, blocks_a - 1), 0)),
                   pl.BlockSpec((tm, d), lambda i: (jnp.maximum(i - blocks_a, 0), 0))),
        compiler_params=_params(("arbitrary",)),
        name="combine",
    )(x1, y2, y2, wts, fin)


def _routing_plan(ids, tm):
    n = ids.shape[0]
    e_flat = ids[:, :2].reshape(-1)
    onehot = (e_flat[:, None] == jnp.arange(N_EXPERTS, dtype=jnp.int32)[None, :]).astype(jnp.int32)
    csum = jnp.cumsum(onehot, axis=0)
    counts = csum[-1]
    padded = (counts + tm - 1) // tm * tm
    ends = jnp.cumsum(padded)
    pos = jnp.sum(onehot * (csum - 1 + (ends - padded)[None, :]), axis=1)
    total = 2 * n + N_EXPERTS * tm
    inv = jnp.full((total,), -1, jnp.int32).at[pos].set(jnp.arange(2 * n, dtype=jnp.int32))
    tile_start = jnp.arange(total // tm, dtype=jnp.int32) * tm
    tile_expert = jnp.sum((tile_start[:, None] >= ends[None, :]).astype(jnp.int32), axis=1)
    return tile_expert, (ends[-1:] // tm).astype(jnp.int32), inv


def _common_tile(na, nb, want):
    tm = want
    while na % tm or nb % tm:
        tm //= 2
    return tm


def _ffn_moe(xa, ma, xb, mb, wo, g, rw, wg, wu, wd, fin):
    na, nb = xa.shape[0], xb.shape[0]
    n = na + nb
    big_experts = 2 * n >= 4 * 512 * N_EXPERTS
    tm_expert = _row_tile(2 * n, 512 if big_experts else 256)
    x1, ids, wts = _router(xa, ma, xb, mb, wo, g, rw, _common_tile(na, nb, 1024))
    tile_expert, n_used, inv = _routing_plan(ids, tm_expert)
    y2 = _experts(x1, g, wg, wu, wd, tile_expert, n_used, inv, tm_expert)
    return _combine(x1, y2, wts, fin, na, _common_tile(na, nb, 512))


def _prep_layer_weights(l, w_in, conv_w, shift_mu, decay_w0, decay_w2, aaa_a0, aaa_a2, gate_g2,
                        key_k, key_a, bonus_r_k, lnx_w, lnx_b, w_out):
    zero = jnp.zeros((DECAY_LORA, D_RWKV), F32)
    wwa = jnp.concatenate([jnp.concatenate([decay_w2[l], zero], axis=1),
                           jnp.concatenate([zero, aaa_a2[l]], axis=1)], axis=0).astype(BF16)
    vecs = jnp.stack([decay_w0[l], aaa_a0[l], key_k[l], key_a[l], bonus_r_k[l], lnx_w[l], lnx_b[l],
                      jnp.zeros((D_RWKV,), F32)])
    return dict(w_in=w_in[l].astype(BF16), convw=conv_w[l], mu=shift_mu[l][None], vecs=vecs,
                wwa=wwa, g2=gate_g2[l].astype(BF16), w_out=w_out[l].astype(BF16))


def _row_tile(n, want):
    return want if n % want == 0 else n


def _tiles(n):
    return _row_tile(n, 512), _row_tile(n, 256)


def _mix(x2d, batch, seq, lw, conv0, shift0, wkv0):
    _, tm_proj = _tiles(batch * seq)
    p = _norm_proj(x2d, lw['norm'], lw['w_in'], tm_proj, PROJ_CHUNKS)
    merged, c, s, w = _mixer(p, conv0, shift0[:, None], wkv0, lw['convw'], lw['mu'], lw['vecs'],
                             lw['wwa'], lw['g2'], batch, seq)
    return merged, c, s[:, 0], w


def _lower_trunk(x, conv_st, shift_st, wkv_st, layers, ffn_norm, dense):
    batch, seq, d = x.shape
    n = batch * seq
    tm, _ = _tiles(n)
    x2d = x.reshape(n, d)
    merged, c0, s0, w0 = _mix(x2d, batch, seq, layers[0], conv_st[0], shift_st[0], wkv_st[0])
    wg, wu, wd = dense
    x2d = _ffn_dense(x2d, merged, layers[0]['w_out'], ffn_norm[0][None], wg, wu, wd, tm,
                     FFN_CHUNKS)
    merged, c1, s1, w1 = _mix(x2d, batch, seq, layers[1], conv_st[1], shift_st[1], wkv_st[1])
    return x2d, merged, jnp.stack([c0, c1]), jnp.stack([s0, s1]), jnp.stack([w0, w1])


def kernel(x_prompt, x_sample, state_conv, state_shift, state_wkv, mix_norm, w_in, conv_w, shift_mu,
           decay_w0, decay_w2, aaa_a0, aaa_a2, gate_g2, key_k, key_a, bonus_r_k, lnx_w, lnx_b, w_out,
           ffn_norm, ffn_w_gate, ffn_w_up, ffn_w_down, router_w, moe_w_gate, moe_w_up, moe_w_down,
           final_norm):
    depth = w_in.shape[0]
    assert depth == 2 and ffn_w_gate.shape[0] == 1 and moe_w_gate.shape[0] == 1
    layers = []
    for l in range(depth):
        lw = _prep_layer_weights(l, w_in, conv_w, shift_mu, decay_w0, decay_w2, aaa_a0, aaa_a2,
                                 gate_g2, key_k, key_a, bonus_r_k, lnx_w, lnx_b, w_out)
        lw['norm'] = mix_norm[l][None]
        layers.append(lw)
    dense = (ffn_w_gate[0].astype(BF16), ffn_w_up[0].astype(BF16), ffn_w_down[0].astype(BF16))
    rw = jnp.pad(router_w[0], ((0, 0), (0, LANES - N_EXPERTS)))
    moe = (rw, moe_w_gate[0].astype(BF16), moe_w_up[0].astype(BF16), moe_w_down[0].astype(BF16))

    b = x_prompt.shape[0]
    zero_conv = jnp.zeros((depth, b) + state_conv.shape[2:], state_conv.dtype)
    zero_shift = jnp.zeros((depth, b) + state_shift.shape[2:], state_shift.dtype)
    zero_wkv = jnp.zeros((depth, b) + state_wkv.shape[2:], state_wkv.dtype)
    run = functools.partial(_lower_trunk, layers=layers, ffn_norm=ffn_norm, dense=dense)
    x_s, m_s, conv_s, shift_s, wkv_s = run(x_sample, state_conv, state_shift, state_wkv)
    x_p, m_p, conv_p, shift_p, wkv_p = run(x_prompt, zero_conv, zero_shift, zero_wkv)
    y_p, y_s = _ffn_moe(x_p, m_p, x_s, m_s, layers[1]['w_out'], ffn_norm[1][None], *moe,
                        final_norm[None])
    return (y_p.reshape(x_prompt.shape), y_s.reshape(x_sample.shape),
            conv_p, shift_p, wkv_p, conv_s, shift_s, wkv_s)
```
